```python
import jax, jax.numpy as jnp
from jax import lax
import numpy as np

D_MODEL = 4096
BATCH = 8
SEQ = 2048
DEPTH = 2

RET_HEADS = 8
RET_DK = 256
RET_DV = 256
RET_CHUNK = 128
NSA_HEADS = 16
NSA_KV_HEADS = 4
NSA_DH = 128
CMP_BLOCK = 32
CMP_STRIDE = 16
CMP_HIDDEN = 128
SEL_BLOCK = 64
SEL_TOPK = 16
SEL_LOCAL = 2
SEL_QBLOCK = 16
WINDOW = 512
WIN_QBLOCK = 128
D_FF = 4 * D_MODEL
EPS = 1e-6
NEG_INF = -1e30
FORCED_SCORE = 1e4

RET_W = RET_HEADS * RET_DV
NSA_W = NSA_HEADS * NSA_DH
KV_W = NSA_KV_HEADS * NSA_DH
SPLITS = (RET_HEADS * RET_DK,
          RET_HEADS * RET_DK,
          RET_W,
          RET_W,
          NSA_W,
          6 * KV_W,
          3 * NSA_HEADS,
          D_MODEL,
          D_MODEL)
D_IN = sum(SPLITS)

kernel_name = "hybrid_retention_nsa_sqrelu_sandwich"


def rmsnorm(x, gain):
    xf = x.astype(jnp.float32)
    y = xf * lax.rsqrt(jnp.mean(xf * xf, axis=-1, keepdims=True) + EPS)
    return (y * gain.astype(jnp.float32)).astype(x.dtype)


def retention(q, k, v, g):
    B, T, H, DK = q.shape
    DV = v.shape[-1]
    C = RET_CHUNK
    N = T // C
    f32 = jnp.float32
    log_gamma = jnp.log1p(-jnp.exp2(-5.0 - jnp.arange(H, dtype=f32)))
    pos = jnp.arange(C, dtype=f32)
    rel = pos[:, None] - pos[None, :]
    d_intra = jnp.where(rel >= 0, jnp.exp(log_gamma[:, None, None] * jnp.maximum(rel, 0.0)), 0.0)
    d_query = jnp.exp(log_gamma[:, None] * (pos + 1.0))[..., None]
    d_key = jnp.exp(log_gamma[:, None] * (C - 1.0 - pos))[..., None]
    d_chunk = jnp.exp(log_gamma * C)[:, None, None]

    def to_chunks(a):
        return a.astype(f32).reshape(B, N, C, H, a.shape[-1]).transpose(1, 0, 3, 2, 4)

    qc, kc, vc = to_chunks(q), to_chunks(k * DK ** -0.5), to_chunks(v)

    def step(state, inp):
        qi, ki, vi = inp
        s = jnp.einsum('bhid,bhjd->bhij', qi, ki) * d_intra
        o = (jnp.einsum('bhij,bhje->bhie', s, vi)
             + jnp.einsum('bhid,bhde->bhie', qi, state) * d_query)
        state = state * d_chunk + jnp.einsum('bhjd,bhje->bhde', ki * d_key, vi)
        return state, o

    s0 = jnp.zeros((B, H, DK, DV), f32)
    _, o = lax.scan(step, s0, (qc, kc, vc))
    o = o.transpose(1, 0, 3, 2, 4).reshape(B, T, H, DV)
    mu = jnp.mean(o, axis=-1, keepdims=True)
    var = jnp.mean(jnp.square(o - mu), axis=-1, keepdims=True)
    o = (o - mu) * lax.rsqrt(var + EPS)
    y = jax.nn.silu(g.astype(f32)) * o
    return y.reshape(B, T, H * DV).astype(q.dtype)


def compress_blocks(x, pos_emb, w1, w2):
    B, T, KVH, DH = x.shape
    n_cmp = (T - CMP_BLOCK) // CMP_STRIDE + 1
    idx = CMP_STRIDE * jnp.arange(n_cmp)[:, None] + jnp.arange(CMP_BLOCK)[None, :]
    blocks = x[:, idx] + pos_emb[:, None, :]
    flat = blocks.transpose(0, 1, 3, 2, 4).reshape(B, n_cmp, KVH, CMP_BLOCK * DH)
    return jax.nn.silu(flat @ w1) @ w2


def nsa(q, k_cmp, v_cmp, k_sel, v_sel, k_win, v_win, gate_logits,
        pos_k, w1_k, w2_k, pos_v, w1_v, w2_v):
    B, T, H, DH = q.shape
    KVH = k_cmp.shape[2]
    G = H // KVH
    f32 = jnp.float32
    scale = DH ** -0.5
    slopes = jnp.exp2(-8.0 * (jnp.arange(H, dtype=f32) + 1.0) / H).reshape(KVH, G)
    qg = q.reshape(B, T, KVH, G, DH)
    t_pos = jnp.arange(T)

    kc = compress_blocks(k_cmp, pos_k, w1_k, w2_k)
    vc = compress_blocks(v_cmp, pos_v, w1_v, w2_v)
    n_cmp = kc.shape[1]
    cmp_start = CMP_STRIDE * jnp.arange(n_cmp)
    dist_c = t_pos[:, None] - (cmp_start + CMP_BLOCK - 1)[None, :]
    valid_c = dist_c >= 0
    s_c = (jnp.einsum('btgrd,bngd->bgrtn', qg, kc).astype(f32) * scale
           - slopes[:, :, None, None] * dist_c)
    p_c = jnp.where(valid_c, jax.nn.softmax(jnp.where(valid_c, s_c, NEG_INF), axis=-1), 0.0)
    o_cmp = jnp.einsum('bgrtn,bngd->btgrd', p_c, vc.astype(f32))

    n_sel = T // SEL_BLOCK
    k_top = min(SEL_TOPK, n_sel)
    sel_start = SEL_BLOCK * jnp.arange(n_sel)
    overlap = jnp.clip(jnp.minimum(cmp_start[:, None] + CMP_BLOCK, sel_start[None, :] + SEL_BLOCK)
                       - jnp.maximum(cmp_start[:, None], sel_start[None, :]), 0).astype(f32) / CMP_BLOCK
    imp = jnp.einsum('bgrtn,nj->bgtj', p_c, overlap)
    blk = jnp.arange(n_sel)[None, :]
    cur = (t_pos // SEL_BLOCK)[:, None]
    future = sel_start[None, :] > t_pos[:, None]
    forced = (blk == 0) | ((cur - blk >= 0) & (cur - blk < SEL_LOCAL))
    imp = jnp.where(forced, FORCED_SCORE, jnp.where(future, -1.0, imp))
    _, sel_idx = lax.top_k(imp, k_top)

    ksb = k_sel.reshape(B, n_sel, SEL_BLOCK, KVH, DH).transpose(0, 3, 1, 2, 4)
    vsb = v_sel.reshape(B, n_sel, SEL_BLOCK, KVH, DH).transpose(0, 3, 1, 2, 4)
    nq = T // SEL_QBLOCK
    q_chunks = qg.reshape(B, nq, SEL_QBLOCK, KVH, G, DH).transpose(1, 0, 2, 3, 4, 5)
    i_chunks = sel_idx.reshape(B, KVH, nq, SEL_QBLOCK, k_top).transpose(2, 0, 1, 3, 4)
    t_chunks = t_pos.reshape(nq, SEL_QBLOCK)
    bi = jnp.arange(B)[:, None, None, None]
    gi = jnp.arange(KVH)[None, :, None, None]
    offs = jnp.arange(SEL_BLOCK)

    def sel_step(args):
        qc, ic, tc = args
        kg = ksb[bi, gi, ic]
        vg = vsb[bi, gi, ic]
        dist = tc[None, None, :, None, None] - (ic[..., None] * SEL_BLOCK + offs)
        valid = (dist >= 0)[:, :, None]
        s = (jnp.einsum('bqgrd,bgqksd->bgrqks', qc, kg).astype(f32) * scale
             - slopes[None, :, :, None, None, None] * dist[:, :, None])
        p = jax.nn.softmax(jnp.where(valid, s, NEG_INF), axis=(-2, -1))
        return jnp.einsum('bgrqks,bgqksd->bqgrd', p, vg.astype(f32))

    o_sel = lax.map(sel_step, (q_chunks, i_chunks, t_chunks))
    o_sel = o_sel.transpose(1, 0, 2, 3, 4, 5).reshape(B, T, KVH, G, DH)

    nw = T // WIN_QBLOCK
    span = WINDOW + WIN_QBLOCK
    kwp = jnp.pad(k_win, ((0, 0), (WINDOW, 0), (0, 0), (0, 0)))
    vwp = jnp.pad(v_win, ((0, 0), (WINDOW, 0), (0, 0), (0, 0)))
    qw = qg.reshape(B, nw, WIN_QBLOCK, KVH, G, DH).transpose(1, 0, 2, 3, 4, 5)
    rows = jnp.arange(WIN_QBLOCK)
    cols = jnp.arange(span)

    def win_step(args):
        qb, i = args
        start = i * WIN_QBLOCK
        kb = lax.dynamic_slice_in_dim(kwp, start, span, axis=1)
        vb = lax.dynamic_slice_in_dim(vwp, start, span, axis=1)
        tq = start + rows
        tk = start - WINDOW + cols
        dist = tq[:, None] - tk[None, :]
        valid = (dist >= 0) & (dist < WINDOW) & (tk[None, :] >= 0)
        s = (jnp.einsum('bqgrd,bkgd->bgrqk', qb, kb).astype(f32) * scale
             - slopes[:, :, None, None] * dist)
        p = jax.nn.softmax(jnp.where(valid, s, NEG_INF), axis=-1)
        return jnp.einsum('bgrqk,bkgd->bqgrd', p, vb.astype(f32))

    o_win = lax.map(win_step, (qw, jnp.arange(nw)))
    o_win = o_win.transpose(1, 0, 2, 3, 4, 5).reshape(B, T, KVH, G, DH)

    gates = jax.nn.sigmoid(gate_logits.astype(f32)).reshape(B, T, KVH, G, 3, 1)
    o = gates[..., 0, :] * o_cmp + gates[..., 1, :] * o_sel + gates[..., 2, :] * o_win
    return o.reshape(B, T, H * DH).astype(q.dtype)


def setup_inputs(seed: int = 0) -> dict:
    key = jax.random.key(seed)
    ks = jax.random.split(key, 17)
    f32 = jnp.float32

    def nrm(k, shape, fan_in):
        return jax.random.normal(k, shape, f32) * (fan_in ** -0.5)

    def gain(k):
        return 1.0 + 0.05 * jax.random.normal(k, (DEPTH, D_MODEL), f32)

    return {
        "x": jax.random.normal(ks[0], (BATCH, SEQ, D_MODEL), f32),
        "mix_norm_pre": gain(ks[1]),
        "w_in": nrm(ks[2], (DEPTH, D_MODEL, D_IN), D_MODEL),
        "cmp_pos_k": 0.1 * jax.random.normal(ks[3], (DEPTH, CMP_BLOCK, NSA_DH), f32),
        "cmp_w1_k": nrm(ks[4], (DEPTH, CMP_BLOCK * NSA_DH, CMP_HIDDEN), CMP_BLOCK * NSA_DH),
        "cmp_w2_k": nrm(ks[5], (DEPTH, CMP_HIDDEN, NSA_DH), CMP_HIDDEN),
        "cmp_pos_v": 0.1 * jax.random.normal(ks[6], (DEPTH, CMP_BLOCK, NSA_DH), f32),
        "cmp_w1_v": nrm(ks[7], (DEPTH, CMP_BLOCK * NSA_DH, CMP_HIDDEN), CMP_BLOCK * NSA_DH),
        "cmp_w2_v": nrm(ks[8], (DEPTH, CMP_HIDDEN, NSA_DH), CMP_HIDDEN),
        "w_ret_up": nrm(ks[9], (DEPTH, RET_W, D_MODEL), RET_W),
        "w_nsa_up": nrm(ks[10], (DEPTH, NSA_W, D_MODEL), NSA_W),
        "w_out": nrm(ks[11], (DEPTH, D_MODEL, D_MODEL), D_MODEL),
        "mix_norm_post": gain(ks[12]),
        "mlp_norm_pre": gain(ks[13]),
        "w_mlp_in": nrm(ks[14], (DEPTH, D_MODEL, D_FF), D_MODEL),
        "w_mlp_out": nrm(ks[15], (DEPTH, D_FF, D_MODEL), D_FF),
        "mlp_norm_post": gain(ks[16]),
    }


def reference(x, mix_norm_pre, w_in, cmp_pos_k, cmp_w1_k, cmp_w2_k, cmp_pos_v, cmp_w1_v,
              cmp_w2_v, w_ret_up, w_nsa_up, w_out, mix_norm_post, mlp_norm_pre, w_mlp_in,
              w_mlp_out, mlp_norm_post):
    B, T, _ = x.shape
    cuts = np.cumsum(np.array(SPLITS))[:-1].tolist()
    for l in range(DEPTH):
        h = rmsnorm(x, mix_norm_pre[l])
        q_r, k_r, v_r, g_r, q_n, kv_n, gate_n, gate_ret, gate_nsa = jnp.split(h @ w_in[l], cuts, axis=-1)
        k_c, v_c, k_s, v_s, k_w, v_w = [a.reshape(B, T, NSA_KV_HEADS, NSA_DH)
                                        for a in jnp.split(kv_n, 6, axis=-1)]
        y_ret = retention(q_r.reshape(B, T, RET_HEADS, RET_DK),
                          k_r.reshape(B, T, RET_HEADS, RET_DK),
                          v_r.reshape(B, T, RET_HEADS, RET_DV),
                          g_r.reshape(B, T, RET_HEADS, RET_DV))
        y_nsa = nsa(q_n.reshape(B, T, NSA_HEADS, NSA_DH), k_c, v_c, k_s, v_s, k_w, v_w, gate_n,
                    cmp_pos_k[l], cmp_w1_k[l], cmp_w2_k[l], cmp_pos_v[l], cmp_w1_v[l], cmp_w2_v[l])
        merged = (jax.nn.sigmoid(gate_ret) * (y_ret @ w_ret_up[l])
                  + jax.nn.sigmoid(gate_nsa) * (y_nsa @ w_nsa_up[l]))
        x = x + rmsnorm(merged @ w_out[l], mix_norm_post[l])
        h = rmsnorm(x, mlp_norm_pre[l])
        u = jnp.square(jax.nn.relu(h @ w_mlp_in[l]))
        x = x + rmsnorm(u @ w_mlp_out[l], mlp_norm_post[l])
    return x
```

```python
import functools

import numpy as np
import jax
import jax.numpy as jnp
from jax import lax
from jax.experimental import pallas as pl
from jax.experimental.pallas import tpu as pltpu

RET_HEADS = 8
RET_CHUNK = 128
NSA_HEADS = 16
NSA_KV_HEADS = 4
NSA_DH = 128
CMP_BLOCK = 32
CMP_STRIDE = 16
SEL_BLOCK = 64
SEL_TOPK = 16
SEL_LOCAL = 2
WINDOW = 512
EPS = 1e-6
NEG_INF = -1e30
FORCED_SCORE = 1e4

NSA_GROUP = NSA_HEADS // NSA_KV_HEADS
SEL_SHIFT = SEL_BLOCK.bit_length() - 1
LANES = 128
VMEM_LIMIT = 56 * 1024 * 1024

F32 = jnp.float32
BF16 = jnp.bfloat16


def _params(*semantics):
    return pltpu.CompilerParams(dimension_semantics=semantics, vmem_limit_bytes=VMEM_LIMIT)


def _dot(a, b):
    return jnp.dot(a, b, preferred_element_type=F32)


def _dot_nt(a, b):
    return lax.dot_general(a, b, (((1,), (1,)), ((), ())), preferred_element_type=F32)


def _dot_tn(a, b):
    return lax.dot_general(a, b, (((0,), (0,)), ((), ())), preferred_element_type=F32)


def _rmsnorm_kernel(x_ref, g_ref, o_ref):
    x = x_ref[...]
    ms = jnp.mean(x * x, axis=-1, keepdims=True)
    o_ref[...] = (x * lax.rsqrt(ms + EPS) * g_ref[...]).astype(o_ref.dtype)


def _rmsnorm(x, gain, tm=256):
    m, d = x.shape
    return pl.pallas_call(
        _rmsnorm_kernel,
        grid=(m // tm,),
        in_specs=[pl.BlockSpec((tm, d), lambda i: (i, 0)),
                  pl.BlockSpec((1, d), lambda i: (0, 0))],
        out_specs=pl.BlockSpec((tm, d), lambda i: (i, 0)),
        out_shape=jax.ShapeDtypeStruct((m, d), BF16),
        compiler_params=_params("parallel"),
        name="rmsnorm",
    )(x, gain.reshape(1, d))


def _post_kernel(y_ref, x_ref, gp_ref, gn_ref, xo_ref, ho_ref):
    y = y_ref[...].astype(F32)
    yn = y * lax.rsqrt(jnp.mean(y * y, axis=-1, keepdims=True) + EPS) * gp_ref[...]
    xn = x_ref[...] + yn
    xo_ref[...] = xn
    ho_ref[...] = (xn * lax.rsqrt(jnp.mean(xn * xn, axis=-1, keepdims=True) + EPS)
                   * gn_ref[...]).astype(ho_ref.dtype)


def _post_last_kernel(y_ref, x_ref, gp_ref, xo_ref):
    y = y_ref[...].astype(F32)
    yn = y * lax.rsqrt(jnp.mean(y * y, axis=-1, keepdims=True) + EPS) * gp_ref[...]
    xo_ref[...] = x_ref[...] + yn


def _post(y, x, g_post, g_next, tm=256):
    m, d = x.shape
    row = pl.BlockSpec((tm, d), lambda i: (i, 0))
    vec = pl.BlockSpec((1, d), lambda i: (0, 0))
    if g_next is None:
        return pl.pallas_call(
            _post_last_kernel, grid=(m // tm,),
            in_specs=[row, row, vec], out_specs=row,
            out_shape=jax.ShapeDtypeStruct((m, d), F32),
            compiler_params=_params("parallel"), name="post_last",
        )(y, x, g_post.reshape(1, d)), None
    return pl.pallas_call(
        _post_kernel, grid=(m // tm,),
        in_specs=[row, row, vec, vec], out_specs=[row, row],
        out_shape=[jax.ShapeDtypeStruct((m, d), F32), jax.ShapeDtypeStruct((m, d), BF16)],
        compiler_params=_params("parallel"), name="post",
    )(y, x, g_post.reshape(1, d), g_next.reshape(1, d))


def _matmul_kernel(a_ref, b_ref, o_ref, *, relu2):
    acc = _dot(a_ref[...], b_ref[...])
    if relu2:
        acc = jnp.square(jnp.maximum(acc, 0.0))
    o_ref[...] = acc.astype(o_ref.dtype)


def _matmul(a, b, out_dtype, tm=1024, tn=1024, relu2=False, split_cols=False, name="matmul"):
    m, k = a.shape
    n = b.shape[1]
    tm, tn = min(tm, m), min(tn, n)
    if split_cols:
        out_shape = jax.ShapeDtypeStruct((n // tn, m, tn), out_dtype)
        out_spec = pl.BlockSpec((None, tm, tn), lambda i, j: (j, i, 0))
    else:
        out_shape = jax.ShapeDtypeStruct((m, n), out_dtype)
        out_spec = pl.BlockSpec((tm, tn), lambda i, j: (i, j))
    return pl.pallas_call(
        functools.partial(_matmul_kernel, relu2=relu2),
        grid=(m // tm, n // tn),
        in_specs=[pl.BlockSpec((tm, k), lambda i, j: (i, 0)),
                  pl.BlockSpec((k, tn), lambda i, j: (0, j))],
        out_specs=out_spec, out_shape=out_shape,
        compiler_params=_params("parallel", "arbitrary"), name=name,
    )(a, b)


def _matmul_ktiled_kernel(a_ref, b_ref, o_ref):
    @pl.when(pl.program_id(2) == 0)
    def _():
        o_ref[...] = jnp.zeros_like(o_ref)

    o_ref[...] += _dot(a_ref[...], b_ref[...])


def _matmul_ktiled(a, b, tm=1024, tn=1024, tk=2048, name="matmul_k"):
    m, k = a.shape
    n = b.shape[1]
    tm, tn, tk = min(tm, m), min(tn, n), min(tk, k)
    return pl.pallas_call(
        _matmul_ktiled_kernel,
        grid=(m // tm, n // tn, k // tk),
        in_specs=[pl.BlockSpec((tm, tk), lambda i, j, kk: (i, kk)),
                  pl.BlockSpec((tk, tn), lambda i, j, kk: (kk, j))],
        out_specs=pl.BlockSpec((tm, tn), lambda i, j, kk: (i, j)),
        out_shape=jax.ShapeDtypeStruct((m, n), F32),
        compiler_params=_params("parallel", "parallel", "arbitrary"), name=name,
    )(a, b)


def _merge_kernel(yr_ref, yn_ref, h_ref, wru_ref, wnu_ref, wgr_ref, wgn_ref, o_ref):
    h = h_ref[...]
    a = _dot(yr_ref[...], wru_ref[...])
    b = _dot(yn_ref[...], wnu_ref[...])
    gr = jax.nn.sigmoid(_dot(h, wgr_ref[...]))
    gn = jax.nn.sigmoid(_dot(h, wgn_ref[...]))
    o_ref[...] = (gr * a + gn * b).astype(o_ref.dtype)


def _merge(y_ret, y_nsa, h, w_ru, w_nu, w_gr, w_gn, tm=512, tn=512):
    m, d = h.shape
    kr, kn = y_ret.shape[1], y_nsa.shape[1]
    n = w_ru.shape[1]
    tm, tn = min(tm, m), min(tn, n)
    lhs = lambda kk: pl.BlockSpec((tm, kk), lambda i, j: (i, 0))
    rhs = lambda kk: pl.BlockSpec((kk, tn), lambda i, j: (0, j))
    return pl.pallas_call(
        _merge_kernel, grid=(m // tm, n // tn),
        in_specs=[lhs(kr), lhs(kn), lhs(d), rhs(kr), rhs(kn), rhs(d), rhs(d)],
        out_specs=pl.BlockSpec((tm, tn), lambda i, j: (i, j)),
        out_shape=jax.ShapeDtypeStruct((m, n), BF16),
        compiler_params=_params("parallel", "arbitrary"), name="merge",
    )(y_ret, y_nsa, h, w_ru, w_nu, w_gr, w_gn)


def _retention_kernel(q_ref, k_ref, v_ref, g_ref, di_ref, dq_ref, dk_ref, dc_ref, o_ref,
                      state_ref, *, n_chunks, chunk, k_scale):
    state_ref[...] = jnp.zeros_like(state_ref)
    d_intra = di_ref[...]
    d_query = dq_ref[...]
    d_key = dk_ref[...]
    d_chunk = dc_ref[0:1, :]

    def body(n, carry):
        rows = pl.ds(pl.multiple_of(n * chunk, chunk), chunk)
        q = q_ref[rows, :]
        kf = k_ref[rows, :].astype(F32) * k_scale
        v = v_ref[rows, :]
        s = _dot_nt(q, kf.astype(BF16)) * d_intra
        state = state_ref[...]
        o = _dot(s.astype(BF16), v) + _dot(q, state.astype(BF16)) * d_query
        state_ref[...] = state * d_chunk + _dot_tn((kf * d_key).astype(BF16), v)
        mu = jnp.mean(o, axis=-1, keepdims=True)
        oc = o - mu
        var = jnp.mean(oc * oc, axis=-1, keepdims=True)
        gate = g_ref[rows, :].astype(F32)
        y = gate * jax.nn.sigmoid(gate) * (oc * lax.rsqrt(var + EPS))
        o_ref[rows, :] = y.astype(o_ref.dtype)
        return carry

    lax.fori_loop(0, n_chunks, body, 0)


def _retention(proj, batch, seq):
    heads, chunk = RET_HEADS, RET_CHUNK
    dk = proj.shape[1] // (4 * heads)
    dv = dk
    n_chunks = seq // chunk
    log_gamma = jnp.log1p(-jnp.exp2(-5.0 - jnp.arange(heads, dtype=F32)))
    pos = jnp.arange(chunk, dtype=F32)
    rel = pos[:, None] - pos[None, :]
    d_intra = jnp.where(rel >= 0, jnp.exp(log_gamma[:, None, None] * jnp.maximum(rel, 0.0)), 0.0)
    d_query = jnp.broadcast_to(jnp.exp(log_gamma[:, None] * (pos + 1.0))[..., None], (heads, chunk, dv))
    d_key = jnp.broadcast_to(jnp.exp(log_gamma[:, None] * (chunk - 1.0 - pos))[..., None], (heads, chunk, dk))
    d_chunk = jnp.broadcast_to(jnp.exp(log_gamma * chunk)[:, None, None], (heads, 8, dv))

    col = lambda off: pl.BlockSpec((seq, dk), lambda b, h: (b, off + h))
    tab = lambda r, c: pl.BlockSpec((None, r, c), lambda b, h: (h, 0, 0))
    return pl.pallas_call(
        functools.partial(_retention_kernel, n_chunks=n_chunks, chunk=chunk, k_scale=dk ** -0.5),
        grid=(batch, heads),
        in_specs=[col(0), col(heads), col(2 * heads), col(3 * heads),
                  tab(chunk, chunk), tab(chunk, dv), tab(chunk, dk), tab(8, dv)],
        out_specs=pl.BlockSpec((seq, dv), lambda b, h: (b, h)),
        out_shape=jax.ShapeDtypeStruct((batch * seq, heads * dv), BF16),
        scratch_shapes=[pltpu.VMEM((dk, dv), F32)],
        compiler_params=_params("parallel", "parallel"), name="retention",
    )(proj, proj, proj, proj, d_intra, d_query, d_key, d_chunk)


def _compress_kernel(x_ref, pos_ref, w1_ref, w2_ref, o_ref, *, n_cmp):
    dh = NSA_DH
    half = CMP_BLOCK // 2
    width = NSA_KV_HEADS * dh
    n_rows = x_ref.shape[0]
    row = lax.broadcasted_iota(jnp.int32, (n_rows, dh), 0)
    for g in range(NSA_KV_HEADS):
        lo = jnp.zeros((n_rows, w1_ref.shape[1]), F32)
        hi = jnp.zeros((n_rows, w1_ref.shape[1]), F32)
        for l in range(half):
            xl = x_ref[:, l * width + g * dh:l * width + (g + 1) * dh].astype(F32)
            a = (xl + pos_ref[l:l + 1, :]).astype(BF16)
            b = (xl + pos_ref[half + l:half + l + 1, :]).astype(BF16)
            lo = lo + _dot(a, w1_ref[l * dh:(l + 1) * dh, :])
            hi = hi + _dot(b, w1_ref[(half + l) * dh:(half + l + 1) * dh, :])
        pre = lo + pltpu.roll(hi, n_rows - 1, axis=0)
        hid = pre * jax.nn.sigmoid(pre)
        out = _dot(hid.astype(BF16), w2_ref[...])
        o_ref[g] = jnp.where(row < n_cmp, out, 0.0).astype(o_ref.dtype)


def _compress(kv6, pos, w1, w2, batch, seq):
    width = kv6.shape[-1]
    n_chunks = seq // CMP_STRIDE
    n_cmp = (seq - CMP_BLOCK) // CMP_STRIDE + 1
    assert CMP_BLOCK == 2 * CMP_STRIDE and n_cmp == n_chunks - 1
    x = kv6.reshape(6, batch, n_chunks, CMP_STRIDE * width)
    hidden = w1.shape[-1]
    return pl.pallas_call(
        functools.partial(_compress_kernel, n_cmp=n_cmp),
        grid=(2, batch),
        in_specs=[pl.BlockSpec((None, None, n_chunks, CMP_STRIDE * width), lambda c, b: (c, b, 0, 0)),
                  pl.BlockSpec((None, CMP_BLOCK, NSA_DH), lambda c, b: (c, 0, 0)),
                  pl.BlockSpec((None, CMP_BLOCK * NSA_DH, hidden), lambda c, b: (c, 0, 0)),
                  pl.BlockSpec((None, hidden, NSA_DH), lambda c, b: (c, 0, 0))],
        out_specs=pl.BlockSpec((None, None, NSA_KV_HEADS, n_chunks, NSA_DH), lambda c, b: (c, b, 0, 0, 0)),
        out_shape=jax.ShapeDtypeStruct((2, batch, NSA_KV_HEADS, n_chunks, NSA_DH), BF16),
        compiler_params=_params("parallel", "parallel"), name="nsa_compress",
    )(x, pos, w1, w2)


def _masked_softmax_rows(s, valid):
    s = jnp.where(valid, s, NEG_INF)
    m = jnp.max(s, axis=-1, keepdims=True)
    e = jnp.where(valid, jnp.exp(s - m), 0.0)
    l = jnp.sum(e, axis=-1, keepdims=True)
    return e / jnp.where(l > 0.0, l, 1.0)


def _nsa_kernel(q_ref, ks_ref, vs_ref, kw_ref, vw_ref, kc_ref, vc_ref, gl_ref, slope_ref, ovt_ref,
                o_ref, *, tq, tk, n_cmp, n_sel, k_top):
    dh = NSA_DH
    grp = NSA_GROUP
    scale = dh ** -0.5
    span = WINDOW + tq
    t0 = pl.program_id(2) * tq
    q = q_ref[...]
    q_heads = [q[:, r * dh:(r + 1) * dh] for r in range(grp)]
    q4 = jnp.concatenate(q_heads, axis=0)

    kc = kc_ref[...]
    vc = vc_ref[...]
    nc = kc.shape[0]
    row_c = lax.broadcasted_iota(jnp.int32, (tq, nc), 0)
    col_c = lax.broadcasted_iota(jnp.int32, (tq, nc), 1)
    dist_c = t0 + row_c - (CMP_STRIDE * col_c + (CMP_BLOCK - 1))
    valid_c = jnp.where(col_c < n_cmp, dist_c, -1) >= 0
    dist_cf = dist_c.astype(F32)
    p_sum = jnp.zeros((tq, nc), F32)
    o_cmp = []
    for r in range(grp):
        s = _dot_nt(q_heads[r], kc) * scale - slope_ref[r:r + 1, 0:nc] * dist_cf
        p = _masked_softmax_rows(s, valid_c)
        p_sum = p_sum + p
        o_cmp.append(_dot(p.astype(BF16), vc))

    p_hi = p_sum.astype(BF16)
    p_lo = (p_sum - p_hi.astype(F32)).astype(BF16)
    ovt = ovt_ref[...]
    imp = _dot_nt(ovt, p_hi) + _dot_nt(ovt, p_lo)
    blk = lax.broadcasted_iota(jnp.int32, (n_sel, tq), 0)
    t_sel = t0 + lax.broadcasted_iota(jnp.int32, (n_sel, tq), 1)
    back = jnp.right_shift(t_sel, SEL_SHIFT) - blk
    forced = jnp.where(blk == 0, 0, jnp.where(back < 0, SEL_LOCAL, back)) < SEL_LOCAL
    imp = jnp.where(forced, FORCED_SCORE, jnp.where(blk * SEL_BLOCK > t_sel, -1.0, imp))
    rank = jnp.zeros((n_sel, tq), F32)
    for j in range(n_sel):
        cand = imp[j:j + 1, :]
        tie = jnp.where(blk > j, 1.0, 0.0)
        rank = rank + jnp.where(cand > imp, 1.0, jnp.where(cand == imp, tie, 0.0))
    sel_t = jnp.where(rank < k_top, 1.0, 0.0)
    sel_t = jnp.concatenate([sel_t, jnp.zeros((LANES - n_sel, tq), F32)], axis=0)
    sel = sel_t.T.astype(BF16)

    row_s = lax.broadcasted_iota(jnp.int32, (grp * tq, tk), 0)
    col_s = lax.broadcasted_iota(jnp.int32, (grp * tq, tk), 1)
    rel_s = ((row_s & (tq - 1)) - col_s).astype(F32)
    slope_s = jnp.concatenate(
        [jnp.broadcast_to(slope_ref[r:r + 1, 0:tk], (tq, tk)) for r in range(grp)], axis=0)
    blk_e = lax.broadcasted_iota(jnp.int32, (LANES, tk), 0)
    col_e = lax.broadcasted_iota(jnp.int32, (LANES, tk), 1)

    def sel_body(kt, carry):
        m, l, acc = carry
        k0 = pl.multiple_of(kt * tk, tk)
        k = ks_ref[pl.ds(k0, tk), :]
        v = vs_ref[pl.ds(k0, tk), :]
        dist = rel_s + (t0 - k0).astype(F32)
        expand = jnp.where(jnp.right_shift(col_e + k0, SEL_SHIFT) == blk_e, 1.0, 0.0).astype(BF16)
        chosen = _dot(sel, expand)
        chosen = jnp.concatenate([chosen] * grp, axis=0)
        s = _dot_nt(q4, k) * scale - slope_s * dist
        s = jnp.where(chosen > 0.5, jnp.where(dist >= 0.0, s, NEG_INF), NEG_INF)
        m_new = jnp.maximum(m, jnp.max(s, axis=-1, keepdims=True))
        alpha = jnp.exp(m - m_new)
        p = jnp.exp(s - m_new)
        l = alpha * l + jnp.sum(p, axis=-1, keepdims=True)
        acc = alpha * acc + _dot(p.astype(BF16), v)
        return m_new, l, acc

    n_kt = (t0 + tq + tk - 1) // tk
    init = (jnp.full((grp * tq, 1), NEG_INF, F32), jnp.zeros((grp * tq, 1), F32),
            jnp.zeros((grp * tq, dh), F32))
    _, l_sel, acc_sel = lax.fori_loop(0, n_kt, sel_body, init)
    o_sel = acc_sel / l_sel

    w0 = pl.multiple_of(jnp.maximum(t0 - WINDOW, 0), tq)
    kw = kw_ref[pl.ds(w0, span), :]
    vw = vw_ref[pl.ds(w0, span), :]
    row_w = lax.broadcasted_iota(jnp.int32, (grp * tq, span), 0)
    col_w = lax.broadcasted_iota(jnp.int32, (grp * tq, span), 1)
    dist_w = (row_w & (tq - 1)) - col_w + (t0 - w0)
    valid_w = jnp.where(dist_w < WINDOW, dist_w, -1) >= 0
    slope_w = jnp.concatenate(
        [jnp.broadcast_to(slope_ref[r:r + 1, 0:span], (tq, span)) for r in range(grp)], axis=0)
    s = _dot_nt(q4, kw) * scale - slope_w * dist_w.astype(F32)
    s = jnp.where(valid_w, s, NEG_INF)
    p = jnp.exp(s - jnp.max(s, axis=-1, keepdims=True))
    o_win = _dot(p.astype(BF16), vw) / jnp.sum(p, axis=-1, keepdims=True)

    gate = jax.nn.sigmoid(gl_ref[...])
    for r in range(grp):
        rows = slice(r * tq, (r + 1) * tq)
        out = (gate[:, 3 * r:3 * r + 1] * o_cmp[r]
               + gate[:, 3 * r + 1:3 * r + 2] * o_sel[rows]
               + gate[:, 3 * r + 2:3 * r + 3] * o_win[rows])
        o_ref[:, r * dh:(r + 1) * dh] = out.astype(o_ref.dtype)


def _nsa(q_n, kv6, cmp_kv, gate_logits, batch, seq, tq=128, tk=256):
    dh, kvh, grp = NSA_DH, NSA_KV_HEADS, NSA_GROUP
    n_cmp = (seq - CMP_BLOCK) // CMP_STRIDE + 1
    nc = cmp_kv.shape[3]
    n_sel = seq // SEL_BLOCK
    k_top = min(SEL_TOPK, n_sel)
    span = WINDOW + tq
    assert seq % tk == 0 and seq >= span and WINDOW % tq == 0 and n_sel <= LANES and tk % SEL_BLOCK == 0

    slopes = jnp.exp2(-8.0 * (jnp.arange(NSA_HEADS, dtype=F32) + 1.0) / NSA_HEADS).reshape(kvh, grp)
    slope_tab = jnp.zeros((kvh, 8, span), F32).at[:, :grp, :].set(
        jnp.broadcast_to(slopes[:, :, None], (kvh, grp, span)))
    cmp_start = CMP_STRIDE * jnp.arange(n_cmp)
    sel_start = SEL_BLOCK * jnp.arange(n_sel)
    overlap = jnp.clip(jnp.minimum(cmp_start[:, None] + CMP_BLOCK, sel_start[None, :] + SEL_BLOCK)
                       - jnp.maximum(cmp_start[:, None], sel_start[None, :]), 0).astype(F32) / CMP_BLOCK
    ovt = jnp.zeros((n_sel, nc), F32).at[:, :n_cmp].set(overlap.T).astype(BF16)

    nq = seq // tq
    kv_spec = lambda c: pl.BlockSpec((None, seq, dh), lambda b, g, i: (c, b, g))
    cmp_spec = lambda c: pl.BlockSpec((None, None, None, nc, dh), lambda b, g, i: (c, b, g, 0, 0))
    return pl.pallas_call(
        functools.partial(_nsa_kernel, tq=tq, tk=tk, n_cmp=n_cmp, n_sel=n_sel, k_top=k_top),
        grid=(batch, kvh, nq),
        in_specs=[pl.BlockSpec((tq, grp * dh), lambda b, g, i: (b * nq + i, g)),
                  kv_spec(2), kv_spec(3), kv_spec(4), kv_spec(5), cmp_spec(0), cmp_spec(1),
                  pl.BlockSpec((tq, LANES), lambda b, g, i: (b * nq + i, g)),
                  pl.BlockSpec((None, 8, span), lambda b, g, i: (g, 0, 0)),
                  pl.BlockSpec((n_sel, nc), lambda b, g, i: (0, 0))],
        out_specs=pl.BlockSpec((tq, grp * dh), lambda b, g, i: (b * nq + i, g)),
        out_shape=jax.ShapeDtypeStruct((batch * seq, kvh * grp * dh), BF16),
        compiler_params=_params("parallel", "parallel", "arbitrary"), name="nsa_attention",
    )(q_n, kv6, kv6, kv6, kv6, cmp_kv, cmp_kv, gate_logits, slope_tab, ovt)


def _layer_weights(l, w_in, cmp_pos_k, cmp_w1_k, cmp_w2_k, cmp_pos_v, cmp_w1_v, cmp_w2_v,
                   w_ret_up, w_nsa_up, w_out, w_mlp_in, w_mlp_out):
    d = w_in.shape[1]
    ret_w = w_ret_up.shape[1]
    nsa_w = w_nsa_up.shape[1]
    kv_w = NSA_KV_HEADS * NSA_DH
    n_gate = 3 * NSA_HEADS
    cuts = np.cumsum([0, 4 * ret_w, nsa_w, 6 * kv_w, n_gate, d, d])
    wi = w_in[l]
    pieces = [wi[:, cuts[i]:cuts[i + 1]] for i in range(6)]
    per_grp = n_gate // NSA_KV_HEADS
    w_gate = jnp.pad(pieces[3].reshape(d, NSA_KV_HEADS, per_grp),
                     ((0, 0), (0, 0), (0, LANES - per_grp))).reshape(d, NSA_KV_HEADS * LANES)
    bf = lambda a: a.astype(BF16)
    return dict(
        w_ret=bf(pieces[0]), w_qn=bf(pieces[1]), w_kv=bf(pieces[2]), w_gate=bf(w_gate),
        w_gr=bf(pieces[4]), w_gn=bf(pieces[5]),
        pos=jnp.stack([cmp_pos_k[l], cmp_pos_v[l]]),
        w1=bf(jnp.stack([cmp_w1_k[l], cmp_w1_v[l]])), w2=bf(jnp.stack([cmp_w2_k[l], cmp_w2_v[l]])),
        w_ru=bf(w_ret_up[l]), w_nu=bf(w_nsa_up[l]), w_out=bf(w_out[l]),
        w_mi=bf(w_mlp_in[l]), w_mo=bf(w_mlp_out[l]))


def kernel(x, mix_norm_pre, w_in, cmp_pos_k, cmp_w1_k, cmp_w2_k, cmp_pos_v, cmp_w1_v, cmp_w2_v,
           w_ret_up, w_nsa_up, w_out, mix_norm_post, mlp_norm_pre, w_mlp_in, w_mlp_out,
           mlp_norm_post):
    batch, seq, d = x.shape
    depth = w_in.shape[0]
    kv_w = NSA_KV_HEADS * NSA_DH
    xs = x.reshape(batch * seq, d)
    h = _rmsnorm(xs, mix_norm_pre[0])
    for l in range(depth):
        w = _layer_weights(l, w_in, cmp_pos_k, cmp_w1_k, cmp_w2_k, cmp_pos_v, cmp_w1_v, cmp_w2_v,
                           w_ret_up, w_nsa_up, w_out, w_mlp_in, w_mlp_out)
        ret_proj = _matmul(h, w["w_ret"], BF16, name="proj_ret")
        q_n = _matmul(h, w["w_qn"], BF16, name="proj_nsa_q")
        kv6 = _matmul(h, w["w_kv"], BF16, tn=kv_w, split_cols=True, name="proj_nsa_kv")
        gate_logits = _matmul(h, w["w_gate"], F32, tn=NSA_KV_HEADS * LANES, name="proj_nsa_gate")
        y_ret = _retention(ret_proj, batch, seq)
        cmp_kv = _compress(kv6, w["pos"], w["w1"], w["w2"], batch, seq)
        y_nsa = _nsa(q_n, kv6, cmp_kv, gate_logits, batch, seq)
        merged = _merge(y_ret, y_nsa, h, w["w_ru"], w["w_nu"], w["w_gr"], w["w_gn"])
        mix = _matmul(merged, w["w_out"], F32, name="proj_out")
        xs, h = _post(mix, xs, mix_norm_post[l], mlp_norm_pre[l])
        u = _matmul(h, w["w_mi"], BF16, relu2=True, name="mlp_in")
        y = _matmul_ktiled(u, w["w_mo"], name="mlp_out")
        xs, h = _post(y, xs, mlp_norm_post[l], mix_norm_pre[l + 1] if l + 1 < depth else None)
    return xs.reshape(batch, seq, d)
```

```python
import functools
import math

import numpy as np
import jax
import jax.numpy as jnp
from jax import lax
from jax.experimental import pallas as pl
from jax.experimental.pallas import tpu as pltpu

RET_HEADS = 8
RET_CHUNK = 128
NSA_HEADS = 16
NSA_KV_HEADS = 4
NSA_DH = 128
CMP_BLOCK = 32
CMP_STRIDE = 16
SEL_BLOCK = 64
SEL_TOPK = 16
SEL_LOCAL = 2
WINDOW = 512
EPS = 1e-6
NEG_INF = -1e30
FORCED_SCORE = 1e4

NSA_GROUP = NSA_HEADS // NSA_KV_HEADS
SEL_SHIFT = SEL_BLOCK.bit_length() - 1
LANES = 128
VMEM_LIMIT = 56 * 1024 * 1024
LOG2E = math.log2(math.e)
MASK_DIST = 1e32
MASK_POW2 = 2.0 ** 100

F32 = jnp.float32
BF16 = jnp.bfloat16


def _params(*semantics):
    return pltpu.CompilerParams(dimension_semantics=semantics, vmem_limit_bytes=VMEM_LIMIT)


def _dot(a, b):
    return jnp.dot(a, b, preferred_element_type=F32)


def _dot_nt(a, b):
    return lax.dot_general(a, b, (((1,), (1,)), ((), ())), preferred_element_type=F32)


def _dot_tn(a, b):
    return lax.dot_general(a, b, (((0,), (0,)), ((), ())), preferred_element_type=F32)


def _rmsnorm_kernel(x_ref, g_ref, o_ref):
    x = x_ref[...]
    ms = jnp.mean(x * x, axis=-1, keepdims=True)
    o_ref[...] = (x * lax.rsqrt(ms + EPS) * g_ref[...]).astype(o_ref.dtype)


def _rmsnorm(x, gain, tm=256):
    m, d = x.shape
    return pl.pallas_call(
        _rmsnorm_kernel,
        grid=(m // tm,),
        in_specs=[pl.BlockSpec((tm, d), lambda i: (i, 0)),
                  pl.BlockSpec((1, d), lambda i: (0, 0))],
        out_specs=pl.BlockSpec((tm, d), lambda i: (i, 0)),
        out_shape=jax.ShapeDtypeStruct((m, d), BF16),
        compiler_params=_params("parallel"),
        name="rmsnorm",
    )(x, gain.reshape(1, d))


def _post_kernel(y_ref, x_ref, gp_ref, gn_ref, xo_ref, ho_ref):
    y = y_ref[...].astype(F32)
    yn = y * lax.rsqrt(jnp.mean(y * y, axis=-1, keepdims=True) + EPS) * gp_ref[...]
    xn = x_ref[...] + yn
    xo_ref[...] = xn
    ho_ref[...] = (xn * lax.rsqrt(jnp.mean(xn * xn, axis=-1, keepdims=True) + EPS)
                   * gn_ref[...]).astype(ho_ref.dtype)


def _post_last_kernel(y_ref, x_ref, gp_ref, xo_ref):
    y = y_ref[...].astype(F32)
    yn = y * lax.rsqrt(jnp.mean(y * y, axis=-1, keepdims=True) + EPS) * gp_ref[...]
    xo_ref[...] = x_ref[...] + yn


def _post(y, x, g_post, g_next, tm=256):
    m, d = x.shape
    row = pl.BlockSpec((tm, d), lambda i: (i, 0))
    vec = pl.BlockSpec((1, d), lambda i: (0, 0))
    if g_next is None:
        return pl.pallas_call(
            _post_last_kernel, grid=(m // tm,),
            in_specs=[row, row, vec], out_specs=row,
            out_shape=jax.ShapeDtypeStruct((m, d), F32),
            compiler_params=_params("parallel"), name="post_last",
        )(y, x, g_post.reshape(1, d)), None
    return pl.pallas_call(
        _post_kernel, grid=(m // tm,),
        in_specs=[row, row, vec, vec], out_specs=[row, row],
        out_shape=[jax.ShapeDtypeStruct((m, d), F32), jax.ShapeDtypeStruct((m, d), BF16)],
        compiler_params=_params("parallel"), name="post",
    )(y, x, g_post.reshape(1, d), g_next.reshape(1, d))


def _matmul_kernel(a_ref, b_ref, o_ref):
    o_ref[...] = _dot(a_ref[...], b_ref[...]).astype(o_ref.dtype)


def _matmul(a, b, out_dtype, tm=1024, tn=1024, name="matmul"):
    m, k = a.shape
    n = b.shape[1]
    tm, tn = min(tm, m), min(tn, n)
    return pl.pallas_call(
        _matmul_kernel,
        grid=(m // tm, n // tn),
        in_specs=[pl.BlockSpec((tm, k), lambda i, j: (i, 0)),
                  pl.BlockSpec((k, tn), lambda i, j: (0, j))],
        out_specs=pl.BlockSpec((tm, tn), lambda i, j: (i, j)),
        out_shape=jax.ShapeDtypeStruct((m, n), out_dtype),
        compiler_params=_params("parallel", "arbitrary"), name=name,
    )(a, b)


def _matmul_ws_kernel(a_ref, w_ref, o_ref, wq_ref, *, relu2):
    @pl.when(pl.program_id(1) == 0)
    def _():
        wq_ref[...] = w_ref[...].astype(BF16)

    acc = _dot(a_ref[...], wq_ref[...])
    if relu2:
        acc = jnp.square(jnp.maximum(acc, 0.0))
    o_ref[...] = acc.astype(o_ref.dtype)


def _matmul_ws(a, w, layer, col0, n, out_dtype, tm=1024, tn=512, relu2=False, name="matmul_ws"):
    m, k = a.shape
    tm, tn = min(tm, m), min(tn, n)
    assert col0 % tn == 0 and n % tn == 0 and m % tm == 0
    j0 = col0 // tn
    return pl.pallas_call(
        functools.partial(_matmul_ws_kernel, relu2=relu2),
        grid=(n // tn, m // tm),
        in_specs=[pl.BlockSpec((tm, k), lambda j, i: (i, 0)),
                  pl.BlockSpec((None, k, tn), lambda j, i: (layer, 0, j0 + j))],
        out_specs=pl.BlockSpec((tm, tn), lambda j, i: (i, j)),
        out_shape=jax.ShapeDtypeStruct((m, n), out_dtype),
        scratch_shapes=[pltpu.VMEM((k, tn), BF16)],
        compiler_params=_params("parallel", "arbitrary"), name=name,
    )(a, w)


def _matmul_ktiled_kernel(a_ref, w_ref, o_ref):
    @pl.when(pl.program_id(2) == 0)
    def _():
        o_ref[...] = jnp.zeros_like(o_ref)

    o_ref[...] += _dot(a_ref[...], w_ref[...].astype(BF16))


def _matmul_ktiled(a, w, layer, tm=2048, tn=1024, tk=1024, name="matmul_k"):
    m, k = a.shape
    n = w.shape[2]
    tm, tn, tk = min(tm, m), min(tn, n), min(tk, k)
    return pl.pallas_call(
        _matmul_ktiled_kernel,
        grid=(m // tm, n // tn, k // tk),
        in_specs=[pl.BlockSpec((tm, tk), lambda i, j, kk: (i, kk)),
                  pl.BlockSpec((None, tk, tn), lambda i, j, kk: (layer, kk, j))],
        out_specs=pl.BlockSpec((tm, tn), lambda i, j, kk: (i, j)),
        out_shape=jax.ShapeDtypeStruct((m, n), F32),
        compiler_params=_params("parallel", "parallel", "arbitrary"), name=name,
    )(a, w)


def _merge_kernel(yr_ref, yn_ref, h_ref, wru_ref, wnu_ref, wgr_ref, wgn_ref, o_ref):
    h = h_ref[...]
    a = _dot(yr_ref[...], wru_ref[...])
    b = _dot(yn_ref[...], wnu_ref[...])
    gr = jax.nn.sigmoid(_dot(h, wgr_ref[...]))
    gn = jax.nn.sigmoid(_dot(h, wgn_ref[...]))
    o_ref[...] = (gr * a + gn * b).astype(o_ref.dtype)


def _merge(y_ret, y_nsa, h, w_ru, w_nu, w_gr, w_gn, tm=512, tn=512):
    m, d = h.shape
    kr, kn = y_ret.shape[1], y_nsa.shape[1]
    n = w_ru.shape[1]
    tm, tn = min(tm, m), min(tn, n)
    lhs = lambda kk: pl.BlockSpec((tm, kk), lambda i, j: (i, 0))
    rhs = lambda kk: pl.BlockSpec((kk, tn), lambda i, j: (0, j))
    return pl.pallas_call(
        _merge_kernel, grid=(m // tm, n // tn),
        in_specs=[lhs(kr), lhs(kn), lhs(d), rhs(kr), rhs(kn), rhs(d), rhs(d)],
        out_specs=pl.BlockSpec((tm, tn), lambda i, j: (i, j)),
        out_shape=jax.ShapeDtypeStruct((m, n), BF16),
        compiler_params=_params("parallel", "arbitrary"), name="merge",
    )(y_ret, y_nsa, h, w_ru, w_nu, w_gr, w_gn)


def _retention_kernel(q_ref, k_ref, v_ref, g_ref, di_ref, dq_ref, dk_ref, dc_ref, o_ref,
                      state_ref, *, n_chunks, chunk, k_scale):
    state_ref[...] = jnp.zeros_like(state_ref)
    d_intra = di_ref[...]
    d_query = dq_ref[...]
    d_key = dk_ref[...]
    d_chunk = dc_ref[0:1, :]

    def body(n, carry):
        rows = pl.ds(pl.multiple_of(n * chunk, chunk), chunk)
        q = q_ref[rows, :]
        kf = k_ref[rows, :].astype(F32) * k_scale
        v = v_ref[rows, :]
        s = _dot_nt(q, kf.astype(BF16)) * d_intra
        state = state_ref[...]
        o = _dot(s.astype(BF16), v) + _dot(q, state.astype(BF16)) * d_query
        state_ref[...] = state * d_chunk + _dot_tn((kf * d_key).astype(BF16), v)
        mu = jnp.mean(o, axis=-1, keepdims=True)
        oc = o - mu
        var = jnp.mean(oc * oc, axis=-1, keepdims=True)
        gate = g_ref[rows, :].astype(F32)
        y = gate * jax.nn.sigmoid(gate) * (oc * lax.rsqrt(var + EPS))
        o_ref[rows, :] = y.astype(o_ref.dtype)
        return carry

    lax.fori_loop(0, n_chunks, body, 0)


def _retention(proj, batch, seq, dk):
    heads, chunk = RET_HEADS, RET_CHUNK
    dv = dk
    n_chunks = seq // chunk
    log_gamma = jnp.log1p(-jnp.exp2(-5.0 - jnp.arange(heads, dtype=F32)))
    pos = jnp.arange(chunk, dtype=F32)
    rel = pos[:, None] - pos[None, :]
    d_intra = jnp.where(rel >= 0, jnp.exp(log_gamma[:, None, None] * jnp.maximum(rel, 0.0)), 0.0)
    d_query = jnp.broadcast_to(jnp.exp(log_gamma[:, None] * (pos + 1.0))[..., None], (heads, chunk, dv))
    d_key = jnp.broadcast_to(jnp.exp(log_gamma[:, None] * (chunk - 1.0 - pos))[..., None], (heads, chunk, dk))
    d_chunk = jnp.broadcast_to(jnp.exp(log_gamma * chunk)[:, None, None], (heads, 8, dv))

    col = lambda off: pl.BlockSpec((seq, dk), lambda b, h: (b, off + h))
    tab = lambda r, c: pl.BlockSpec((None, r, c), lambda b, h: (h, 0, 0))
    return pl.pallas_call(
        functools.partial(_retention_kernel, n_chunks=n_chunks, chunk=chunk, k_scale=dk ** -0.5),
        grid=(batch, heads),
        in_specs=[col(0), col(heads), col(2 * heads), col(3 * heads),
                  tab(chunk, chunk), tab(chunk, dv), tab(chunk, dk), tab(8, dv)],
        out_specs=pl.BlockSpec((seq, dv), lambda b, h: (b, h)),
        out_shape=jax.ShapeDtypeStruct((batch * seq, heads * dv), BF16),
        scratch_shapes=[pltpu.VMEM((dk, dv), F32)],
        compiler_params=_params("parallel", "parallel"), name="retention",
    )(proj, proj, proj, proj, d_intra, d_query, d_key, d_chunk)


def _compress_kernel(x_ref, pos_ref, w1_ref, w2_ref, o_ref, x32_ref, *, n_cmp, n_chunks):
    dh = NSA_DH
    half = CMP_BLOCK // 2
    x32_ref[...] = x_ref[...].astype(F32)
    lo = jnp.zeros((n_chunks, w1_ref.shape[1]), F32)
    hi = jnp.zeros((n_chunks, w1_ref.shape[1]), F32)
    for l in range(half):
        xl = x32_ref[pl.ds(l, n_chunks, stride=CMP_STRIDE), :]
        a = (xl + pos_ref[l:l + 1, :]).astype(BF16)
        b = (xl + pos_ref[half + l:half + l + 1, :]).astype(BF16)
        lo = lo + _dot(a, w1_ref[l * dh:(l + 1) * dh, :])
        hi = hi + _dot(b, w1_ref[(half + l) * dh:(half + l + 1) * dh, :])
    pre = lo + pltpu.roll(hi, n_chunks - 1, axis=0)
    hid = pre * jax.nn.sigmoid(pre)
    out = _dot(hid.astype(BF16), w2_ref[...])
    row = lax.broadcasted_iota(jnp.int32, out.shape, 0)
    o_ref[...] = jnp.where(row < n_cmp, out, 0.0).astype(o_ref.dtype)


def _compress(proj, col_blk, pos, w1, w2, batch, seq):
    kvh, dh = NSA_KV_HEADS, NSA_DH
    n_chunks = seq // CMP_STRIDE
    n_cmp = (seq - CMP_BLOCK) // CMP_STRIDE + 1
    assert CMP_BLOCK == 2 * CMP_STRIDE and n_cmp == n_chunks - 1
    hidden = w1.shape[-1]
    return pl.pallas_call(
        functools.partial(_compress_kernel, n_cmp=n_cmp, n_chunks=n_chunks),
        grid=(2, batch, kvh),
        in_specs=[pl.BlockSpec((seq, dh), lambda c, b, g: (b, col_blk + kvh * c + g)),
                  pl.BlockSpec((None, CMP_BLOCK, dh), lambda c, b, g: (c, 0, 0)),
                  pl.BlockSpec((None, CMP_BLOCK * dh, hidden), lambda c, b, g: (c, 0, 0)),
                  pl.BlockSpec((None, hidden, dh), lambda c, b, g: (c, 0, 0))],
        out_specs=pl.BlockSpec((None, None, None, n_chunks, dh), lambda c, b, g: (c, b, g, 0, 0)),
        out_shape=jax.ShapeDtypeStruct((2, batch, kvh, n_chunks, dh), BF16),
        scratch_shapes=[pltpu.VMEM((seq, dh), F32)],
        compiler_params=_params("parallel", "parallel", "parallel"), name="nsa_compress",
    )(proj, pos, w1, w2)


def _nsa_kernel(q_ref, ks_ref, vs_ref, kw_ref, vw_ref, kc_ref, vc_ref, gl_ref, slope_ref, ovt_ref,
                et_ref, dw_ref, o_ref, *, tq, tk, n_cmp, n_sel, k_top):
    dh = NSA_DH
    grp = NSA_GROUP
    sc = (dh ** -0.5) * LOG2E
    span = WINDOW + tq
    t0 = pl.program_id(2) * tq
    q = q_ref[...]
    q4 = jnp.concatenate([q[:, r * dh:(r + 1) * dh] for r in range(grp)], axis=0)
    slope_row = slope_ref[0:1, :]
    tile4 = lambda a: jnp.concatenate([a] * grp, axis=1)

    kc = kc_ref[...]
    vc = vc_ref[...]
    nc = kc.shape[0]
    n_idx = lax.broadcasted_iota(jnp.int32, (nc, tq), 0)
    t_idx = t0 + lax.broadcasted_iota(jnp.int32, (nc, tq), 1)
    dist_c = t_idx - (CMP_STRIDE * n_idx + (CMP_BLOCK - 1))
    dist_c = jnp.where(n_idx < n_cmp, dist_c, -1)
    dist_c = tile4(jnp.where(dist_c >= 0, dist_c.astype(F32), MASK_DIST))
    s = _dot_nt(kc, q4) * sc - slope_row * dist_c
    m = jnp.max(s, axis=0, keepdims=True)
    e = jnp.where(dist_c < MASK_DIST, jnp.exp2(s - m), 0.0)
    l = jnp.sum(e, axis=0, keepdims=True)
    p = e / jnp.where(l > 0.0, l, 1.0)
    o_cmp = _dot_tn(vc, p.astype(BF16))
    p_sum = p[:, 0:tq]
    for r in range(1, grp):
        p_sum = p_sum + p[:, r * tq:(r + 1) * tq]

    p_hi = p_sum.astype(BF16)
    p_lo = (p_sum - p_hi.astype(F32)).astype(BF16)
    ovt = ovt_ref[...]
    imp = _dot(ovt, p_hi) + _dot(ovt, p_lo)
    blk = lax.broadcasted_iota(jnp.int32, (n_sel, tq), 0)
    t_sel = t0 + lax.broadcasted_iota(jnp.int32, (n_sel, tq), 1)
    back = jnp.right_shift(t_sel, SEL_SHIFT) - blk
    forced = jnp.where(blk == 0, 0, jnp.where(back < 0, SEL_LOCAL, back)) < SEL_LOCAL
    imp = jnp.where(forced, FORCED_SCORE, jnp.where(blk * SEL_BLOCK > t_sel, -1.0, imp))
    rank = jnp.zeros((n_sel, tq), F32)
    for j in range(n_sel):
        cand = imp[j:j + 1, :]
        tie = jnp.where(blk > j, 1.0, 0.0)
        rank = rank + jnp.where(cand > imp, 1.0, jnp.where(cand == imp, tie, 0.0))
    unsel = jnp.where(rank < k_top, 0.0, 1.0)
    unsel = jnp.concatenate([unsel, jnp.zeros((LANES - n_sel, tq), F32)], axis=0).astype(BF16)

    case = jnp.minimum(pl.program_id(2), WINDOW // tq)
    w0 = pl.multiple_of(jnp.maximum(t0 - WINDOW, 0), tq)
    s = _dot_nt(kw_ref[pl.ds(w0, span), :], q4) * sc - slope_row * tile4(dw_ref[case])
    p = jnp.exp2(s - jnp.max(s, axis=0, keepdims=True))
    o_win = _dot_tn(vw_ref[pl.ds(w0, span), :], p.astype(BF16)) / jnp.sum(p, axis=0, keepdims=True)

    rel = (lax.broadcasted_iota(jnp.int32, (tk, tq), 1)
           - lax.broadcasted_iota(jnp.int32, (tk, tq), 0)).astype(F32)

    def sel_body(kt, carry):
        m, l, acc = carry
        k0 = pl.multiple_of(kt * tk, tk)
        k = ks_ref[pl.ds(k0, tk), :]
        v = vs_ref[pl.ds(k0, tk), :]
        dist = rel + (t0 - k0).astype(F32)
        dist = jnp.where(dist >= 0.0, dist, MASK_DIST) + _dot(et_ref[pl.ds(k0, tk), :], unsel)
        s = _dot_nt(k, q4) * sc - slope_row * tile4(dist)
        m_new = jnp.maximum(m, jnp.max(s, axis=0, keepdims=True))
        alpha = jnp.exp2(m - m_new)
        p = jnp.exp2(s - m_new)
        l = alpha * l + jnp.sum(p, axis=0, keepdims=True)
        acc = alpha * acc + _dot_tn(v, p.astype(BF16))
        return m_new, l, acc

    n_kt = (t0 + tq + tk - 1) // tk
    init = (jnp.full((1, grp * tq), NEG_INF, F32), jnp.zeros((1, grp * tq), F32),
            jnp.zeros((dh, grp * tq), F32))
    _, l_sel, acc_sel = lax.fori_loop(0, n_kt, sel_body, init)
    o_sel = acc_sel / l_sel

    gate = jax.nn.sigmoid(gl_ref[...].T)
    for r in range(grp):
        cols = slice(r * tq, (r + 1) * tq)
        out = (gate[3 * r:3 * r + 1, :] * o_cmp[:, cols]
               + gate[3 * r + 1:3 * r + 2, :] * o_sel[:, cols]
               + gate[3 * r + 2:3 * r + 3, :] * o_win[:, cols])
        o_ref[:, r * dh:(r + 1) * dh] = out.T.astype(o_ref.dtype)


def _nsa(proj, q_blk, kv_blk, cmp_kv, gate_logits, batch, seq, tq=256, tk=512):
    dh, kvh, grp = NSA_DH, NSA_KV_HEADS, NSA_GROUP
    n_cmp = (seq - CMP_BLOCK) // CMP_STRIDE + 1
    nc = cmp_kv.shape[3]
    n_sel = seq // SEL_BLOCK
    k_top = min(SEL_TOPK, n_sel)
    span = WINDOW + tq
    n_case = WINDOW // tq + 1
    assert seq % tk == 0 and seq >= span and WINDOW % tq == 0 and n_sel <= LANES and tk % SEL_BLOCK == 0

    slopes = jnp.exp2(-8.0 * (jnp.arange(NSA_HEADS, dtype=F32) + 1.0) / NSA_HEADS).reshape(kvh, grp)
    slope_tab = jnp.broadcast_to(jnp.repeat(slopes * LOG2E, tq, axis=1)[:, None, :], (kvh, 8, grp * tq))
    cmp_start = CMP_STRIDE * jnp.arange(n_cmp)
    sel_start = SEL_BLOCK * jnp.arange(n_sel)
    overlap = jnp.clip(jnp.minimum(cmp_start[:, None] + CMP_BLOCK, sel_start[None, :] + SEL_BLOCK)
                       - jnp.maximum(cmp_start[:, None], sel_start[None, :]), 0).astype(F32) / CMP_BLOCK
    ovt = jnp.zeros((n_sel, nc), F32).at[:, :n_cmp].set(overlap.T).astype(BF16)
    e_t = jnp.where(jnp.arange(seq)[:, None] // SEL_BLOCK == jnp.arange(LANES)[None, :],
                    MASK_POW2, 0.0).astype(BF16)
    dist_w = (jnp.arange(n_case)[:, None, None] * tq + jnp.arange(tq)[None, None, :]
              - jnp.arange(span)[None, :, None])
    dist_w = jnp.where((dist_w >= 0) & (dist_w < WINDOW), dist_w.astype(F32), MASK_DIST)

    nq = seq // tq
    kv_spec = lambda c: pl.BlockSpec((seq, dh), lambda b, g, i: (b, kv_blk + kvh * c + g))
    cmp_spec = lambda c: pl.BlockSpec((None, None, None, nc, dh), lambda b, g, i: (c, b, g, 0, 0))
    whole = lambda a: pl.BlockSpec(a.shape, lambda b, g, i: (0,) * a.ndim)
    return pl.pallas_call(
        functools.partial(_nsa_kernel, tq=tq, tk=tk, n_cmp=n_cmp, n_sel=n_sel, k_top=k_top),
        grid=(batch, kvh, nq),
        in_specs=[pl.BlockSpec((tq, grp * dh), lambda b, g, i: (b * nq + i, q_blk + g)),
                  kv_spec(0), kv_spec(1), kv_spec(2), kv_spec(3), cmp_spec(0), cmp_spec(1),
                  pl.BlockSpec((tq, LANES), lambda b, g, i: (b * nq + i, g)),
                  pl.BlockSpec((None, 8, grp * tq), lambda b, g, i: (g, 0, 0)),
                  whole(ovt), whole(e_t), whole(dist_w)],
        out_specs=pl.BlockSpec((tq, grp * dh), lambda b, g, i: (b * nq + i, g)),
        out_shape=jax.ShapeDtypeStruct((batch * seq, kvh * grp * dh), BF16),
        compiler_params=_params("parallel", "parallel", "arbitrary"), name="nsa_attention",
    )(proj, proj, proj, proj, proj, cmp_kv, cmp_kv, gate_logits, slope_tab, ovt, e_t, dist_w)


def kernel(x, mix_norm_pre, w_in, cmp_pos_k, cmp_w1_k, cmp_w2_k, cmp_pos_v, cmp_w1_v, cmp_w2_v,
           w_ret_up, w_nsa_up, w_out, mix_norm_post, mlp_norm_pre, w_mlp_in, w_mlp_out,
           mlp_norm_post):
    batch, seq, d = x.shape
    depth = w_in.shape[0]
    ret_w = w_ret_up.shape[1]
    nsa_w = w_nsa_up.shape[1]
    kvh, dh = NSA_KV_HEADS, NSA_DH
    kv_w = kvh * dh
    n_gate = 3 * NSA_HEADS
    c_qn = 4 * ret_w
    c_kv = c_qn + nsa_w
    c_gate = c_kv + 6 * kv_w
    c_gr = c_gate + n_gate
    c_gn = c_gr + d

    per_grp = n_gate // kvh
    w_gate = jnp.pad(w_in[:, :, c_gate:c_gr].reshape(depth, d, kvh, per_grp),
                     ((0, 0), (0, 0), (0, 0), (0, LANES - per_grp))).reshape(depth, d, kvh * LANES).astype(BF16)
    w_gr = w_in[:, :, c_gr:c_gn].astype(BF16)
    w_gn = w_in[:, :, c_gn:c_gn + d].astype(BF16)
    w_ru = w_ret_up.astype(BF16)
    w_nu = w_nsa_up.astype(BF16)
    cmp_pos = jnp.stack([cmp_pos_k, cmp_pos_v], axis=1)
    cmp_w1 = jnp.stack([cmp_w1_k, cmp_w1_v], axis=1).astype(BF16)
    cmp_w2 = jnp.stack([cmp_w2_k, cmp_w2_v], axis=1).astype(BF16)

    xs = x.reshape(batch * seq, d)
    h = _rmsnorm(xs, mix_norm_pre[0])
    for l in range(depth):
        proj = _matmul_ws(h, w_in, l, 0, c_gate, BF16, name="proj_in")
        gate_logits = _matmul(h, w_gate[l], F32, name="proj_nsa_gate")
        y_ret = _retention(proj, batch, seq, ret_w // RET_HEADS)
        cmp_kv = _compress(proj, c_kv // dh, cmp_pos[l], cmp_w1[l], cmp_w2[l], batch, seq)
        y_nsa = _nsa(proj, c_qn // (NSA_GROUP * dh), (c_kv + 2 * kv_w) // dh, cmp_kv, gate_logits,
                     batch, seq)
        merged = _merge(y_ret, y_nsa, h, w_ru[l], w_nu[l], w_gr[l], w_gn[l])
        mix = _matmul_ws(merged, w_out, l, 0, d, F32, name="proj_out")
        xs, h = _post(mix, xs, mix_norm_post[l], mlp_norm_pre[l])
        u = _matmul_ws(h, w_mlp_in, l, 0, w_mlp_in.shape[2], BF16, relu2=True, name="mlp_in")
        y = _matmul_ktiled(u, w_mlp_out, l, name="mlp_out")
        xs, h = _post(y, xs, mlp_norm_post[l], mix_norm_pre[l + 1] if l + 1 < depth else None)
    return xs.reshape(batch, seq, d)
```

```python
import functools
import math

import numpy as np
import jax
import jax.numpy as jnp
from jax import lax
from jax.experimental import pallas as pl
from jax.experimental.pallas import tpu as pltpu

RET_HEADS = 8
RET_CHUNK = 128
NSA_HEADS = 16
NSA_KV_HEADS = 4
NSA_DH = 128
CMP_BLOCK = 32
CMP_STRIDE = 16
SEL_BLOCK = 64
SEL_TOPK = 16
SEL_LOCAL = 2
WINDOW = 512
EPS = 1e-6
NEG_INF = -1e30
FORCED_SCORE = 1e4

NSA_GROUP = NSA_HEADS // NSA_KV_HEADS
SEL_SHIFT = SEL_BLOCK.bit_length() - 1
LANES = 128
VMEM_LIMIT = 56 * 1024 * 1024
LOG2E = math.log2(math.e)
MASK_DIST = 1e32
MASK_POW2 = 2.0 ** 100

F32 = jnp.float32
BF16 = jnp.bfloat16


def _params(*semantics):
    return pltpu.CompilerParams(dimension_semantics=semantics, vmem_limit_bytes=VMEM_LIMIT)


def _dot(a, b):
    return jnp.dot(a, b, preferred_element_type=F32)


def _dot_nt(a, b):
    return lax.dot_general(a, b, (((1,), (1,)), ((), ())), preferred_element_type=F32)


def _dot_tn(a, b):
    return lax.dot_general(a, b, (((0,), (0,)), ((), ())), preferred_element_type=F32)


def _rmsnorm_kernel(x_ref, g_ref, o_ref):
    x = x_ref[...]
    ms = jnp.mean(x * x, axis=-1, keepdims=True)
    o_ref[...] = (x * lax.rsqrt(ms + EPS) * g_ref[...]).astype(o_ref.dtype)


def _rmsnorm(x, gain, tm=256):
    m, d = x.shape
    return pl.pallas_call(
        _rmsnorm_kernel,
        grid=(m // tm,),
        in_specs=[pl.BlockSpec((tm, d), lambda i: (i, 0)),
                  pl.BlockSpec((1, d), lambda i: (0, 0))],
        out_specs=pl.BlockSpec((tm, d), lambda i: (i, 0)),
        out_shape=jax.ShapeDtypeStruct((m, d), BF16),
        compiler_params=_params("parallel"),
        name="rmsnorm",
    )(x, gain.reshape(1, d))


def _post_kernel(y_ref, x_ref, gp_ref, gn_ref, xo_ref, ho_ref):
    y = y_ref[...].astype(F32)
    yn = y * lax.rsqrt(jnp.mean(y * y, axis=-1, keepdims=True) + EPS) * gp_ref[...]
    xn = x_ref[...] + yn
    xo_ref[...] = xn
    ho_ref[...] = (xn * lax.rsqrt(jnp.mean(xn * xn, axis=-1, keepdims=True) + EPS)
                   * gn_ref[...]).astype(ho_ref.dtype)


def _post_last_kernel(y_ref, x_ref, gp_ref, xo_ref):
    y = y_ref[...].astype(F32)
    yn = y * lax.rsqrt(jnp.mean(y * y, axis=-1, keepdims=True) + EPS) * gp_ref[...]
    xo_ref[...] = x_ref[...] + yn


def _post(y, x, g_post, g_next, tm=256):
    m, d = x.shape
    row = pl.BlockSpec((tm, d), lambda i: (i, 0))
    vec = pl.BlockSpec((1, d), lambda i: (0, 0))
    if g_next is None:
        return pl.pallas_call(
            _post_last_kernel, grid=(m // tm,),
            in_specs=[row, row, vec], out_specs=row,
            out_shape=jax.ShapeDtypeStruct((m, d), F32),
            compiler_params=_params("parallel"), name="post_last",
        )(y, x, g_post.reshape(1, d)), None
    return pl.pallas_call(
        _post_kernel, grid=(m // tm,),
        in_specs=[row, row, vec, vec], out_specs=[row, row],
        out_shape=[jax.ShapeDtypeStruct((m, d), F32), jax.ShapeDtypeStruct((m, d), BF16)],
        compiler_params=_params("parallel"), name="post",
    )(y, x, g_post.reshape(1, d), g_next.reshape(1, d))


def _matmul_kernel(a_ref, b_ref, o_ref):
    o_ref[...] = _dot(a_ref[...], b_ref[...]).astype(o_ref.dtype)


def _matmul(a, b, out_dtype, tm=1024, tn=1024, name="matmul"):
    m, k = a.shape
    n = b.shape[1]
    tm, tn = min(tm, m), min(tn, n)
    return pl.pallas_call(
        _matmul_kernel,
        grid=(m // tm, n // tn),
        in_specs=[pl.BlockSpec((tm, k), lambda i, j: (i, 0)),
                  pl.BlockSpec((k, tn), lambda i, j: (0, j))],
        out_specs=pl.BlockSpec((tm, tn), lambda i, j: (i, j)),
        out_shape=jax.ShapeDtypeStruct((m, n), out_dtype),
        compiler_params=_params("parallel", "arbitrary"), name=name,
    )(a, b)


def _matmul_ws_kernel(a_ref, w_ref, o_ref, wq_ref, *, relu2):
    @pl.when(pl.program_id(1) == 0)
    def _():
        wq_ref[...] = w_ref[...].astype(BF16)

    acc = _dot(a_ref[...], wq_ref[...])
    if relu2:
        acc = jnp.square(jnp.maximum(acc, 0.0))
    o_ref[...] = acc.astype(o_ref.dtype)


def _matmul_ws(a, w, layer, col0, n, out_dtype, tm=1024, tn=512, relu2=False, name="matmul_ws"):
    m, k = a.shape
    tm, tn = min(tm, m), min(tn, n)
    assert col0 % tn == 0 and n % tn == 0 and m % tm == 0
    j0 = col0 // tn
    return pl.pallas_call(
        functools.partial(_matmul_ws_kernel, relu2=relu2),
        grid=(n // tn, m // tm),
        in_specs=[pl.BlockSpec((tm, k), lambda j, i: (i, 0)),
                  pl.BlockSpec((None, k, tn), lambda j, i: (layer, 0, j0 + j))],
        out_specs=pl.BlockSpec((tm, tn), lambda j, i: (i, j)),
        out_shape=jax.ShapeDtypeStruct((m, n), out_dtype),
        scratch_shapes=[pltpu.VMEM((k, tn), BF16)],
        compiler_params=_params("parallel", "arbitrary"), name=name,
    )(a, w)


def _matmul_ktiled_kernel(a_ref, w_ref, o_ref, acc_ref):
    kk = pl.program_id(2)

    @pl.when(kk == 0)
    def _():
        acc_ref[...] = jnp.zeros_like(acc_ref)

    acc_ref[...] += _dot(a_ref[...], w_ref[...].astype(BF16))

    @pl.when(kk == pl.num_programs(2) - 1)
    def _():
        o_ref[...] = acc_ref[...].astype(o_ref.dtype)


def _matmul_ktiled(a, w, layer, out_dtype, tm=2048, tn=1024, tk=1024, name="matmul_k"):
    m, k = a.shape
    n = w.shape[2]
    tm, tn, tk = min(tm, m), min(tn, n), min(tk, k)
    return pl.pallas_call(
        _matmul_ktiled_kernel,
        grid=(m // tm, n // tn, k // tk),
        in_specs=[pl.BlockSpec((tm, tk), lambda i, j, kk: (i, kk)),
                  pl.BlockSpec((None, tk, tn), lambda i, j, kk: (layer, kk, j))],
        out_specs=pl.BlockSpec((tm, tn), lambda i, j, kk: (i, j)),
        out_shape=jax.ShapeDtypeStruct((m, n), out_dtype),
        scratch_shapes=[pltpu.VMEM((tm, tn), F32)],
        compiler_params=_params("parallel", "parallel", "arbitrary"), name=name,
    )(a, w)


def _realign_kernel(a_ref, b_ref, o_ref, *, shift):
    x = jnp.concatenate([a_ref[...], b_ref[...]], axis=1)
    o_ref[...] = x[:, shift:shift + o_ref.shape[1]].astype(o_ref.dtype)


def _realign_cast(w, col0, n, tr=512, tn=512):
    depth, k, n_all = w.shape
    tr = min(tr, k)
    shift = col0 % tn
    assert 0 < shift <= LANES and n % tn == 0 and k % tr == 0
    j0 = col0 // tn
    per = tn // LANES
    return pl.pallas_call(
        functools.partial(_realign_kernel, shift=shift),
        grid=(depth, k // tr, n // tn),
        in_specs=[pl.BlockSpec((None, tr, tn), lambda l, r, j: (l, r, j0 + j)),
                  pl.BlockSpec((None, tr, LANES), lambda l, r, j: (l, r, (j0 + j + 1) * per))],
        out_specs=pl.BlockSpec((None, tr, tn), lambda l, r, j: (l, r, j)),
        out_shape=jax.ShapeDtypeStruct((depth, k, n), BF16),
        compiler_params=_params("parallel", "parallel", "parallel"), name="realign_cast",
    )(w, w)


def _merge_kernel(yr_ref, yn_ref, h_ref, wru_ref, wnu_ref, wgr_ref, wgn_ref, o_ref):
    h = h_ref[...]
    a = _dot(yr_ref[...], wru_ref[...])
    b = _dot(yn_ref[...], wnu_ref[...])
    gr = jax.nn.sigmoid(_dot(h, wgr_ref[...]))
    gn = jax.nn.sigmoid(_dot(h, wgn_ref[...]))
    o_ref[...] = (gr * a + gn * b).astype(o_ref.dtype)


def _merge(y_ret, y_nsa, h, w_ru, w_nu, w_g, layer, tm=512, tn=512):
    m, d = h.shape
    kr, kn = y_ret.shape[1], y_nsa.shape[1]
    n = w_ru.shape[2]
    tm, tn = min(tm, m), min(tn, n)
    lhs = lambda kk: pl.BlockSpec((tm, kk), lambda i, j: (i, 0))
    rhs = lambda kk, j0: pl.BlockSpec((None, kk, tn), lambda i, j: (layer, 0, j0 + j))
    return pl.pallas_call(
        _merge_kernel, grid=(m // tm, n // tn),
        in_specs=[lhs(kr), lhs(kn), lhs(d), rhs(kr, 0), rhs(kn, 0), rhs(d, 0), rhs(d, n // tn)],
        out_specs=pl.BlockSpec((tm, tn), lambda i, j: (i, j)),
        out_shape=jax.ShapeDtypeStruct((m, n), BF16),
        compiler_params=_params("parallel", "arbitrary"), name="merge",
    )(y_ret, y_nsa, h, w_ru, w_nu, w_g, w_g)


def _retention_kernel(q_ref, k_ref, v_ref, g_ref, di_ref, dq_ref, dk_ref, dc_ref, o_ref,
                      state_ref, *, n_chunks, chunk, k_scale):
    state_ref[...] = jnp.zeros_like(state_ref)
    hp = state_ref.shape[0]
    dk = state_ref.shape[1]

    def body(n, carry):
        rows = pl.ds(pl.multiple_of(n * chunk, chunk), chunk)
        for hh in range(hp):
            cols = slice(hh * dk, (hh + 1) * dk)
            q = q_ref[rows, cols]
            kf = k_ref[rows, cols].astype(F32) * k_scale
            v = v_ref[rows, cols]
            s = _dot_nt(q, kf.astype(BF16)) * di_ref[hh]
            state = state_ref[hh]
            o = _dot(s.astype(BF16), v) + _dot(q, state.astype(BF16)) * dq_ref[hh]
            state_ref[hh] = state * dc_ref[hh, 0:1, :] + _dot_tn((kf * dk_ref[hh]).astype(BF16), v)
            mu = jnp.mean(o, axis=-1, keepdims=True)
            oc = o - mu
            var = jnp.mean(oc * oc, axis=-1, keepdims=True)
            gate = g_ref[rows, cols].astype(F32)
            y = gate * jax.nn.sigmoid(gate) * (oc * lax.rsqrt(var + EPS))
            o_ref[rows, cols] = y.astype(o_ref.dtype)
        return carry

    lax.fori_loop(0, n_chunks, body, 0)


def _retention(proj, batch, seq, dk, hp=4):
    heads, chunk = RET_HEADS, RET_CHUNK
    assert heads % hp == 0
    dv = dk
    n_chunks = seq // chunk
    log_gamma = jnp.log1p(-jnp.exp2(-5.0 - jnp.arange(heads, dtype=F32)))
    pos = jnp.arange(chunk, dtype=F32)
    rel = pos[:, None] - pos[None, :]
    d_intra = jnp.where(rel >= 0, jnp.exp(log_gamma[:, None, None] * jnp.maximum(rel, 0.0)), 0.0)
    d_query = jnp.broadcast_to(jnp.exp(log_gamma[:, None] * (pos + 1.0))[..., None], (heads, chunk, dv))
    d_key = jnp.broadcast_to(jnp.exp(log_gamma[:, None] * (chunk - 1.0 - pos))[..., None], (heads, chunk, dk))
    d_chunk = jnp.broadcast_to(jnp.exp(log_gamma * chunk)[:, None, None], (heads, 8, dv))

    col = lambda off: pl.BlockSpec((seq, hp * dk), lambda b, h: (b, off // hp + h))
    tab = lambda r, c: pl.BlockSpec((hp, r, c), lambda b, h: (h, 0, 0))
    return pl.pallas_call(
        functools.partial(_retention_kernel, n_chunks=n_chunks, chunk=chunk, k_scale=dk ** -0.5),
        grid=(batch, heads // hp),
        in_specs=[col(0), col(heads), col(2 * heads), col(3 * heads),
                  tab(chunk, chunk), tab(chunk, dv), tab(chunk, dk), tab(8, dv)],
        out_specs=pl.BlockSpec((seq, hp * dv), lambda b, h: (b, h)),
        out_shape=jax.ShapeDtypeStruct((batch * seq, heads * dv), BF16),
        scratch_shapes=[pltpu.VMEM((hp, dk, dv), F32)],
        compiler_params=_params("parallel", "parallel"), name="retention",
    )(proj, proj, proj, proj, d_intra, d_query, d_key, d_chunk)


def _compress_kernel(x_ref, pos_ref, w1_ref, w2_ref, o_ref, x32_ref, *, n_cmp, n_chunks):
    dh = NSA_DH
    half = CMP_BLOCK // 2
    x32_ref[...] = x_ref[...].astype(F32)
    lo = jnp.zeros((n_chunks, w1_ref.shape[1]), F32)
    hi = jnp.zeros((n_chunks, w1_ref.shape[1]), F32)
    for l in range(half):
        xl = x32_ref[pl.ds(l, n_chunks, stride=CMP_STRIDE), :]
        a = (xl + pos_ref[l:l + 1, :]).astype(BF16)
        b = (xl + pos_ref[half + l:half + l + 1, :]).astype(BF16)
        lo = lo + _dot(a, w1_ref[l * dh:(l + 1) * dh, :])
        hi = hi + _dot(b, w1_ref[(half + l) * dh:(half + l + 1) * dh, :])
    pre = lo + pltpu.roll(hi, n_chunks - 1, axis=0)
    hid = pre * jax.nn.sigmoid(pre)
    out = _dot(hid.astype(BF16), w2_ref[...])
    row = lax.broadcasted_iota(jnp.int32, out.shape, 0)
    o_ref[...] = jnp.where(row < n_cmp, out, 0.0).astype(o_ref.dtype)


def _compress(proj, col_blk, pos, w1, w2, batch, seq):
    kvh, dh = NSA_KV_HEADS, NSA_DH
    n_chunks = seq // CMP_STRIDE
    n_cmp = (seq - CMP_BLOCK) // CMP_STRIDE + 1
    assert CMP_BLOCK == 2 * CMP_STRIDE and n_cmp == n_chunks - 1
    hidden = w1.shape[-1]
    return pl.pallas_call(
        functools.partial(_compress_kernel, n_cmp=n_cmp, n_chunks=n_chunks),
        grid=(2, batch, kvh),
        in_specs=[pl.BlockSpec((seq, dh), lambda c, b, g: (b, col_blk + kvh * c + g)),
                  pl.BlockSpec((None, CMP_BLOCK, dh), lambda c, b, g: (c, 0, 0)),
                  pl.BlockSpec((None, CMP_BLOCK * dh, hidden), lambda c, b, g: (c, 0, 0)),
                  pl.BlockSpec((None, hidden, dh), lambda c, b, g: (c, 0, 0))],
        out_specs=pl.BlockSpec((None, None, None, n_chunks, dh), lambda c, b, g: (c, b, g, 0, 0)),
        out_shape=jax.ShapeDtypeStruct((2, batch, kvh, n_chunks, dh), BF16),
        scratch_shapes=[pltpu.VMEM((seq, dh), F32)],
        compiler_params=_params("parallel", "parallel", "parallel"), name="nsa_compress",
    )(proj, pos, w1, w2)


def _nsa_kernel(q_ref, ks_ref, vs_ref, kw_ref, vw_ref, kc_ref, vc_ref, gl_ref, slope_ref, ovt_ref,
                et_ref, dw_ref, o_ref, *, tq, tk, n_cmp, n_sel, k_top):
    dh = NSA_DH
    grp = NSA_GROUP
    sc = (dh ** -0.5) * LOG2E
    span = WINDOW + tq
    t0 = pl.program_id(2) * tq
    q = q_ref[...]
    q4 = jnp.concatenate([q[:, r * dh:(r + 1) * dh] for r in range(grp)], axis=0)
    slope_row = slope_ref[0:1, :]
    tile4 = lambda a: jnp.concatenate([a] * grp, axis=1)

    kc = kc_ref[...]
    vc = vc_ref[...]
    nc = kc.shape[0]
    n_idx = lax.broadcasted_iota(jnp.int32, (nc, tq), 0)
    t_idx = t0 + lax.broadcasted_iota(jnp.int32, (nc, tq), 1)
    dist_c = t_idx - (CMP_STRIDE * n_idx + (CMP_BLOCK - 1))
    dist_c = jnp.where(n_idx < n_cmp, dist_c, -1)
    dist_c = tile4(jnp.where(dist_c >= 0, dist_c.astype(F32), MASK_DIST))
    s = _dot_nt(kc, q4) * sc - slope_row * dist_c
    m = jnp.max(s, axis=0, keepdims=True)
    e = jnp.where(dist_c < MASK_DIST, jnp.exp2(s - m), 0.0)
    l = jnp.sum(e, axis=0, keepdims=True)
    p = e / jnp.where(l > 0.0, l, 1.0)
    o_cmp = _dot_tn(vc, p.astype(BF16))
    p_sum = p[:, 0:tq]
    for r in range(1, grp):
        p_sum = p_sum + p[:, r * tq:(r + 1) * tq]

    p_hi = p_sum.astype(BF16)
    p_lo = (p_sum - p_hi.astype(F32)).astype(BF16)
    ovt = ovt_ref[...]
    imp = _dot(ovt, p_hi) + _dot(ovt, p_lo)
    blk = lax.broadcasted_iota(jnp.int32, (n_sel, tq), 0)
    t_sel = t0 + lax.broadcasted_iota(jnp.int32, (n_sel, tq), 1)
    back = jnp.right_shift(t_sel, SEL_SHIFT) - blk
    forced = jnp.where(blk == 0, 0, jnp.where(back < 0, SEL_LOCAL, back)) < SEL_LOCAL
    imp = jnp.where(forced, FORCED_SCORE, jnp.where(blk * SEL_BLOCK > t_sel, -1.0, imp))
    rank = jnp.zeros((n_sel, tq), F32)
    for j in range(n_sel):
        cand = imp[j:j + 1, :]
        tie = jnp.where(blk > j, 1.0, 0.0)
        rank = rank + jnp.where(cand > imp, 1.0, jnp.where(cand == imp, tie, 0.0))
    unsel = jnp.where(rank < k_top, 0.0, 1.0)
    unsel = jnp.concatenate([unsel, jnp.zeros((LANES - n_sel, tq), F32)], axis=0).astype(BF16)

    case = jnp.minimum(pl.program_id(2), WINDOW // tq)
    w0 = pl.multiple_of(jnp.maximum(t0 - WINDOW, 0), tq)
    s = _dot_nt(kw_ref[pl.ds(w0, span), :], q4) * sc - slope_row * tile4(dw_ref[case])
    p = jnp.exp2(s - jnp.max(s, axis=0, keepdims=True))
    o_win = _dot_tn(vw_ref[pl.ds(w0, span), :], p.astype(BF16)) / jnp.sum(p, axis=0, keepdims=True)

    rel = (lax.broadcasted_iota(jnp.int32, (tk, tq), 1)
           - lax.broadcasted_iota(jnp.int32, (tk, tq), 0)).astype(F32)

    def sel_body(kt, carry):
        m, l, acc = carry
        k0 = pl.multiple_of(kt * tk, tk)
        k = ks_ref[pl.ds(k0, tk), :]
        v = vs_ref[pl.ds(k0, tk), :]
        dist = rel + (t0 - k0).astype(F32)
        dist = jnp.where(dist >= 0.0, dist, MASK_DIST) + _dot(et_ref[pl.ds(k0, tk), :], unsel)
        s = _dot_nt(k, q4) * sc - slope_row * tile4(dist)
        m_new = jnp.maximum(m, jnp.max(s, axis=0, keepdims=True))
        alpha = jnp.exp2(m - m_new)
        p = jnp.exp2(s - m_new)
        l = alpha * l + jnp.sum(p, axis=0, keepdims=True)
        acc = alpha * acc + _dot_tn(v, p.astype(BF16))
        return m_new, l, acc

    n_kt = (t0 + tq + tk - 1) // tk
    init = (jnp.full((1, grp * tq), NEG_INF, F32), jnp.zeros((1, grp * tq), F32),
            jnp.zeros((dh, grp * tq), F32))
    _, l_sel, acc_sel = lax.fori_loop(0, n_kt, sel_body, init)
    o_sel = acc_sel / l_sel

    gate = jax.nn.sigmoid(gl_ref[...].T)
    for r in range(grp):
        cols = slice(r * tq, (r + 1) * tq)
        out = (gate[3 * r:3 * r + 1, :] * o_cmp[:, cols]
               + gate[3 * r + 1:3 * r + 2, :] * o_sel[:, cols]
               + gate[3 * r + 2:3 * r + 3, :] * o_win[:, cols])
        o_ref[:, r * dh:(r + 1) * dh] = out.T.astype(o_ref.dtype)


def _nsa(proj, q_blk, kv_blk, cmp_kv, gate_logits, batch, seq, tq=256, tk=512):
    dh, kvh, grp = NSA_DH, NSA_KV_HEADS, NSA_GROUP
    n_cmp = (seq - CMP_BLOCK) // CMP_STRIDE + 1
    nc = cmp_kv.shape[3]
    n_sel = seq // SEL_BLOCK
    k_top = min(SEL_TOPK, n_sel)
    span = WINDOW + tq
    n_case = WINDOW // tq + 1
    assert seq % tk == 0 and seq >= span and WINDOW % tq == 0 and n_sel <= LANES and tk % SEL_BLOCK == 0

    slopes = jnp.exp2(-8.0 * (jnp.arange(NSA_HEADS, dtype=F32) + 1.0) / NSA_HEADS).reshape(kvh, grp)
    slope_tab = jnp.broadcast_to(jnp.repeat(slopes * LOG2E, tq, axis=1)[:, None, :], (kvh, 8, grp * tq))
    cmp_start = CMP_STRIDE * jnp.arange(n_cmp)
    sel_start = SEL_BLOCK * jnp.arange(n_sel)
    overlap = jnp.clip(jnp.minimum(cmp_start[:, None] + CMP_BLOCK, sel_start[None, :] + SEL_BLOCK)
                       - jnp.maximum(cmp_start[:, None], sel_start[None, :]), 0).astype(F32) / CMP_BLOCK
    ovt = jnp.zeros((n_sel, nc), F32).at[:, :n_cmp].set(overlap.T).astype(BF16)
    e_t = jnp.where(jnp.arange(seq)[:, None] // SEL_BLOCK == jnp.arange(LANES)[None, :],
                    MASK_POW2, 0.0).astype(BF16)
    dist_w = (jnp.arange(n_case)[:, None, None] * tq + jnp.arange(tq)[None, None, :]
              - jnp.arange(span)[None, :, None])
    dist_w = jnp.where((dist_w >= 0) & (dist_w < WINDOW), dist_w.astype(F32), MASK_DIST)

    nq = seq // tq
    kv_spec = lambda c: pl.BlockSpec((seq, dh), lambda b, g, i: (b, kv_blk + kvh * c + g))
    cmp_spec = lambda c: pl.BlockSpec((None, None, None, nc, dh), lambda b, g, i: (c, b, g, 0, 0))
    whole = lambda a: pl.BlockSpec(a.shape, lambda b, g, i: (0,) * a.ndim)
    return pl.pallas_call(
        functools.partial(_nsa_kernel, tq=tq, tk=tk, n_cmp=n_cmp, n_sel=n_sel, k_top=k_top),
        grid=(batch, kvh, nq),
        in_specs=[pl.BlockSpec((tq, grp * dh), lambda b, g, i: (b * nq + i, q_blk + g)),
                  kv_spec(0), kv_spec(1), kv_spec(2), kv_spec(3), cmp_spec(0), cmp_spec(1),
                  pl.BlockSpec((tq, LANES), lambda b, g, i: (b * nq + i, g)),
                  pl.BlockSpec((None, 8, grp * tq), lambda b, g, i: (g, 0, 0)),
                  whole(ovt), whole(e_t), whole(dist_w)],
        out_specs=pl.BlockSpec((tq, grp * dh), lambda b, g, i: (b * nq + i, g)),
        out_shape=jax.ShapeDtypeStruct((batch * seq, kvh * grp * dh), BF16),
        compiler_params=_params("parallel", "parallel", "arbitrary"), name="nsa_attention",
    )(proj, proj, proj, proj, proj, cmp_kv, cmp_kv, gate_logits, slope_tab, ovt, e_t, dist_w)


def kernel(x, mix_norm_pre, w_in, cmp_pos_k, cmp_w1_k, cmp_w2_k, cmp_pos_v, cmp_w1_v, cmp_w2_v,
           w_ret_up, w_nsa_up, w_out, mix_norm_post, mlp_norm_pre, w_mlp_in, w_mlp_out,
           mlp_norm_post):
    batch, seq, d = x.shape
    depth = w_in.shape[0]
    ret_w = w_ret_up.shape[1]
    nsa_w = w_nsa_up.shape[1]
    kvh, dh = NSA_KV_HEADS, NSA_DH
    kv_w = kvh * dh
    n_gate = 3 * NSA_HEADS
    c_qn = 4 * ret_w
    c_kv = c_qn + nsa_w
    c_gate = c_kv + 6 * kv_w
    c_gr = c_gate + n_gate
    c_gn = c_gr + d

    per_grp = n_gate // kvh
    w_gate = jnp.pad(w_in[:, :, c_gate:c_gr].reshape(depth, d, kvh, per_grp),
                     ((0, 0), (0, 0), (0, 0), (0, LANES - per_grp))).reshape(depth, d, kvh * LANES).astype(BF16)
    w_g = _realign_cast(w_in, c_gr, 2 * d)
    w_ru = w_ret_up.astype(BF16)
    w_nu = w_nsa_up.astype(BF16)
    cmp_pos = jnp.stack([cmp_pos_k, cmp_pos_v], axis=1)
    cmp_w1 = jnp.stack([cmp_w1_k, cmp_w1_v], axis=1).astype(BF16)
    cmp_w2 = jnp.stack([cmp_w2_k, cmp_w2_v], axis=1).astype(BF16)

    xs = x.reshape(batch * seq, d)
    h = _rmsnorm(xs, mix_norm_pre[0])
    for l in range(depth):
        proj = _matmul_ws(h, w_in, l, 0, c_gate, BF16, name="proj_in")
        gate_logits = _matmul(h, w_gate[l], F32, name="proj_nsa_gate")
        y_ret = _retention(proj, batch, seq, ret_w // RET_HEADS)
        cmp_kv = _compress(proj, c_kv // dh, cmp_pos[l], cmp_w1[l], cmp_w2[l], batch, seq)
        y_nsa = _nsa(proj, c_qn // (NSA_GROUP * dh), (c_kv + 2 * kv_w) // dh, cmp_kv, gate_logits,
                     batch, seq)
        merged = _merge(y_ret, y_nsa, h, w_ru, w_nu, w_g, l)
        mix = _matmul_ws(merged, w_out, l, 0, d, BF16, name="proj_out")
        xs, h = _post(mix, xs, mix_norm_post[l], mlp_norm_pre[l])
        u = _matmul_ws(h, w_mlp_in, l, 0, w_mlp_in.shape[2], BF16, relu2=True, name="mlp_in")
        y = _matmul_ktiled(u, w_mlp_out, l, BF16, name="mlp_out")
        xs, h = _post(y, xs, mlp_norm_post[l], mix_norm_pre[l + 1] if l + 1 < depth else None)
    return xs.reshape(batch, seq, d)
```

```python
import functools
import math

import numpy as np
import jax
import jax.numpy as jnp
from jax import lax
from jax.experimental import pallas as pl
from jax.experimental.pallas import tpu as pltpu

RET_HEADS = 8
RET_CHUNK = 128
NSA_HEADS = 16
NSA_KV_HEADS = 4
NSA_DH = 128
CMP_BLOCK = 32
CMP_STRIDE = 16
SEL_BLOCK = 64
SEL_TOPK = 16
SEL_LOCAL = 2
WINDOW = 512
EPS = 1e-6
NEG_INF = -1e30
FORCED_SCORE = 1e4

NSA_GROUP = NSA_HEADS // NSA_KV_HEADS
SEL_SHIFT = SEL_BLOCK.bit_length() - 1
LANES = 128
VMEM_LIMIT = 56 * 1024 * 1024
LOG2E = math.log2(math.e)
MASK_DIST = 1e32
MASK_POW2 = 2.0 ** 100

F32 = jnp.float32
BF16 = jnp.bfloat16


def _params(*semantics):
    return pltpu.CompilerParams(dimension_semantics=semantics, vmem_limit_bytes=VMEM_LIMIT)


def _dot(a, b):
    return jnp.dot(a, b, preferred_element_type=F32)


def _dot_nt(a, b):
    return lax.dot_general(a, b, (((1,), (1,)), ((), ())), preferred_element_type=F32)


def _dot_tn(a, b):
    return lax.dot_general(a, b, (((0,), (0,)), ((), ())), preferred_element_type=F32)


def _rmsnorm_kernel(x_ref, g_ref, o_ref):
    x = x_ref[...]
    ms = jnp.mean(x * x, axis=-1, keepdims=True)
    o_ref[...] = (x * lax.rsqrt(ms + EPS) * g_ref[...]).astype(o_ref.dtype)


def _rmsnorm(x, gain, tm=256):
    m, d = x.shape
    return pl.pallas_call(
        _rmsnorm_kernel,
        grid=(m // tm,),
        in_specs=[pl.BlockSpec((tm, d), lambda i: (i, 0)),
                  pl.BlockSpec((1, d), lambda i: (0, 0))],
        out_specs=pl.BlockSpec((tm, d), lambda i: (i, 0)),
        out_shape=jax.ShapeDtypeStruct((m, d), BF16),
        compiler_params=_params("parallel"),
        name="rmsnorm",
    )(x, gain.reshape(1, d))


def _post_kernel(y_ref, x_ref, gp_ref, gn_ref, xo_ref, ho_ref):
    y = y_ref[...].astype(F32)
    yn = y * lax.rsqrt(jnp.mean(y * y, axis=-1, keepdims=True) + EPS) * gp_ref[...]
    xn = x_ref[...] + yn
    xo_ref[...] = xn
    ho_ref[...] = (xn * lax.rsqrt(jnp.mean(xn * xn, axis=-1, keepdims=True) + EPS)
                   * gn_ref[...]).astype(ho_ref.dtype)


def _post_last_kernel(y_ref, x_ref, gp_ref, xo_ref):
    y = y_ref[...].astype(F32)
    yn = y * lax.rsqrt(jnp.mean(y * y, axis=-1, keepdims=True) + EPS) * gp_ref[...]
    xo_ref[...] = x_ref[...] + yn


def _post(y, x, g_post, g_next, tm=256):
    m, d = x.shape
    row = pl.BlockSpec((tm, d), lambda i: (i, 0))
    vec = pl.BlockSpec((1, d), lambda i: (0, 0))
    if g_next is None:
        return pl.pallas_call(
            _post_last_kernel, grid=(m // tm,),
            in_specs=[row, row, vec], out_specs=row,
            out_shape=jax.ShapeDtypeStruct((m, d), F32),
            compiler_params=_params("parallel"), name="post_last",
        )(y, x, g_post.reshape(1, d)), None
    return pl.pallas_call(
        _post_kernel, grid=(m // tm,),
        in_specs=[row, row, vec, vec], out_specs=[row, row],
        out_shape=[jax.ShapeDtypeStruct((m, d), F32), jax.ShapeDtypeStruct((m, d), BF16)],
        compiler_params=_params("parallel"), name="post",
    )(y, x, g_post.reshape(1, d), g_next.reshape(1, d))


def _matmul_nt_kernel(a_ref, b_ref, o_ref):
    o_ref[...] = _dot_nt(a_ref[...], b_ref[...]).astype(o_ref.dtype)


def _matmul_nt(a, b_t, out_dtype, tm=1024, tn=1024, name="matmul"):
    m, k = a.shape
    n = b_t.shape[0]
    tm, tn = min(tm, m), min(tn, n)
    return pl.pallas_call(
        _matmul_nt_kernel,
        grid=(m // tm, n // tn),
        in_specs=[pl.BlockSpec((tm, k), lambda i, j: (i, 0)),
                  pl.BlockSpec((tn, k), lambda i, j: (j, 0))],
        out_specs=pl.BlockSpec((tm, tn), lambda i, j: (i, j)),
        out_shape=jax.ShapeDtypeStruct((m, n), out_dtype),
        compiler_params=_params("parallel", "arbitrary"), name=name,
    )(a, b_t)


def _matmul_ws_kernel(a_ref, w_ref, o_ref, wq_ref, *, relu2, transposed):
    @pl.when(pl.program_id(1) == 0)
    def _():
        wq_ref[...] = w_ref[...].astype(BF16)

    acc = _dot_nt(a_ref[...], wq_ref[...]) if transposed else _dot(a_ref[...], wq_ref[...])
    if relu2:
        acc = jnp.square(jnp.maximum(acc, 0.0))
    o_ref[...] = acc.astype(o_ref.dtype)


def _matmul_ws(a, w, layer, col0, n, out_dtype, tm=1024, tn=512, relu2=False, transposed=False,
               name="matmul_ws"):
    m, k = a.shape
    tm, tn = min(tm, m), min(tn, n)
    assert col0 % tn == 0 and n % tn == 0 and m % tm == 0
    j0 = col0 // tn
    if transposed:
        w_spec = pl.BlockSpec((None, tn, k), lambda j, i: (layer, j0 + j, 0))
    else:
        w_spec = pl.BlockSpec((None, k, tn), lambda j, i: (layer, 0, j0 + j))
    return pl.pallas_call(
        functools.partial(_matmul_ws_kernel, relu2=relu2, transposed=transposed),
        grid=(n // tn, m // tm),
        in_specs=[pl.BlockSpec((tm, k), lambda j, i: (i, 0)), w_spec],
        out_specs=pl.BlockSpec((tm, tn), lambda j, i: (i, j)),
        out_shape=jax.ShapeDtypeStruct((m, n), out_dtype),
        scratch_shapes=[pltpu.VMEM((tn, k) if transposed else (k, tn), BF16)],
        compiler_params=_params("parallel", "arbitrary"), name=name,
    )(a, w)


def _matmul_ktiled_kernel(a_ref, w_ref, o_ref, acc_ref):
    kk = pl.program_id(2)

    @pl.when(kk == 0)
    def _():
        acc_ref[...] = jnp.zeros_like(acc_ref)

    acc_ref[...] += _dot(a_ref[...], w_ref[...].astype(BF16))

    @pl.when(kk == pl.num_programs(2) - 1)
    def _():
        o_ref[...] = acc_ref[...].astype(o_ref.dtype)


def _matmul_ktiled(a, w, layer, out_dtype, tm=2048, tn=1024, tk=1024, name="matmul_k"):
    m, k = a.shape
    n = w.shape[2]
    tm, tn, tk = min(tm, m), min(tn, n), min(tk, k)
    return pl.pallas_call(
        _matmul_ktiled_kernel,
        grid=(m // tm, n // tn, k // tk),
        in_specs=[pl.BlockSpec((tm, tk), lambda i, j, kk: (i, kk)),
                  pl.BlockSpec((None, tk, tn), lambda i, j, kk: (layer, kk, j))],
        out_specs=pl.BlockSpec((tm, tn), lambda i, j, kk: (i, j)),
        out_shape=jax.ShapeDtypeStruct((m, n), out_dtype),
        scratch_shapes=[pltpu.VMEM((tm, tn), F32)],
        compiler_params=_params("parallel", "parallel", "arbitrary"), name=name,
    )(a, w)


def _cast_rows_kernel(w_ref, o_ref):
    o_ref[...] = w_ref[0].astype(o_ref.dtype)


def _cast_rows(w, row0, n, tr=512):
    depth, _, k = w.shape
    tr = min(tr, n)
    assert n % tr == 0 and row0 % 16 == 0
    return pl.pallas_call(
        _cast_rows_kernel,
        grid=(depth, n // tr),
        in_specs=[pl.BlockSpec((pl.Element(1), pl.Element(tr), pl.Element(k)),
                               lambda l, j: (l, pl.multiple_of(row0 + j * tr, 16), 0))],
        out_specs=pl.BlockSpec((None, tr, k), lambda l, j: (l, j, 0)),
        out_shape=jax.ShapeDtypeStruct((depth, n, k), BF16),
        compiler_params=_params("parallel", "parallel"), name="cast_rows",
    )(w)


def _merge_kernel(yr_ref, yn_ref, h_ref, wru_ref, wnu_ref, wgr_ref, wgn_ref, o_ref):
    h = h_ref[...]
    a = _dot(yr_ref[...], wru_ref[...])
    b = _dot(yn_ref[...], wnu_ref[...])
    gr = jax.nn.sigmoid(_dot_nt(h, wgr_ref[...]))
    gn = jax.nn.sigmoid(_dot_nt(h, wgn_ref[...]))
    o_ref[...] = (gr * a + gn * b).astype(o_ref.dtype)


def _merge(y_ret, y_nsa, h, w_ru, w_nu, w_g_t, layer, tm=512, tn=512):
    m, d = h.shape
    kr, kn = y_ret.shape[1], y_nsa.shape[1]
    n = w_ru.shape[2]
    tm, tn = min(tm, m), min(tn, n)
    lhs = lambda kk: pl.BlockSpec((tm, kk), lambda i, j: (i, 0))
    rhs = lambda kk: pl.BlockSpec((None, kk, tn), lambda i, j: (layer, 0, j))
    rhs_t = lambda j0: pl.BlockSpec((None, tn, d), lambda i, j: (layer, j0 + j, 0))
    return pl.pallas_call(
        _merge_kernel, grid=(m // tm, n // tn),
        in_specs=[lhs(kr), lhs(kn), lhs(d), rhs(kr), rhs(kn), rhs_t(0), rhs_t(n // tn)],
        out_specs=pl.BlockSpec((tm, tn), lambda i, j: (i, j)),
        out_shape=jax.ShapeDtypeStruct((m, n), BF16),
        compiler_params=_params("parallel", "arbitrary"), name="merge",
    )(y_ret, y_nsa, h, w_ru, w_nu, w_g_t, w_g_t)


def _retention_kernel(q_ref, k_ref, v_ref, g_ref, di_ref, dq_ref, dk_ref, dc_ref, o_ref,
                      state_ref, *, n_chunks, chunk, k_scale):
    state_ref[...] = jnp.zeros_like(state_ref)
    hp = state_ref.shape[0]
    dk = state_ref.shape[1]

    def body(n, carry):
        rows = pl.ds(pl.multiple_of(n * chunk, chunk), chunk)
        for hh in range(hp):
            cols = slice(hh * dk, (hh + 1) * dk)
            q = q_ref[rows, cols]
            kf = k_ref[rows, cols].astype(F32) * k_scale
            v = v_ref[rows, cols]
            s = _dot_nt(q, kf.astype(BF16)) * di_ref[hh]
            state = state_ref[hh]
            o = _dot(s.astype(BF16), v) + _dot(q, state.astype(BF16)) * dq_ref[hh]
            state_ref[hh] = state * dc_ref[hh, 0:1, :] + _dot_tn((kf * dk_ref[hh]).astype(BF16), v)
            mu = jnp.mean(o, axis=-1, keepdims=True)
            oc = o - mu
            var = jnp.mean(oc * oc, axis=-1, keepdims=True)
            gate = g_ref[rows, cols].astype(F32)
            y = gate * jax.nn.sigmoid(gate) * (oc * lax.rsqrt(var + EPS))
            o_ref[rows, cols] = y.astype(o_ref.dtype)
        return carry

    lax.fori_loop(0, n_chunks, body, 0)


def _retention(proj, batch, seq, dk, hp=4):
    heads, chunk = RET_HEADS, RET_CHUNK
    assert heads % hp == 0
    dv = dk
    n_chunks = seq // chunk
    log_gamma = jnp.log1p(-jnp.exp2(-5.0 - jnp.arange(heads, dtype=F32)))
    pos = jnp.arange(chunk, dtype=F32)
    rel = pos[:, None] - pos[None, :]
    d_intra = jnp.where(rel >= 0, jnp.exp(log_gamma[:, None, None] * jnp.maximum(rel, 0.0)), 0.0)
    d_query = jnp.broadcast_to(jnp.exp(log_gamma[:, None] * (pos + 1.0))[..., None], (heads, chunk, dv))
    d_key = jnp.broadcast_to(jnp.exp(log_gamma[:, None] * (chunk - 1.0 - pos))[..., None], (heads, chunk, dk))
    d_chunk = jnp.broadcast_to(jnp.exp(log_gamma * chunk)[:, None, None], (heads, 8, dv))

    col = lambda off: pl.BlockSpec((seq, hp * dk), lambda b, h: (b, off // hp + h))
    tab = lambda r, c: pl.BlockSpec((hp, r, c), lambda b, h: (h, 0, 0))
    return pl.pallas_call(
        functools.partial(_retention_kernel, n_chunks=n_chunks, chunk=chunk, k_scale=dk ** -0.5),
        grid=(batch, heads // hp),
        in_specs=[col(0), col(heads), col(2 * heads), col(3 * heads),
                  tab(chunk, chunk), tab(chunk, dv), tab(chunk, dk), tab(8, dv)],
        out_specs=pl.BlockSpec((seq, hp * dv), lambda b, h: (b, h)),
        out_shape=jax.ShapeDtypeStruct((batch * seq, heads * dv), BF16),
        scratch_shapes=[pltpu.VMEM((hp, dk, dv), F32)],
        compiler_params=_params("parallel", "parallel"), name="retention",
    )(proj, proj, proj, proj, d_intra, d_query, d_key, d_chunk)


def _compress_kernel(x_ref, pos_ref, w1_ref, w2_ref, o_ref, x32_ref, *, n_cmp, n_chunks):
    dh = NSA_DH
    half = CMP_BLOCK // 2
    x32_ref[...] = x_ref[...].astype(F32)
    lo = jnp.zeros((n_chunks, w1_ref.shape[1]), F32)
    hi = jnp.zeros((n_chunks, w1_ref.shape[1]), F32)
    for l in range(half):
        xl = x32_ref[pl.ds(l, n_chunks, stride=CMP_STRIDE), :]
        a = (xl + pos_ref[l:l + 1, :]).astype(BF16)
        b = (xl + pos_ref[half + l:half + l + 1, :]).astype(BF16)
        lo = lo + _dot(a, w1_ref[l * dh:(l + 1) * dh, :])
        hi = hi + _dot(b, w1_ref[(half + l) * dh:(half + l + 1) * dh, :])
    pre = lo + pltpu.roll(hi, n_chunks - 1, axis=0)
    hid = pre * jax.nn.sigmoid(pre)
    out = _dot(hid.astype(BF16), w2_ref[...])
    row = lax.broadcasted_iota(jnp.int32, out.shape, 0)
    o_ref[...] = jnp.where(row < n_cmp, out, 0.0).astype(o_ref.dtype)


def _compress(proj, col_blk, pos, w1, w2, batch, seq):
    kvh, dh = NSA_KV_HEADS, NSA_DH
    n_chunks = seq // CMP_STRIDE
    n_cmp = (seq - CMP_BLOCK) // CMP_STRIDE + 1
    assert CMP_BLOCK == 2 * CMP_STRIDE and n_cmp == n_chunks - 1
    hidden = w1.shape[-1]
    return pl.pallas_call(
        functools.partial(_compress_kernel, n_cmp=n_cmp, n_chunks=n_chunks),
        grid=(2, batch, kvh),
        in_specs=[pl.BlockSpec((seq, dh), lambda c, b, g: (b, col_blk + kvh * c + g)),
                  pl.BlockSpec((None, CMP_BLOCK, dh), lambda c, b, g: (c, 0, 0)),
                  pl.BlockSpec((None, CMP_BLOCK * dh, hidden), lambda c, b, g: (c, 0, 0)),
                  pl.BlockSpec((None, hidden, dh), lambda c, b, g: (c, 0, 0))],
        out_specs=pl.BlockSpec((None, None, None, n_chunks, dh), lambda c, b, g: (c, b, g, 0, 0)),
        out_shape=jax.ShapeDtypeStruct((2, batch, kvh, n_chunks, dh), BF16),
        scratch_shapes=[pltpu.VMEM((seq, dh), F32)],
        compiler_params=_params("parallel", "parallel", "parallel"), name="nsa_compress",
    )(proj, pos, w1, w2)


def _nsa_kernel(q_ref, ks_ref, vs_ref, kw_ref, vw_ref, kc_ref, vc_ref, gl_ref, slope_ref, ovt_ref,
                et_ref, dw_ref, o_ref, *, tq, tk, n_cmp, n_sel, k_top):
    dh = NSA_DH
    grp = NSA_GROUP
    sc = (dh ** -0.5) * LOG2E
    span = WINDOW + tq
    t0 = pl.program_id(2) * tq
    q = q_ref[...]
    q4 = jnp.concatenate([q[:, r * dh:(r + 1) * dh] for r in range(grp)], axis=0)
    slope_row = slope_ref[0:1, :]
    tile4 = lambda a: jnp.concatenate([a] * grp, axis=1)

    kc = kc_ref[...]
    vc = vc_ref[...]
    nc = kc.shape[0]
    n_idx = lax.broadcasted_iota(jnp.int32, (nc, tq), 0)
    t_idx = t0 + lax.broadcasted_iota(jnp.int32, (nc, tq), 1)
    dist_c = t_idx - (CMP_STRIDE * n_idx + (CMP_BLOCK - 1))
    dist_c = jnp.where(n_idx < n_cmp, dist_c, -1)
    dist_c = tile4(jnp.where(dist_c >= 0, dist_c.astype(F32), MASK_DIST))
    s = _dot_nt(kc, q4) * sc - slope_row * dist_c
    m = jnp.max(s, axis=0, keepdims=True)
    e = jnp.where(dist_c < MASK_DIST, jnp.exp2(s - m), 0.0)
    l = jnp.sum(e, axis=0, keepdims=True)
    p = e / jnp.where(l > 0.0, l, 1.0)
    o_cmp = _dot_tn(vc, p.astype(BF16))
    p_sum = p[:, 0:tq]
    for r in range(1, grp):
        p_sum = p_sum + p[:, r * tq:(r + 1) * tq]

    p_hi = p_sum.astype(BF16)
    p_lo = (p_sum - p_hi.astype(F32)).astype(BF16)
    ovt = ovt_ref[...]
    imp = _dot(ovt, p_hi) + _dot(ovt, p_lo)
    blk = lax.broadcasted_iota(jnp.int32, (n_sel, tq), 0)
    t_sel = t0 + lax.broadcasted_iota(jnp.int32, (n_sel, tq), 1)
    back = jnp.right_shift(t_sel, SEL_SHIFT) - blk
    forced = jnp.where(blk == 0, 0, jnp.where(back < 0, SEL_LOCAL, back)) < SEL_LOCAL
    imp = jnp.where(forced, FORCED_SCORE, jnp.where(blk * SEL_BLOCK > t_sel, -1.0, imp))
    rank = jnp.zeros((n_sel, tq), F32)
    for j in range(n_sel):
        cand = imp[j:j + 1, :]
        tie = jnp.where(blk > j, 1.0, 0.0)
        rank = rank + jnp.where(cand > imp, 1.0, jnp.where(cand == imp, tie, 0.0))
    unsel = jnp.where(rank < k_top, 0.0, 1.0)
    unsel = jnp.concatenate([unsel, jnp.zeros((LANES - n_sel, tq), F32)], axis=0).astype(BF16)

    case = jnp.minimum(pl.program_id(2), WINDOW // tq)
    w0 = pl.multiple_of(jnp.maximum(t0 - WINDOW, 0), tq)
    s = _dot_nt(kw_ref[pl.ds(w0, span), :], q4) * sc - slope_row * tile4(dw_ref[case])
    p = jnp.exp2(s - jnp.max(s, axis=0, keepdims=True))
    o_win = _dot_tn(vw_ref[pl.ds(w0, span), :], p.astype(BF16)) / jnp.sum(p, axis=0, keepdims=True)

    rel = (lax.broadcasted_iota(jnp.int32, (tk, tq), 1)
           - lax.broadcasted_iota(jnp.int32, (tk, tq), 0)).astype(F32)

    def sel_body(kt, carry):
        m, l, acc = carry
        k0 = pl.multiple_of(kt * tk, tk)
        k = ks_ref[pl.ds(k0, tk), :]
        v = vs_ref[pl.ds(k0, tk), :]
        dist = rel + (t0 - k0).astype(F32)
        dist = jnp.where(dist >= 0.0, dist, MASK_DIST) + _dot(et_ref[pl.ds(k0, tk), :], unsel)
        s = _dot_nt(k, q4) * sc - slope_row * tile4(dist)
        m_new = jnp.maximum(m, jnp.max(s, axis=0, keepdims=True))
        alpha = jnp.exp2(m - m_new)
        p = jnp.exp2(s - m_new)
        l = alpha * l + jnp.sum(p, axis=0, keepdims=True)
        acc = alpha * acc + _dot_tn(v, p.astype(BF16))
        return m_new, l, acc

    n_kt = (t0 + tq + tk - 1) // tk
    init = (jnp.full((1, grp * tq), NEG_INF, F32), jnp.zeros((1, grp * tq), F32),
            jnp.zeros((dh, grp * tq), F32))
    _, l_sel, acc_sel = lax.fori_loop(0, n_kt, sel_body, init)
    o_sel = acc_sel / l_sel

    gate = jax.nn.sigmoid(gl_ref[...].T)
    for r in range(grp):
        cols = slice(r * tq, (r + 1) * tq)
        out = (gate[3 * r:3 * r + 1, :] * o_cmp[:, cols]
               + gate[3 * r + 1:3 * r + 2, :] * o_sel[:, cols]
               + gate[3 * r + 2:3 * r + 3, :] * o_win[:, cols])
        o_ref[:, r * dh:(r + 1) * dh] = out.T.astype(o_ref.dtype)


def _nsa(proj, q_blk, kv_blk, cmp_kv, gate_logits, batch, seq, tq=256, tk=512):
    dh, kvh, grp = NSA_DH, NSA_KV_HEADS, NSA_GROUP
    n_cmp = (seq - CMP_BLOCK) // CMP_STRIDE + 1
    nc = cmp_kv.shape[3]
    n_sel = seq // SEL_BLOCK
    k_top = min(SEL_TOPK, n_sel)
    span = WINDOW + tq
    n_case = WINDOW // tq + 1
    assert seq % tk == 0 and seq >= span and WINDOW % tq == 0 and n_sel <= LANES and tk % SEL_BLOCK == 0

    slopes = jnp.exp2(-8.0 * (jnp.arange(NSA_HEADS, dtype=F32) + 1.0) / NSA_HEADS).reshape(kvh, grp)
    slope_tab = jnp.broadcast_to(jnp.repeat(slopes * LOG2E, tq, axis=1)[:, None, :], (kvh, 8, grp * tq))
    cmp_start = CMP_STRIDE * jnp.arange(n_cmp)
    sel_start = SEL_BLOCK * jnp.arange(n_sel)
    overlap = jnp.clip(jnp.minimum(cmp_start[:, None] + CMP_BLOCK, sel_start[None, :] + SEL_BLOCK)
                       - jnp.maximum(cmp_start[:, None], sel_start[None, :]), 0).astype(F32) / CMP_BLOCK
    ovt = jnp.zeros((n_sel, nc), F32).at[:, :n_cmp].set(overlap.T).astype(BF16)
    e_t = jnp.where(jnp.arange(seq)[:, None] // SEL_BLOCK == jnp.arange(LANES)[None, :],
                    MASK_POW2, 0.0).astype(BF16)
    dist_w = (jnp.arange(n_case)[:, None, None] * tq + jnp.arange(tq)[None, None, :]
              - jnp.arange(span)[None, :, None])
    dist_w = jnp.where((dist_w >= 0) & (dist_w < WINDOW), dist_w.astype(F32), MASK_DIST)

    nq = seq // tq
    kv_spec = lambda c: pl.BlockSpec((seq, dh), lambda b, g, i: (b, kv_blk + kvh * c + g))
    cmp_spec = lambda c: pl.BlockSpec((None, None, None, nc, dh), lambda b, g, i: (c, b, g, 0, 0))
    whole = lambda a: pl.BlockSpec(a.shape, lambda b, g, i: (0,) * a.ndim)
    return pl.pallas_call(
        functools.partial(_nsa_kernel, tq=tq, tk=tk, n_cmp=n_cmp, n_sel=n_sel, k_top=k_top),
        grid=(batch, kvh, nq),
        in_specs=[pl.BlockSpec((tq, grp * dh), lambda b, g, i: (b * nq + i, q_blk + g)),
                  kv_spec(0), kv_spec(1), kv_spec(2), kv_spec(3), cmp_spec(0), cmp_spec(1),
                  pl.BlockSpec((tq, LANES), lambda b, g, i: (b * nq + i, g)),
                  pl.BlockSpec((None, 8, grp * tq), lambda b, g, i: (g, 0, 0)),
                  whole(ovt), whole(e_t), whole(dist_w)],
        out_specs=pl.BlockSpec((tq, grp * dh), lambda b, g, i: (b * nq + i, g)),
        out_shape=jax.ShapeDtypeStruct((batch * seq, kvh * grp * dh), BF16),
        compiler_params=_params("parallel", "parallel", "arbitrary"), name="nsa_attention",
    )(proj, proj, proj, proj, proj, cmp_kv, cmp_kv, gate_logits, slope_tab, ovt, e_t, dist_w)


def kernel(x, mix_norm_pre, w_in, cmp_pos_k, cmp_w1_k, cmp_w2_k, cmp_pos_v, cmp_w1_v, cmp_w2_v,
           w_ret_up, w_nsa_up, w_out, mix_norm_post, mlp_norm_pre, w_mlp_in, w_mlp_out,
           mlp_norm_post):
    batch, seq, d = x.shape
    depth = w_in.shape[0]
    ret_w = w_ret_up.shape[1]
    nsa_w = w_nsa_up.shape[1]
    kvh, dh = NSA_KV_HEADS, NSA_DH
    kv_w = kvh * dh
    n_gate = 3 * NSA_HEADS
    c_qn = 4 * ret_w
    c_kv = c_qn + nsa_w
    c_gate = c_kv + 6 * kv_w
    c_gr = c_gate + n_gate
    c_gn = c_gr + d

    w_in_t = jnp.swapaxes(w_in, 1, 2)
    per_grp = n_gate // kvh
    w_gate_t = jnp.pad(w_in_t[:, c_gate:c_gr, :].reshape(depth, kvh, per_grp, d),
                       ((0, 0), (0, 0), (0, LANES - per_grp), (0, 0))).reshape(depth, kvh * LANES, d).astype(BF16)
    w_g_t = _cast_rows(w_in_t, c_gr, 2 * d)
    w_ru = w_ret_up.astype(BF16)
    w_nu = w_nsa_up.astype(BF16)
    cmp_pos = jnp.stack([cmp_pos_k, cmp_pos_v], axis=1)
    cmp_w1 = jnp.stack([cmp_w1_k, cmp_w1_v], axis=1).astype(BF16)
    cmp_w2 = jnp.stack([cmp_w2_k, cmp_w2_v], axis=1).astype(BF16)

    xs = x.reshape(batch * seq, d)
    h = _rmsnorm(xs, mix_norm_pre[0])
    for l in range(depth):
        proj = _matmul_ws(h, w_in_t, l, 0, c_gate, BF16, transposed=True, name="proj_in")
        gate_logits = _matmul_nt(h, w_gate_t[l], F32, name="proj_nsa_gate")
        y_ret = _retention(proj, batch, seq, ret_w // RET_HEADS)
        cmp_kv = _compress(proj, c_kv // dh, cmp_pos[l], cmp_w1[l], cmp_w2[l], batch, seq)
        y_nsa = _nsa(proj, c_qn // (NSA_GROUP * dh), (c_kv + 2 * kv_w) // dh, cmp_kv, gate_logits,
                     batch, seq)
        merged = _merge(y_ret, y_nsa, h, w_ru, w_nu, w_g_t, l)
        mix = _matmul_ws(merged, w_out, l, 0, d, BF16, name="proj_out")
        xs, h = _post(mix, xs, mix_norm_post[l], mlp_norm_pre[l])
        u = _matmul_ws(h, w_mlp_in, l, 0, w_mlp_in.shape[2], BF16, relu2=True, name="mlp_in")
        y = _matmul_ktiled(u, w_mlp_out, l, BF16, name="mlp_out")
        xs, h = _post(y, xs, mlp_norm_post[l], mix_norm_pre[l + 1] if l + 1 < depth else None)
    return xs.reshape(batch, seq, d)
```

```python
import functools
import math

import numpy as np
import jax
import jax.numpy as jnp
from jax import lax
from jax.experimental import pallas as pl
from jax.experimental.pallas import tpu as pltpu

RET_HEADS = 8
RET_CHUNK = 128
NSA_HEADS = 16
NSA_KV_HEADS = 4
NSA_DH = 128
CMP_BLOCK = 32
CMP_STRIDE = 16
SEL_BLOCK = 64
SEL_TOPK = 16
SEL_LOCAL = 2
WINDOW = 512
EPS = 1e-6
NEG_INF = -1e30
FORCED_SCORE = 1e4

NSA_GROUP = NSA_HEADS // NSA_KV_HEADS
SEL_SHIFT = SEL_BLOCK.bit_length() - 1
LANES = 128
VMEM_LIMIT = 56 * 1024 * 1024
LOG2E = math.log2(math.e)
MASK_DIST = 1e32
MASK_POW2 = 2.0 ** 100
SLOPE_PIECES = 3
SEL_LANE0 = 32

F32 = jnp.float32
BF16 = jnp.bfloat16


def _params(*semantics):
    return pltpu.CompilerParams(dimension_semantics=semantics, vmem_limit_bytes=VMEM_LIMIT)


def _dot(a, b):
    return jnp.dot(a, b, preferred_element_type=F32)


def _dot_nt(a, b):
    return lax.dot_general(a, b, (((1,), (1,)), ((), ())), preferred_element_type=F32)


def _dot_tn(a, b):
    return lax.dot_general(a, b, (((0,), (0,)), ((), ())), preferred_element_type=F32)


def _rmsnorm_kernel(x_ref, g_ref, o_ref):
    x = x_ref[...]
    ms = jnp.mean(x * x, axis=-1, keepdims=True)
    o_ref[...] = (x * lax.rsqrt(ms + EPS) * g_ref[...]).astype(o_ref.dtype)


def _rmsnorm(x, gain, tm=256):
    m, d = x.shape
    return pl.pallas_call(
        _rmsnorm_kernel,
        grid=(m // tm,),
        in_specs=[pl.BlockSpec((tm, d), lambda i: (i, 0)),
                  pl.BlockSpec((1, d), lambda i: (0, 0))],
        out_specs=pl.BlockSpec((tm, d), lambda i: (i, 0)),
        out_shape=jax.ShapeDtypeStruct((m, d), BF16),
        compiler_params=_params("parallel"),
        name="rmsnorm",
    )(x, gain.reshape(1, d))


def _post_kernel(y_ref, x_ref, gp_ref, gn_ref, xo_ref, ho_ref):
    y = y_ref[...].astype(F32)
    yn = y * lax.rsqrt(jnp.mean(y * y, axis=-1, keepdims=True) + EPS) * gp_ref[...]
    xn = x_ref[...] + yn
    xo_ref[...] = xn
    ho_ref[...] = (xn * lax.rsqrt(jnp.mean(xn * xn, axis=-1, keepdims=True) + EPS)
                   * gn_ref[...]).astype(ho_ref.dtype)


def _post_last_kernel(y_ref, x_ref, gp_ref, xo_ref):
    y = y_ref[...].astype(F32)
    yn = y * lax.rsqrt(jnp.mean(y * y, axis=-1, keepdims=True) + EPS) * gp_ref[...]
    xo_ref[...] = x_ref[...] + yn


def _post(y, x, g_post, g_next, tm=256):
    m, d = x.shape
    row = pl.BlockSpec((tm, d), lambda i: (i, 0))
    vec = pl.BlockSpec((1, d), lambda i: (0, 0))
    if g_next is None:
        return pl.pallas_call(
            _post_last_kernel, grid=(m // tm,),
            in_specs=[row, row, vec], out_specs=row,
            out_shape=jax.ShapeDtypeStruct((m, d), F32),
            compiler_params=_params("parallel"), name="post_last",
        )(y, x, g_post.reshape(1, d)), None
    return pl.pallas_call(
        _post_kernel, grid=(m // tm,),
        in_specs=[row, row, vec, vec], out_specs=[row, row],
        out_shape=[jax.ShapeDtypeStruct((m, d), F32), jax.ShapeDtypeStruct((m, d), BF16)],
        compiler_params=_params("parallel"), name="post",
    )(y, x, g_post.reshape(1, d), g_next.reshape(1, d))


def _matmul_nt_kernel(a_ref, b_ref, o_ref):
    o_ref[...] = _dot_nt(a_ref[...], b_ref[...]).astype(o_ref.dtype)


def _matmul_nt(a, b_t, out_dtype, tm=1024, tn=1024, name="matmul"):
    m, k = a.shape
    n = b_t.shape[0]
    tm, tn = min(tm, m), min(tn, n)
    return pl.pallas_call(
        _matmul_nt_kernel,
        grid=(m // tm, n // tn),
        in_specs=[pl.BlockSpec((tm, k), lambda i, j: (i, 0)),
                  pl.BlockSpec((tn, k), lambda i, j: (j, 0))],
        out_specs=pl.BlockSpec((tm, tn), lambda i, j: (i, j)),
        out_shape=jax.ShapeDtypeStruct((m, n), out_dtype),
        compiler_params=_params("parallel", "arbitrary"), name=name,
    )(a, b_t)


def _matmul_ws_kernel(a_ref, w_ref, o_ref, wq_ref, *, relu2, transposed):
    @pl.when(pl.program_id(1) == 0)
    def _():
        wq_ref[...] = w_ref[...].astype(BF16)

    acc = _dot_nt(a_ref[...], wq_ref[...]) if transposed else _dot(a_ref[...], wq_ref[...])
    if relu2:
        acc = jnp.square(jnp.maximum(acc, 0.0))
    o_ref[...] = acc.astype(o_ref.dtype)


def _matmul_ws(a, w, layer, col0, n, out_dtype, tm=1024, tn=512, relu2=False, transposed=False,
               name="matmul_ws"):
    m, k = a.shape
    tm, tn = min(tm, m), min(tn, n)
    assert col0 % tn == 0 and n % tn == 0 and m % tm == 0
    j0 = col0 // tn
    if transposed:
        w_spec = pl.BlockSpec((None, tn, k), lambda j, i: (layer, j0 + j, 0))
    else:
        w_spec = pl.BlockSpec((None, k, tn), lambda j, i: (layer, 0, j0 + j))
    return pl.pallas_call(
        functools.partial(_matmul_ws_kernel, relu2=relu2, transposed=transposed),
        grid=(n // tn, m // tm),
        in_specs=[pl.BlockSpec((tm, k), lambda j, i: (i, 0)), w_spec],
        out_specs=pl.BlockSpec((tm, tn), lambda j, i: (i, j)),
        out_shape=jax.ShapeDtypeStruct((m, n), out_dtype),
        scratch_shapes=[pltpu.VMEM((tn, k) if transposed else (k, tn), BF16)],
        compiler_params=_params("parallel", "arbitrary"), name=name,
    )(a, w)


def _matmul_ktiled_kernel(a_ref, w_ref, o_ref, acc_ref):
    kk = pl.program_id(2)

    @pl.when(kk == 0)
    def _():
        acc_ref[...] = jnp.zeros_like(acc_ref)

    acc_ref[...] += _dot(a_ref[...], w_ref[...].astype(BF16))

    @pl.when(kk == pl.num_programs(2) - 1)
    def _():
        o_ref[...] = acc_ref[...].astype(o_ref.dtype)


def _matmul_ktiled(a, w, layer, out_dtype, tm=2048, tn=1024, tk=1024, name="matmul_k"):
    m, k = a.shape
    n = w.shape[2]
    tm, tn, tk = min(tm, m), min(tn, n), min(tk, k)
    return pl.pallas_call(
        _matmul_ktiled_kernel,
        grid=(m // tm, n // tn, k // tk),
        in_specs=[pl.BlockSpec((tm, tk), lambda i, j, kk: (i, kk)),
                  pl.BlockSpec((None, tk, tn), lambda i, j, kk: (layer, kk, j))],
        out_specs=pl.BlockSpec((tm, tn), lambda i, j, kk: (i, j)),
        out_shape=jax.ShapeDtypeStruct((m, n), out_dtype),
        scratch_shapes=[pltpu.VMEM((tm, tn), F32)],
        compiler_params=_params("parallel", "parallel", "arbitrary"), name=name,
    )(a, w)


def _cast_rows_kernel(w_ref, o_ref):
    o_ref[...] = w_ref[0].astype(o_ref.dtype)


def _cast_rows(w, row0, n, tr=512):
    depth, _, k = w.shape
    tr = min(tr, n)
    assert n % tr == 0 and row0 % 16 == 0
    return pl.pallas_call(
        _cast_rows_kernel,
        grid=(depth, n // tr),
        in_specs=[pl.BlockSpec((pl.Element(1), pl.Element(tr), pl.Element(k)),
                               lambda l, j: (l, pl.multiple_of(row0 + j * tr, 16), 0))],
        out_specs=pl.BlockSpec((None, tr, k), lambda l, j: (l, j, 0)),
        out_shape=jax.ShapeDtypeStruct((depth, n, k), BF16),
        compiler_params=_params("parallel", "parallel"), name="cast_rows",
    )(w)


def _merge_kernel(yr_ref, yn_ref, h_ref, wru_ref, wnu_ref, wgr_ref, wgn_ref, o_ref):
    h = h_ref[...]
    a = _dot(yr_ref[...], wru_ref[...])
    b = _dot(yn_ref[...], wnu_ref[...])
    gr = jax.nn.sigmoid(_dot_nt(h, wgr_ref[...]))
    gn = jax.nn.sigmoid(_dot_nt(h, wgn_ref[...]))
    o_ref[...] = (gr * a + gn * b).astype(o_ref.dtype)


def _merge(y_ret, y_nsa, h, w_ru, w_nu, w_g_t, layer, tm=512, tn=512):
    m, d = h.shape
    kr, kn = y_ret.shape[1], y_nsa.shape[1]
    n = w_ru.shape[2]
    tm, tn = min(tm, m), min(tn, n)
    lhs = lambda kk: pl.BlockSpec((tm, kk), lambda i, j: (i, 0))
    rhs = lambda kk: pl.BlockSpec((None, kk, tn), lambda i, j: (layer, 0, j))
    rhs_t = lambda j0: pl.BlockSpec((None, tn, d), lambda i, j: (layer, j0 + j, 0))
    return pl.pallas_call(
        _merge_kernel, grid=(m // tm, n // tn),
        in_specs=[lhs(kr), lhs(kn), lhs(d), rhs(kr), rhs(kn), rhs_t(0), rhs_t(n // tn)],
        out_specs=pl.BlockSpec((tm, tn), lambda i, j: (i, j)),
        out_shape=jax.ShapeDtypeStruct((m, n), BF16),
        compiler_params=_params("parallel", "arbitrary"), name="merge",
    )(y_ret, y_nsa, h, w_ru, w_nu, w_g_t, w_g_t)


def _retention_kernel(q_ref, k_ref, v_ref, g_ref, di_ref, dq_ref, dk_ref, dc_ref, o_ref,
                      state_ref, *, n_chunks, chunk, k_scale):
    state_ref[...] = jnp.zeros_like(state_ref)
    hp = state_ref.shape[0]
    dk = state_ref.shape[1]

    def body(n, carry):
        rows = pl.ds(pl.multiple_of(n * chunk, chunk), chunk)
        for hh in range(hp):
            cols = slice(hh * dk, (hh + 1) * dk)
            q = q_ref[rows, cols]
            kf = k_ref[rows, cols].astype(F32) * k_scale
            v = v_ref[rows, cols]
            s = _dot_nt(q, kf.astype(BF16)) * di_ref[hh]
            state = state_ref[hh]
            o = _dot(s.astype(BF16), v) + _dot(q, state.astype(BF16)) * dq_ref[hh]
            state_ref[hh] = state * dc_ref[hh, 0:1, :] + _dot_tn((kf * dk_ref[hh]).astype(BF16), v)
            mu = jnp.mean(o, axis=-1, keepdims=True)
            oc = o - mu
            var = jnp.mean(oc * oc, axis=-1, keepdims=True)
            gate = g_ref[rows, cols].astype(F32)
            y = gate * jax.nn.sigmoid(gate) * (oc * lax.rsqrt(var + EPS))
            o_ref[rows, cols] = y.astype(o_ref.dtype)
        return carry

    lax.fori_loop(0, n_chunks, body, 0)


def _retention(proj, batch, seq, dk, hp=4):
    heads, chunk = RET_HEADS, RET_CHUNK
    assert heads % hp == 0
    dv = dk
    n_chunks = seq // chunk
    log_gamma = jnp.log1p(-jnp.exp2(-5.0 - jnp.arange(heads, dtype=F32)))
    pos = jnp.arange(chunk, dtype=F32)
    rel = pos[:, None] - pos[None, :]
    d_intra = jnp.where(rel >= 0, jnp.exp(log_gamma[:, None, None] * jnp.maximum(rel, 0.0)), 0.0)
    d_query = jnp.broadcast_to(jnp.exp(log_gamma[:, None] * (pos + 1.0))[..., None], (heads, chunk, dv))
    d_key = jnp.broadcast_to(jnp.exp(log_gamma[:, None] * (chunk - 1.0 - pos))[..., None], (heads, chunk, dk))
    d_chunk = jnp.broadcast_to(jnp.exp(log_gamma * chunk)[:, None, None], (heads, 8, dv))

    col = lambda off: pl.BlockSpec((seq, hp * dk), lambda b, h: (b, off // hp + h))
    tab = lambda r, c: pl.BlockSpec((hp, r, c), lambda b, h: (h, 0, 0))
    return pl.pallas_call(
        functools.partial(_retention_kernel, n_chunks=n_chunks, chunk=chunk, k_scale=dk ** -0.5),
        grid=(batch, heads // hp),
        in_specs=[col(0), col(heads), col(2 * heads), col(3 * heads),
                  tab(chunk, chunk), tab(chunk, dv), tab(chunk, dk), tab(8, dv)],
        out_specs=pl.BlockSpec((seq, hp * dv), lambda b, h: (b, h)),
        out_shape=jax.ShapeDtypeStruct((batch * seq, heads * dv), BF16),
        scratch_shapes=[pltpu.VMEM((hp, dk, dv), F32)],
        compiler_params=_params("parallel", "parallel"), name="retention",
    )(proj, proj, proj, proj, d_intra, d_query, d_key, d_chunk)


def _compress_kernel(x_ref, pos_ref, w1_ref, w2_ref, o_ref, x32_ref, *, n_cmp, n_chunks):
    dh = NSA_DH
    half = CMP_BLOCK // 2
    x32_ref[...] = x_ref[...].astype(F32)
    lo = jnp.zeros((n_chunks, w1_ref.shape[1]), F32)
    hi = jnp.zeros((n_chunks, w1_ref.shape[1]), F32)
    for l in range(half):
        xl = x32_ref[pl.ds(l, n_chunks, stride=CMP_STRIDE), :]
        a = (xl + pos_ref[l:l + 1, :]).astype(BF16)
        b = (xl + pos_ref[half + l:half + l + 1, :]).astype(BF16)
        lo = lo + _dot(a, w1_ref[l * dh:(l + 1) * dh, :])
        hi = hi + _dot(b, w1_ref[(half + l) * dh:(half + l + 1) * dh, :])
    pre = lo + pltpu.roll(hi, n_chunks - 1, axis=0)
    hid = pre * jax.nn.sigmoid(pre)
    out = _dot(hid.astype(BF16), w2_ref[...])
    row = lax.broadcasted_iota(jnp.int32, out.shape, 0)
    o_ref[...] = jnp.where(row < n_cmp, out, 0.0).astype(o_ref.dtype)


def _compress(proj, col_blk, pos, w1, w2, batch, seq):
    kvh, dh = NSA_KV_HEADS, NSA_DH
    n_chunks = seq // CMP_STRIDE
    n_cmp = (seq - CMP_BLOCK) // CMP_STRIDE + 1
    assert CMP_BLOCK == 2 * CMP_STRIDE and n_cmp == n_chunks - 1
    hidden = w1.shape[-1]
    return pl.pallas_call(
        functools.partial(_compress_kernel, n_cmp=n_cmp, n_chunks=n_chunks),
        grid=(2, batch, kvh),
        in_specs=[pl.BlockSpec((seq, dh), lambda c, b, g: (b, col_blk + kvh * c + g)),
                  pl.BlockSpec((None, CMP_BLOCK, dh), lambda c, b, g: (c, 0, 0)),
                  pl.BlockSpec((None, CMP_BLOCK * dh, hidden), lambda c, b, g: (c, 0, 0)),
                  pl.BlockSpec((None, hidden, dh), lambda c, b, g: (c, 0, 0))],
        out_specs=pl.BlockSpec((None, None, None, n_chunks, dh), lambda c, b, g: (c, b, g, 0, 0)),
        out_shape=jax.ShapeDtypeStruct((2, batch, kvh, n_chunks, dh), BF16),
        scratch_shapes=[pltpu.VMEM((seq, dh), F32)],
        compiler_params=_params("parallel", "parallel", "parallel"), name="nsa_compress",
    )(proj, pos, w1, w2)


def _nsa_kernel(q_ref, ks_ref, vs_ref, kw_ref, vw_ref, kc_ref, vc_ref, gl_ref, slope_ref, qx_ref,
                ovt_ref, kxs_ref, kxw_ref, band_ref, o_ref, osel_ref, *, tq, tk, n_cmp, n_sel, k_top):
    dh = NSA_DH
    grp = NSA_GROUP
    sc = (dh ** -0.5) * LOG2E
    span = WINDOW + tq
    t0 = pl.program_id(2) * tq
    q = (q_ref[...].astype(F32) * sc).astype(BF16)
    q4 = jnp.concatenate([q[:, r * dh:(r + 1) * dh] for r in range(grp)], axis=0)
    slope_row = slope_ref[0:1, :]
    tile4 = lambda a: jnp.concatenate([a] * grp, axis=1)

    kc = kc_ref[...]
    vc = vc_ref[...]
    nc = kc.shape[0]
    n_idx = lax.broadcasted_iota(jnp.int32, (nc, tq), 0)
    t_idx = t0 + lax.broadcasted_iota(jnp.int32, (nc, tq), 1)
    dist_c = t_idx - (CMP_STRIDE * n_idx + (CMP_BLOCK - 1))
    dist_c = jnp.where(n_idx < n_cmp, dist_c, -1)
    dist_c = tile4(jnp.where(dist_c >= 0, dist_c.astype(F32), MASK_DIST))
    s = _dot_nt(kc, q4) - slope_row * dist_c
    m = jnp.max(s, axis=0, keepdims=True)
    e = jnp.where(dist_c < MASK_DIST, jnp.exp2(s - m), 0.0)
    l = jnp.sum(e, axis=0, keepdims=True)
    p = e / jnp.where(l > 0.0, l, 1.0)
    o_cmp = _dot_tn(vc, p.astype(BF16))
    p_sum = p[:, 0:tq]
    for r in range(1, grp):
        p_sum = p_sum + p[:, r * tq:(r + 1) * tq]

    p_hi = p_sum.astype(BF16)
    p_lo = (p_sum - p_hi.astype(F32)).astype(BF16)
    ovt = ovt_ref[...]
    imp = _dot(ovt, p_hi) + _dot(ovt, p_lo)
    blk = lax.broadcasted_iota(jnp.int32, (n_sel, tq), 0)
    t_sel = t0 + lax.broadcasted_iota(jnp.int32, (n_sel, tq), 1)
    back = jnp.right_shift(t_sel, SEL_SHIFT) - blk
    forced = jnp.where(blk == 0, 0, jnp.where(back < 0, SEL_LOCAL, back)) < SEL_LOCAL
    imp = jnp.where(forced, FORCED_SCORE, jnp.where(blk * SEL_BLOCK > t_sel, -1.0, imp))
    rank = jnp.zeros((n_sel, tq), F32)
    for j in range(n_sel):
        cand = imp[j:j + 1, :]
        tie = jnp.where(blk > j, 1.0, 0.0)
        rank = rank + jnp.where(cand > imp, 1.0, jnp.where(cand == imp, tie, 0.0))
    unsel = jnp.where(rank < k_top, 0.0, -MASK_POW2)
    unsel = jnp.concatenate([jnp.zeros((SEL_LANE0, tq), F32), unsel,
                             jnp.zeros((LANES - SEL_LANE0 - n_sel, tq), F32)], axis=0).T
    q_aug = jnp.concatenate(
        [q4, jnp.concatenate([(unsel + qx_ref[r:r + 1, :]).astype(BF16) for r in range(grp)], axis=0)],
        axis=1)

    case = jnp.minimum(pl.program_id(2), WINDOW // tq)
    w0 = pl.multiple_of(jnp.maximum(t0 - WINDOW, 0), tq)
    k_aug = jnp.concatenate([kw_ref[pl.ds(w0, span), :], kxw_ref[pl.ds(w0, span), :]], axis=1)
    s = _dot_nt(k_aug, q_aug) + tile4(band_ref[case])
    p = jnp.exp2(s - jnp.max(s, axis=0, keepdims=True))
    o_win = _dot_tn(vw_ref[pl.ds(w0, span), :], p.astype(BF16)) / jnp.sum(p, axis=0, keepdims=True)

    def sel_branch(nk):
        n_chunks = nk // tk
        m = l = acc = None
        for c in range(n_chunks):
            rows = slice(c * tk, (c + 1) * tk)
            k_aug = jnp.concatenate([ks_ref[rows, :], kxs_ref[rows, :]], axis=1)
            s = _dot_nt(k_aug, q_aug)
            if c == n_chunks - 1:
                ahead = (lax.broadcasted_iota(jnp.int32, (tk, tq), 0) + (c * tk - t0)
                         > lax.broadcasted_iota(jnp.int32, (tk, tq), 1))
                s = s + tile4(jnp.where(ahead, NEG_INF, 0.0))
            m_c = jnp.max(s, axis=0, keepdims=True)
            m_new = m_c if c == 0 else jnp.maximum(m, m_c)
            p = jnp.exp2(s - m_new).astype(BF16)
            l_c = jnp.sum(p.astype(F32), axis=0, keepdims=True)
            pv = _dot_tn(vs_ref[rows, :], p)
            if c == 0:
                l, acc = l_c, pv
            else:
                alpha = jnp.exp2(m - m_new)
                l, acc = alpha * l + l_c, alpha * acc + pv
            m = m_new
        return acc / l

    n_kt = (t0 + tq + tk - 1) // tk
    for n in range(1, ks_ref.shape[0] // tk + 1):
        @pl.when(n_kt == n)
        def _():
            osel_ref[...] = sel_branch(n * tk)

    o_sel = osel_ref[...]

    gate = jax.nn.sigmoid(gl_ref[...].T)
    for r in range(grp):
        cols = slice(r * tq, (r + 1) * tq)
        out = (gate[3 * r:3 * r + 1, :] * o_cmp[:, cols]
               + gate[3 * r + 1:3 * r + 2, :] * o_sel[:, cols]
               + gate[3 * r + 2:3 * r + 3, :] * o_win[:, cols])
        o_ref[:, r * dh:(r + 1) * dh] = out.T.astype(o_ref.dtype)


def _nsa(proj, q_blk, kv_blk, cmp_kv, gate_logits, batch, seq, tq=256, tk=512):
    dh, kvh, grp = NSA_DH, NSA_KV_HEADS, NSA_GROUP
    n_cmp = (seq - CMP_BLOCK) // CMP_STRIDE + 1
    nc = cmp_kv.shape[3]
    n_sel = seq // SEL_BLOCK
    k_top = min(SEL_TOPK, n_sel)
    span = WINDOW + tq
    n_case = WINDOW // tq + 1
    assert seq % tk == 0 and seq >= span and WINDOW % tq == 0 and tk % tq == 0 and tk % SEL_BLOCK == 0
    assert SEL_LANE0 + n_sel <= LANES and seq <= 256 * 256 and dh == LANES

    slopes = jnp.exp2(-8.0 * (jnp.arange(NSA_HEADS, dtype=F32) + 1.0) / NSA_HEADS).reshape(kvh, grp)
    slope_tab = jnp.broadcast_to(jnp.repeat(slopes * LOG2E, tq, axis=1)[:, None, :], (kvh, 8, grp * tq))
    pieces, rest = [], slopes * LOG2E
    for _ in range(SLOPE_PIECES):
        piece = rest.astype(BF16).astype(F32)
        pieces += [piece, piece]
        rest = rest - piece
    qx = jnp.zeros((kvh, 8, LANES), F32).at[:, :grp, :2 * SLOPE_PIECES].set(jnp.stack(pieces, axis=-1))
    key = jnp.arange(seq)
    key_cols = jnp.stack([(key >> 8) * 256, key & 255] * SLOPE_PIECES, axis=1).astype(F32)
    kxw = jnp.zeros((seq, LANES), F32).at[:, :2 * SLOPE_PIECES].set(key_cols)
    kxs = kxw.at[key, SEL_LANE0 + key // SEL_BLOCK].set(1.0)
    kxs, kxw = kxs.astype(BF16), kxw.astype(BF16)
    cmp_start = CMP_STRIDE * jnp.arange(n_cmp)
    sel_start = SEL_BLOCK * jnp.arange(n_sel)
    overlap = jnp.clip(jnp.minimum(cmp_start[:, None] + CMP_BLOCK, sel_start[None, :] + SEL_BLOCK)
                       - jnp.maximum(cmp_start[:, None], sel_start[None, :]), 0).astype(F32) / CMP_BLOCK
    ovt = jnp.zeros((n_sel, nc), F32).at[:, :n_cmp].set(overlap.T).astype(BF16)
    dist_w = (jnp.arange(n_case)[:, None, None] * tq + jnp.arange(tq)[None, None, :]
              - jnp.arange(span)[None, :, None])
    band = jnp.where((dist_w >= 0) & (dist_w < WINDOW), 0.0, NEG_INF).astype(F32)

    nq = seq // tq
    kv_spec = lambda c: pl.BlockSpec((seq, dh), lambda b, g, i: (b, kv_blk + kvh * c + g))
    cmp_spec = lambda c: pl.BlockSpec((None, None, None, nc, dh), lambda b, g, i: (c, b, g, 0, 0))
    whole = lambda a: pl.BlockSpec(a.shape, lambda b, g, i: (0,) * a.ndim)
    return pl.pallas_call(
        functools.partial(_nsa_kernel, tq=tq, tk=tk, n_cmp=n_cmp, n_sel=n_sel, k_top=k_top),
        grid=(batch, kvh, nq),
        in_specs=[pl.BlockSpec((tq, grp * dh), lambda b, g, i: (b * nq + i, q_blk + g)),
                  kv_spec(0), kv_spec(1), kv_spec(2), kv_spec(3), cmp_spec(0), cmp_spec(1),
                  pl.BlockSpec((tq, LANES), lambda b, g, i: (b * nq + i, g)),
                  pl.BlockSpec((None, 8, grp * tq), lambda b, g, i: (g, 0, 0)),
                  pl.BlockSpec((None, 8, LANES), lambda b, g, i: (g, 0, 0)),
                  whole(ovt), whole(kxs), whole(kxw), whole(band)],
        out_specs=pl.BlockSpec((tq, grp * dh), lambda b, g, i: (b * nq + i, g)),
        out_shape=jax.ShapeDtypeStruct((batch * seq, kvh * grp * dh), BF16),
        scratch_shapes=[pltpu.VMEM((dh, grp * tq), F32)],
        compiler_params=_params("parallel", "parallel", "arbitrary"), name="nsa_attention",
    )(proj, proj, proj, proj, proj, cmp_kv, cmp_kv, gate_logits, slope_tab, qx, ovt, kxs, kxw, band)


def kernel(x, mix_norm_pre, w_in, cmp_pos_k, cmp_w1_k, cmp_w2_k, cmp_pos_v, cmp_w1_v, cmp_w2_v,
           w_ret_up, w_nsa_up, w_out, mix_norm_post, mlp_norm_pre, w_mlp_in, w_mlp_out,
           mlp_norm_post):
    batch, seq, d = x.shape
    depth = w_in.shape[0]
    ret_w = w_ret_up.shape[1]
    nsa_w = w_nsa_up.shape[1]
    kvh, dh = NSA_KV_HEADS, NSA_DH
    kv_w = kvh * dh
    n_gate = 3 * NSA_HEADS
    c_qn = 4 * ret_w
    c_kv = c_qn + nsa_w
    c_gate = c_kv + 6 * kv_w
    c_gr = c_gate + n_gate
    c_gn = c_gr + d

    w_in_t = jnp.swapaxes(w_in, 1, 2)
    per_grp = n_gate // kvh
    w_gate_t = jnp.pad(w_in_t[:, c_gate:c_gr, :].reshape(depth, kvh, per_grp, d),
                       ((0, 0), (0, 0), (0, LANES - per_grp), (0, 0))).reshape(depth, kvh * LANES, d).astype(BF16)
    w_g_t = _cast_rows(w_in_t, c_gr, 2 * d)
    w_ru = w_ret_up.astype(BF16)
    w_nu = w_nsa_up.astype(BF16)
    cmp_pos = jnp.stack([cmp_pos_k, cmp_pos_v], axis=1)
    cmp_w1 = jnp.stack([cmp_w1_k, cmp_w1_v], axis=1).astype(BF16)
    cmp_w2 = jnp.stack([cmp_w2_k, cmp_w2_v], axis=1).astype(BF16)

    xs = x.reshape(batch * seq, d)
    h = _rmsnorm(xs, mix_norm_pre[0])
    for l in range(depth):
        proj = _matmul_ws(h, w_in_t, l, 0, c_gate, BF16, transposed=True, name="proj_in")
        gate_logits = _matmul_nt(h, w_gate_t[l], F32, name="proj_nsa_gate")
        y_ret = _retention(proj, batch, seq, ret_w // RET_HEADS)
        cmp_kv = _compress(proj, c_kv // dh, cmp_pos[l], cmp_w1[l], cmp_w2[l], batch, seq)
        y_nsa = _nsa(proj, c_qn // (NSA_GROUP * dh), (c_kv + 2 * kv_w) // dh, cmp_kv, gate_logits,
                     batch, seq)
        merged = _merge(y_ret, y_nsa, h, w_ru, w_nu, w_g_t, l)
        mix = _matmul_ws(merged, w_out, l, 0, d, BF16, name="proj_out")
        xs, h = _post(mix, xs, mix_norm_post[l], mlp_norm_pre[l])
        u = _matmul_ws(h, w_mlp_in, l, 0, w_mlp_in.shape[2], BF16, relu2=True, name="mlp_in")
        y = _matmul_ktiled(u, w_mlp_out, l, BF16, name="mlp_out")
        xs, h = _post(y, xs, mlp_norm_post[l], mix_norm_pre[l + 1] if l + 1 < depth else None)
    return xs.reshape(batch, seq, d)
```

```python
import functools
import math

import numpy as np
import jax
import jax.numpy as jnp
from jax import lax
from jax.experimental import pallas as pl
from jax.experimental.pallas import tpu as pltpu

RET_HEADS = 8
RET_CHUNK = 128
NSA_HEADS = 16
NSA_KV_HEADS = 4
NSA_DH = 128
CMP_BLOCK = 32
CMP_STRIDE = 16
SEL_BLOCK = 64
SEL_TOPK = 16
SEL_LOCAL = 2
WINDOW = 512
EPS = 1e-6
NEG_INF = -1e30
FORCED_SCORE = 1e4

NSA_GROUP = NSA_HEADS // NSA_KV_HEADS
SEL_SHIFT = SEL_BLOCK.bit_length() - 1
LANES = 128
VMEM_LIMIT = 56 * 1024 * 1024
LOG2E = math.log2(math.e)
MASK_DIST = 1e32
MASK_POW2 = 2.0 ** 100
SLOPE_PIECES = 3
SEL_LANE0 = 32

F32 = jnp.float32
BF16 = jnp.bfloat16


def _params(*semantics):
    return pltpu.CompilerParams(dimension_semantics=semantics, vmem_limit_bytes=VMEM_LIMIT)


def _dot(a, b):
    return jnp.dot(a, b, preferred_element_type=F32)


def _dot_nt(a, b):
    return lax.dot_general(a, b, (((1,), (1,)), ((), ())), preferred_element_type=F32)


def _dot_tn(a, b):
    return lax.dot_general(a, b, (((0,), (0,)), ((), ())), preferred_element_type=F32)


def _rmsnorm_kernel(x_ref, g_ref, o_ref):
    x = x_ref[...]
    ms = jnp.mean(x * x, axis=-1, keepdims=True)
    o_ref[...] = (x * lax.rsqrt(ms + EPS) * g_ref[...]).astype(o_ref.dtype)


def _rmsnorm(x, gain, tm=256):
    m, d = x.shape
    return pl.pallas_call(
        _rmsnorm_kernel,
        grid=(m // tm,),
        in_specs=[pl.BlockSpec((tm, d), lambda i: (i, 0)),
                  pl.BlockSpec((1, d), lambda i: (0, 0))],
        out_specs=pl.BlockSpec((tm, d), lambda i: (i, 0)),
        out_shape=jax.ShapeDtypeStruct((m, d), BF16),
        compiler_params=_params("parallel"),
        name="rmsnorm",
    )(x, gain.reshape(1, d))


def _post_kernel(y_ref, x_ref, gp_ref, gn_ref, xo_ref, ho_ref):
    y = y_ref[...].astype(F32)
    yn = y * lax.rsqrt(jnp.mean(y * y, axis=-1, keepdims=True) + EPS) * gp_ref[...]
    xn = x_ref[...] + yn
    xo_ref[...] = xn
    ho_ref[...] = (xn * lax.rsqrt(jnp.mean(xn * xn, axis=-1, keepdims=True) + EPS)
                   * gn_ref[...]).astype(ho_ref.dtype)


def _post_last_kernel(y_ref, x_ref, gp_ref, xo_ref):
    y = y_ref[...].astype(F32)
    yn = y * lax.rsqrt(jnp.mean(y * y, axis=-1, keepdims=True) + EPS) * gp_ref[...]
    xo_ref[...] = x_ref[...] + yn


def _post(y, x, g_post, g_next, tm=256):
    m, d = x.shape
    row = pl.BlockSpec((tm, d), lambda i: (i, 0))
    vec = pl.BlockSpec((1, d), lambda i: (0, 0))
    if g_next is None:
        return pl.pallas_call(
            _post_last_kernel, grid=(m // tm,),
            in_specs=[row, row, vec], out_specs=row,
            out_shape=jax.ShapeDtypeStruct((m, d), F32),
            compiler_params=_params("parallel"), name="post_last",
        )(y, x, g_post.reshape(1, d)), None
    return pl.pallas_call(
        _post_kernel, grid=(m // tm,),
        in_specs=[row, row, vec, vec], out_specs=[row, row],
        out_shape=[jax.ShapeDtypeStruct((m, d), F32), jax.ShapeDtypeStruct((m, d), BF16)],
        compiler_params=_params("parallel"), name="post",
    )(y, x, g_post.reshape(1, d), g_next.reshape(1, d))


def _matmul_nt_kernel(a_ref, b_ref, o_ref):
    o_ref[...] = _dot_nt(a_ref[...], b_ref[...]).astype(o_ref.dtype)


def _matmul_nt(a, b_t, out_dtype, tm=1024, tn=1024, name="matmul"):
    m, k = a.shape
    n = b_t.shape[0]
    tm, tn = min(tm, m), min(tn, n)
    return pl.pallas_call(
        _matmul_nt_kernel,
        grid=(m // tm, n // tn),
        in_specs=[pl.BlockSpec((tm, k), lambda i, j: (i, 0)),
                  pl.BlockSpec((tn, k), lambda i, j: (j, 0))],
        out_specs=pl.BlockSpec((tm, tn), lambda i, j: (i, j)),
        out_shape=jax.ShapeDtypeStruct((m, n), out_dtype),
        compiler_params=_params("parallel", "arbitrary"), name=name,
    )(a, b_t)


def _matmul_ws_kernel(a_ref, w_hbm, o_ref, wq_ref, stage_ref, sem, *, relu2, transposed, layer,
                      row0, n_ch):
    j, i = pl.program_id(0), pl.program_id(1)
    ch = stage_ref.shape[1]
    tile = wq_ref.shape[1] if transposed else wq_ref.shape[2]

    def chunk_copy(col, c, slot):
        if transposed:
            src = w_hbm.at[layer, pl.ds(row0 + col * tile + c * ch, ch), :]
        else:
            src = w_hbm.at[layer, pl.ds(c * ch, ch), pl.ds(row0 + col * tile, tile)]
        return pltpu.make_async_copy(src, stage_ref.at[slot], sem.at[slot])

    def convert(buf, c, slot):
        wq_ref[buf, pl.ds(pl.multiple_of(c * ch, ch), ch), :] = stage_ref[slot].astype(BF16)

    @pl.when((j == 0) & (i == 0))
    def _():
        chunk_copy(0, 0, 0).start()
        if n_ch > 1:
            chunk_copy(0, 1, 1).start()
        for c in range(n_ch):
            chunk_copy(0, c, c % 2).wait()
            convert(0, c, c % 2)
            if c + 2 < n_ch:
                chunk_copy(0, c + 2, c % 2).start()

    has_next = j + 1 < pl.num_programs(0)
    nxt = lax.rem(j + 1, 2)

    @pl.when(has_next & (i >= 1) & (i <= n_ch))
    def _():
        slot = lax.rem(i - 1, 2)
        chunk_copy(j + 1, i - 1, slot).wait()
        convert(nxt, i - 1, slot)

    @pl.when(has_next & (i < n_ch))
    def _():
        chunk_copy(j + 1, i, lax.rem(i, 2)).start()

    for buf in range(2):
        @pl.when(lax.rem(j, 2) == buf)
        def _():
            wq = wq_ref[buf]
            acc = _dot_nt(a_ref[...], wq) if transposed else _dot(a_ref[...], wq)
            if relu2:
                acc = jnp.square(jnp.maximum(acc, 0.0))
            o_ref[...] = acc.astype(o_ref.dtype)


def _matmul_ws(a, w, layer, col0, n, out_dtype, tm=1024, tn=1024, relu2=False, transposed=False,
               name="matmul_ws"):
    m, k = a.shape
    tm, tn = min(tm, m), min(tn, n)
    assert n % tn == 0 and m % tm == 0 and col0 % 16 == 0
    n_i = m // tm
    assert n_i >= 2, "the next tile's weights are staged across the row tiles of the current one"
    rows = tn if transposed else k
    n_ch = min(8, n_i - 1)
    while rows % n_ch or (rows // n_ch) % 16:
        n_ch -= 1
    ch = rows // n_ch
    wq_shape = (2, tn, k) if transposed else (2, k, tn)
    return pl.pallas_call(
        functools.partial(_matmul_ws_kernel, relu2=relu2, transposed=transposed, layer=layer,
                          row0=col0, n_ch=n_ch),
        grid=(n // tn, n_i),
        in_specs=[pl.BlockSpec((tm, k), lambda j, i: (i, 0)), pl.BlockSpec(memory_space=pl.ANY)],
        out_specs=pl.BlockSpec((tm, tn), lambda j, i: (i, j)),
        out_shape=jax.ShapeDtypeStruct((m, n), out_dtype),
        scratch_shapes=[pltpu.VMEM(wq_shape, BF16), pltpu.VMEM((2, ch, wq_shape[2]), F32),
                        pltpu.SemaphoreType.DMA((2,))],
        compiler_params=_params("arbitrary", "arbitrary"), name=name,
    )(a, w)


def _matmul_ktiled_kernel(a_ref, w_ref, o_ref, acc_ref):
    kk = pl.program_id(2)

    @pl.when(kk == 0)
    def _():
        acc_ref[...] = jnp.zeros_like(acc_ref)

    acc_ref[...] += _dot(a_ref[...], w_ref[...].astype(BF16))

    @pl.when(kk == pl.num_programs(2) - 1)
    def _():
        o_ref[...] = acc_ref[...].astype(o_ref.dtype)


def _matmul_ktiled(a, w, layer, out_dtype, tm=2048, tn=1024, tk=1024, name="matmul_k"):
    m, k = a.shape
    n = w.shape[2]
    tm, tn, tk = min(tm, m), min(tn, n), min(tk, k)
    return pl.pallas_call(
        _matmul_ktiled_kernel,
        grid=(m // tm, n // tn, k // tk),
        in_specs=[pl.BlockSpec((tm, tk), lambda i, j, kk: (i, kk)),
                  pl.BlockSpec((None, tk, tn), lambda i, j, kk: (layer, kk, j))],
        out_specs=pl.BlockSpec((tm, tn), lambda i, j, kk: (i, j)),
        out_shape=jax.ShapeDtypeStruct((m, n), out_dtype),
        scratch_shapes=[pltpu.VMEM((tm, tn), F32)],
        compiler_params=_params("parallel", "parallel", "arbitrary"), name=name,
    )(a, w)


def _cast_rows_kernel(w_ref, o_ref):
    o_ref[...] = w_ref[0].astype(o_ref.dtype)


def _cast_rows(w, row0, n, tr=512):
    depth, _, k = w.shape
    tr = min(tr, n)
    assert n % tr == 0 and row0 % 16 == 0
    return pl.pallas_call(
        _cast_rows_kernel,
        grid=(depth, n // tr),
        in_specs=[pl.BlockSpec((pl.Element(1), pl.Element(tr), pl.Element(k)),
                               lambda l, j: (l, pl.multiple_of(row0 + j * tr, 16), 0))],
        out_specs=pl.BlockSpec((None, tr, k), lambda l, j: (l, j, 0)),
        out_shape=jax.ShapeDtypeStruct((depth, n, k), BF16),
        compiler_params=_params("parallel", "parallel"), name="cast_rows",
    )(w)


def _merge_kernel(yr_ref, yn_ref, h_ref, wru_ref, wnu_ref, wgr_ref, wgn_ref, o_ref):
    h = h_ref[...]
    a = _dot(yr_ref[...], wru_ref[...])
    b = _dot(yn_ref[...], wnu_ref[...])
    gr = jax.nn.sigmoid(_dot_nt(h, wgr_ref[...]))
    gn = jax.nn.sigmoid(_dot_nt(h, wgn_ref[...]))
    o_ref[...] = (gr * a + gn * b).astype(o_ref.dtype)


def _merge(y_ret, y_nsa, h, w_ru, w_nu, w_g_t, layer, tm=512, tn=512):
    m, d = h.shape
    kr, kn = y_ret.shape[1], y_nsa.shape[1]
    n = w_ru.shape[2]
    tm, tn = min(tm, m), min(tn, n)
    lhs = lambda kk: pl.BlockSpec((tm, kk), lambda i, j: (i, 0))
    rhs = lambda kk: pl.BlockSpec((None, kk, tn), lambda i, j: (layer, 0, j))
    rhs_t = lambda j0: pl.BlockSpec((None, tn, d), lambda i, j: (layer, j0 + j, 0))
    return pl.pallas_call(
        _merge_kernel, grid=(m // tm, n // tn),
        in_specs=[lhs(kr), lhs(kn), lhs(d), rhs(kr), rhs(kn), rhs_t(0), rhs_t(n // tn)],
        out_specs=pl.BlockSpec((tm, tn), lambda i, j: (i, j)),
        out_shape=jax.ShapeDtypeStruct((m, n), BF16),
        compiler_params=_params("parallel", "arbitrary"), name="merge",
    )(y_ret, y_nsa, h, w_ru, w_nu, w_g_t, w_g_t)


def _retention_kernel(q_ref, k_ref, v_ref, g_ref, di_ref, dq_ref, dk_ref, dc_ref, o_ref,
                      state_ref, *, n_chunks, chunk, k_scale):
    state_ref[...] = jnp.zeros_like(state_ref)
    hp = state_ref.shape[0]
    dk = state_ref.shape[1]

    def body(n, carry):
        rows = pl.ds(pl.multiple_of(n * chunk, chunk), chunk)
        for hh in range(hp):
            cols = slice(hh * dk, (hh + 1) * dk)
            q = q_ref[rows, cols]
            kf = k_ref[rows, cols].astype(F32) * k_scale
            v = v_ref[rows, cols]
            s = _dot_nt(q, kf.astype(BF16)) * di_ref[hh]
            state = state_ref[hh]
            o = _dot(s.astype(BF16), v) + _dot(q, state.astype(BF16)) * dq_ref[hh]
            state_ref[hh] = state * dc_ref[hh, 0:1, :] + _dot_tn((kf * dk_ref[hh]).astype(BF16), v)
            mu = jnp.mean(o, axis=-1, keepdims=True)
            oc = o - mu
            var = jnp.mean(oc * oc, axis=-1, keepdims=True)
            gate = g_ref[rows, cols].astype(F32)
            y = gate * jax.nn.sigmoid(gate) * (oc * lax.rsqrt(var + EPS))
            o_ref[rows, cols] = y.astype(o_ref.dtype)
        return carry

    lax.fori_loop(0, n_chunks, body, 0)


def _retention(proj, batch, seq, dk, hp=4):
    heads, chunk = RET_HEADS, RET_CHUNK
    assert heads % hp == 0
    dv = dk
    n_chunks = seq // chunk
    log_gamma = jnp.log1p(-jnp.exp2(-5.0 - jnp.arange(heads, dtype=F32)))
    pos = jnp.arange(chunk, dtype=F32)
    rel = pos[:, None] - pos[None, :]
    d_intra = jnp.where(rel >= 0, jnp.exp(log_gamma[:, None, None] * jnp.maximum(rel, 0.0)), 0.0)
    d_query = jnp.broadcast_to(jnp.exp(log_gamma[:, None] * (pos + 1.0))[..., None], (heads, chunk, dv))
    d_key = jnp.broadcast_to(jnp.exp(log_gamma[:, None] * (chunk - 1.0 - pos))[..., None], (heads, chunk, dk))
    d_chunk = jnp.broadcast_to(jnp.exp(log_gamma * chunk)[:, None, None], (heads, 8, dv))

    col = lambda off: pl.BlockSpec((seq, hp * dk), lambda b, h: (b, off // hp + h))
    tab = lambda r, c: pl.BlockSpec((hp, r, c), lambda b, h: (h, 0, 0))
    return pl.pallas_call(
        functools.partial(_retention_kernel, n_chunks=n_chunks, chunk=chunk, k_scale=dk ** -0.5),
        grid=(batch, heads // hp),
        in_specs=[col(0), col(heads), col(2 * heads), col(3 * heads),
                  tab(chunk, chunk), tab(chunk, dv), tab(chunk, dk), tab(8, dv)],
        out_specs=pl.BlockSpec((seq, hp * dv), lambda b, h: (b, h)),
        out_shape=jax.ShapeDtypeStruct((batch * seq, heads * dv), BF16),
        scratch_shapes=[pltpu.VMEM((hp, dk, dv), F32)],
        compiler_params=_params("parallel", "parallel"), name="retention",
    )(proj, proj, proj, proj, d_intra, d_query, d_key, d_chunk)


def _compress_kernel(x_ref, pos_ref, w1_ref, w2_ref, o_ref, x32_ref, *, n_cmp, n_chunks):
    dh = NSA_DH
    half = CMP_BLOCK // 2
    x32_ref[...] = x_ref[...].astype(F32)
    lo = jnp.zeros((n_chunks, w1_ref.shape[1]), F32)
    hi = jnp.zeros((n_chunks, w1_ref.shape[1]), F32)
    for l in range(half):
        xl = x32_ref[pl.ds(l, n_chunks, stride=CMP_STRIDE), :]
        a = (xl + pos_ref[l:l + 1, :]).astype(BF16)
        b = (xl + pos_ref[half + l:half + l + 1, :]).astype(BF16)
        lo = lo + _dot(a, w1_ref[l * dh:(l + 1) * dh, :])
        hi = hi + _dot(b, w1_ref[(half + l) * dh:(half + l + 1) * dh, :])
    pre = lo + pltpu.roll(hi, n_chunks - 1, axis=0)
    hid = pre * jax.nn.sigmoid(pre)
    out = _dot(hid.astype(BF16), w2_ref[...])
    row = lax.broadcasted_iota(jnp.int32, out.shape, 0)
    o_ref[...] = jnp.where(row < n_cmp, out, 0.0).astype(o_ref.dtype)


def _compress(proj, col_blk, pos, w1, w2, batch, seq):
    kvh, dh = NSA_KV_HEADS, NSA_DH
    n_chunks = seq // CMP_STRIDE
    n_cmp = (seq - CMP_BLOCK) // CMP_STRIDE + 1
    assert CMP_BLOCK == 2 * CMP_STRIDE and n_cmp == n_chunks - 1
    hidden = w1.shape[-1]
    return pl.pallas_call(
        functools.partial(_compress_kernel, n_cmp=n_cmp, n_chunks=n_chunks),
        grid=(2, batch, kvh),
        in_specs=[pl.BlockSpec((seq, dh), lambda c, b, g: (b, col_blk + kvh * c + g)),
                  pl.BlockSpec((None, CMP_BLOCK, dh), lambda c, b, g: (c, 0, 0)),
                  pl.BlockSpec((None, CMP_BLOCK * dh, hidden), lambda c, b, g: (c, 0, 0)),
                  pl.BlockSpec((None, hidden, dh), lambda c, b, g: (c, 0, 0))],
        out_specs=pl.BlockSpec((None, None, None, n_chunks, dh), lambda c, b, g: (c, b, g, 0, 0)),
        out_shape=jax.ShapeDtypeStruct((2, batch, kvh, n_chunks, dh), BF16),
        scratch_shapes=[pltpu.VMEM((seq, dh), F32)],
        compiler_params=_params("parallel", "parallel", "parallel"), name="nsa_compress",
    )(proj, pos, w1, w2)


def _nsa_kernel(q_ref, ks_ref, vs_ref, kw_ref, vw_ref, kc_ref, vc_ref, gl_ref, slope_ref, qx_ref,
                ovt_ref, kxs_ref, kxw_ref, band_ref, o_ref, osel_ref, *, tq, tk, n_cmp, n_sel, k_top):
    dh = NSA_DH
    grp = NSA_GROUP
    sc = (dh ** -0.5) * LOG2E
    span = WINDOW + tq
    t0 = pl.program_id(2) * tq
    q = (q_ref[...].astype(F32) * sc).astype(BF16)
    q4 = jnp.concatenate([q[:, r * dh:(r + 1) * dh] for r in range(grp)], axis=0)
    slope_row = slope_ref[0:1, :]
    tile4 = lambda a: jnp.concatenate([a] * grp, axis=1)

    kc = kc_ref[...]
    vc = vc_ref[...]
    nc = kc.shape[0]
    n_idx = lax.broadcasted_iota(jnp.int32, (nc, tq), 0)
    t_idx = t0 + lax.broadcasted_iota(jnp.int32, (nc, tq), 1)
    dist_c = t_idx - (CMP_STRIDE * n_idx + (CMP_BLOCK - 1))
    dist_c = jnp.where(n_idx < n_cmp, dist_c, -1)
    dist_c = tile4(jnp.where(dist_c >= 0, dist_c.astype(F32), MASK_DIST))
    s = _dot_nt(kc, q4) - slope_row * dist_c
    m = jnp.max(s, axis=0, keepdims=True)
    e = jnp.where(dist_c < MASK_DIST, jnp.exp2(s - m), 0.0)
    l = jnp.sum(e, axis=0, keepdims=True)
    p = e / jnp.where(l > 0.0, l, 1.0)
    o_cmp = _dot_tn(vc, p.astype(BF16))
    p_sum = p[:, 0:tq]
    for r in range(1, grp):
        p_sum = p_sum + p[:, r * tq:(r + 1) * tq]

    p_hi = p_sum.astype(BF16)
    p_lo = (p_sum - p_hi.astype(F32)).astype(BF16)
    ovt = ovt_ref[...]
    imp = _dot(ovt, p_hi) + _dot(ovt, p_lo)
    blk = lax.broadcasted_iota(jnp.int32, (n_sel, tq), 0)
    t_sel = t0 + lax.broadcasted_iota(jnp.int32, (n_sel, tq), 1)
    back = jnp.right_shift(t_sel, SEL_SHIFT) - blk
    forced = jnp.where(blk == 0, 0, jnp.where(back < 0, SEL_LOCAL, back)) < SEL_LOCAL
    imp = jnp.where(forced, FORCED_SCORE, jnp.where(blk * SEL_BLOCK > t_sel, -1.0, imp))
    rank = jnp.zeros((n_sel, tq), F32)
    for j in range(n_sel):
        cand = imp[j:j + 1, :]
        tie = jnp.where(blk > j, 1.0, 0.0)
        rank = rank + jnp.where(cand > imp, 1.0, jnp.where(cand == imp, tie, 0.0))
    unsel = jnp.where(rank < k_top, 0.0, -MASK_POW2)
    unsel = jnp.concatenate([jnp.zeros((SEL_LANE0, tq), F32), unsel,
                             jnp.zeros((LANES - SEL_LANE0 - n_sel, tq), F32)], axis=0).T
    q_aug = jnp.concatenate(
        [q4, jnp.concatenate([(unsel + qx_ref[r:r + 1, :]).astype(BF16) for r in range(grp)], axis=0)],
        axis=1)

    case = jnp.minimum(pl.program_id(2), WINDOW // tq)
    w0 = pl.multiple_of(jnp.maximum(t0 - WINDOW, 0), tq)
    k_aug = jnp.concatenate([kw_ref[pl.ds(w0, span), :], kxw_ref[pl.ds(w0, span), :]], axis=1)
    s = _dot_nt(k_aug, q_aug) + tile4(band_ref[case])
    p = jnp.exp2(s - jnp.max(s, axis=0, keepdims=True))
    o_win = _dot_tn(vw_ref[pl.ds(w0, span), :], p.astype(BF16)) / jnp.sum(p, axis=0, keepdims=True)

    def sel_branch(nk):
        n_chunks = nk // tk
        m = l = acc = None
        for c in range(n_chunks):
            rows = slice(c * tk, (c + 1) * tk)
            k_aug = jnp.concatenate([ks_ref[rows, :], kxs_ref[rows, :]], axis=1)
            s = _dot_nt(k_aug, q_aug)
            if c == n_chunks - 1:
                ahead = (lax.broadcasted_iota(jnp.int32, (tk, tq), 0) + (c * tk - t0)
                         > lax.broadcasted_iota(jnp.int32, (tk, tq), 1))
                s = s + tile4(jnp.where(ahead, NEG_INF, 0.0))
            m_c = jnp.max(s, axis=0, keepdims=True)
            m_new = m_c if c == 0 else jnp.maximum(m, m_c)
            p = jnp.exp2(s - m_new).astype(BF16)
            l_c = jnp.sum(p.astype(F32), axis=0, keepdims=True)
            pv = _dot_tn(vs_ref[rows, :], p)
            if c == 0:
                l, acc = l_c, pv
            else:
                alpha = jnp.exp2(m - m_new)
                l, acc = alpha * l + l_c, alpha * acc + pv
            m = m_new
        return acc / l

    n_kt = (t0 + tq + tk - 1) // tk
    for n in range(1, ks_ref.shape[0] // tk + 1):
        @pl.when(n_kt == n)
        def _():
            osel_ref[...] = sel_branch(n * tk)

    o_sel = osel_ref[...]

    gate = jax.nn.sigmoid(gl_ref[...].T)
    for r in range(grp):
        cols = slice(r * tq, (r + 1) * tq)
        out = (gate[3 * r:3 * r + 1, :] * o_cmp[:, cols]
               + gate[3 * r + 1:3 * r + 2, :] * o_sel[:, cols]
               + gate[3 * r + 2:3 * r + 3, :] * o_win[:, cols])
        o_ref[:, r * dh:(r + 1) * dh] = out.T.astype(o_ref.dtype)


def _nsa(proj, q_blk, kv_blk, cmp_kv, gate_logits, batch, seq, tq=256, tk=512):
    dh, kvh, grp = NSA_DH, NSA_KV_HEADS, NSA_GROUP
    n_cmp = (seq - CMP_BLOCK) // CMP_STRIDE + 1
    nc = cmp_kv.shape[3]
    n_sel = seq // SEL_BLOCK
    k_top = min(SEL_TOPK, n_sel)
    span = WINDOW + tq
    n_case = WINDOW // tq + 1
    assert seq % tk == 0 and seq >= span and WINDOW % tq == 0 and tk % tq == 0 and tk % SEL_BLOCK == 0
    assert SEL_LANE0 + n_sel <= LANES and seq <= 256 * 256 and dh == LANES

    slopes = jnp.exp2(-8.0 * (jnp.arange(NSA_HEADS, dtype=F32) + 1.0) / NSA_HEADS).reshape(kvh, grp)
    slope_tab = jnp.broadcast_to(jnp.repeat(slopes * LOG2E, tq, axis=1)[:, None, :], (kvh, 8, grp * tq))
    pieces, rest = [], slopes * LOG2E
    for _ in range(SLOPE_PIECES):
        piece = rest.astype(BF16).astype(F32)
        pieces += [piece, piece]
        rest = rest - piece
    qx = jnp.zeros((kvh, 8, LANES), F32).at[:, :grp, :2 * SLOPE_PIECES].set(jnp.stack(pieces, axis=-1))
    key = jnp.arange(seq)
    key_cols = jnp.stack([(key >> 8) * 256, key & 255] * SLOPE_PIECES, axis=1).astype(F32)
    kxw = jnp.zeros((seq, LANES), F32).at[:, :2 * SLOPE_PIECES].set(key_cols)
    kxs = kxw.at[key, SEL_LANE0 + key // SEL_BLOCK].set(1.0)
    kxs, kxw = kxs.astype(BF16), kxw.astype(BF16)
    cmp_start = CMP_STRIDE * jnp.arange(n_cmp)
    sel_start = SEL_BLOCK * jnp.arange(n_sel)
    overlap = jnp.clip(jnp.minimum(cmp_start[:, None] + CMP_BLOCK, sel_start[None, :] + SEL_BLOCK)
                       - jnp.maximum(cmp_start[:, None], sel_start[None, :]), 0).astype(F32) / CMP_BLOCK
    ovt = jnp.zeros((n_sel, nc), F32).at[:, :n_cmp].set(overlap.T).astype(BF16)
    dist_w = (jnp.arange(n_case)[:, None, None] * tq + jnp.arange(tq)[None, None, :]
              - jnp.arange(span)[None, :, None])
    band = jnp.where((dist_w >= 0) & (dist_w < WINDOW), 0.0, NEG_INF).astype(F32)

    nq = seq // tq
    kv_spec = lambda c: pl.BlockSpec((seq, dh), lambda b, g, i: (b, kv_blk + kvh * c + g))
    cmp_spec = lambda c: pl.BlockSpec((None, None, None, nc, dh), lambda b, g, i: (c, b, g, 0, 0))
    whole = lambda a: pl.BlockSpec(a.shape, lambda b, g, i: (0,) * a.ndim)
    return pl.pallas_call(
        functools.partial(_nsa_kernel, tq=tq, tk=tk, n_cmp=n_cmp, n_sel=n_sel, k_top=k_top),
        grid=(batch, kvh, nq),
        in_specs=[pl.BlockSpec((tq, grp * dh), lambda b, g, i: (b * nq + i, q_blk + g)),
                  kv_spec(0), kv_spec(1), kv_spec(2), kv_spec(3), cmp_spec(0), cmp_spec(1),
                  pl.BlockSpec((tq, LANES), lambda b, g, i: (b * nq + i, g)),
                  pl.BlockSpec((None, 8, grp * tq), lambda b, g, i: (g, 0, 0)),
                  pl.BlockSpec((None, 8, LANES), lambda b, g, i: (g, 0, 0)),
                  whole(ovt), whole(kxs), whole(kxw), whole(band)],
        out_specs=pl.BlockSpec((tq, grp * dh), lambda b, g, i: (b * nq + i, g)),
        out_shape=jax.ShapeDtypeStruct((batch * seq, kvh * grp * dh), BF16),
        scratch_shapes=[pltpu.VMEM((dh, grp * tq), F32)],
        compiler_params=_params("parallel", "parallel", "arbitrary"), name="nsa_attention",
    )(proj, proj, proj, proj, proj, cmp_kv, cmp_kv, gate_logits, slope_tab, qx, ovt, kxs, kxw, band)


def kernel(x, mix_norm_pre, w_in, cmp_pos_k, cmp_w1_k, cmp_w2_k, cmp_pos_v, cmp_w1_v, cmp_w2_v,
           w_ret_up, w_nsa_up, w_out, mix_norm_post, mlp_norm_pre, w_mlp_in, w_mlp_out,
           mlp_norm_post):
    batch, seq, d = x.shape
    depth = w_in.shape[0]
    ret_w = w_ret_up.shape[1]
    nsa_w = w_nsa_up.shape[1]
    kvh, dh = NSA_KV_HEADS, NSA_DH
    kv_w = kvh * dh
    n_gate = 3 * NSA_HEADS
    c_qn = 4 * ret_w
    c_kv = c_qn + nsa_w
    c_gate = c_kv + 6 * kv_w
    c_gr = c_gate + n_gate
    c_gn = c_gr + d

    w_in_t = jnp.swapaxes(w_in, 1, 2)
    per_grp = n_gate // kvh
    w_gate_t = jnp.pad(w_in_t[:, c_gate:c_gr, :].reshape(depth, kvh, per_grp, d),
                       ((0, 0), (0, 0), (0, LANES - per_grp), (0, 0))).reshape(depth, kvh * LANES, d).astype(BF16)
    w_g_t = _cast_rows(w_in_t, c_gr, 2 * d)
    w_ru = w_ret_up.astype(BF16)
    w_nu = w_nsa_up.astype(BF16)
    cmp_pos = jnp.stack([cmp_pos_k, cmp_pos_v], axis=1)
    cmp_w1 = jnp.stack([cmp_w1_k, cmp_w1_v], axis=1).astype(BF16)
    cmp_w2 = jnp.stack([cmp_w2_k, cmp_w2_v], axis=1).astype(BF16)

    xs = x.reshape(batch * seq, d)
    h = _rmsnorm(xs, mix_norm_pre[0])
    for l in range(depth):
        proj = _matmul_ws(h, w_in_t, l, 0, c_gate, BF16, transposed=True, name="proj_in")
        gate_logits = _matmul_nt(h, w_gate_t[l], F32, name="proj_nsa_gate")
        y_ret = _retention(proj, batch, seq, ret_w // RET_HEADS)
        cmp_kv = _compress(proj, c_kv // dh, cmp_pos[l], cmp_w1[l], cmp_w2[l], batch, seq)
        y_nsa = _nsa(proj, c_qn // (NSA_GROUP * dh), (c_kv + 2 * kv_w) // dh, cmp_kv, gate_logits,
                     batch, seq)
        merged = _merge(y_ret, y_nsa, h, w_ru, w_nu, w_g_t, l)
        mix = _matmul_ws(merged, w_out, l, 0, d, BF16, name="proj_out")
        xs, h = _post(mix, xs, mix_norm_post[l], mlp_norm_pre[l])
        u = _matmul_ws(h, w_mlp_in, l, 0, w_mlp_in.shape[2], BF16, relu2=True, name="mlp_in")
        y = _matmul_ktiled(u, w_mlp_out, l, BF16, name="mlp_out")
        xs, h = _post(y, xs, mlp_norm_post[l], mix_norm_pre[l + 1] if l + 1 < depth else None)
    return xs.reshape(batch, seq, d)
```

```python
import functools
import math

import numpy as np
import jax
import jax.numpy as jnp
from jax import lax
from jax.experimental import pallas as pl
from jax.experimental.pallas import tpu as pltpu

RET_HEADS = 8
RET_CHUNK = 128
NSA_HEADS = 16
NSA_KV_HEADS = 4
NSA_DH = 128
CMP_BLOCK = 32
CMP_STRIDE = 16
SEL_BLOCK = 64
SEL_TOPK = 16
SEL_LOCAL = 2
WINDOW = 512
EPS = 1e-6
NEG_INF = -1e30
FORCED_SCORE = 1e4

NSA_GROUP = NSA_HEADS // NSA_KV_HEADS
SEL_SHIFT = SEL_BLOCK.bit_length() - 1
LANES = 128
VMEM_LIMIT = 56 * 1024 * 1024
LOG2E = math.log2(math.e)
MASK_DIST = 1e32
MASK_POW2 = 2.0 ** 100
SLOPE_PIECES = 3
SEL_LANE0 = 32

F32 = jnp.float32
BF16 = jnp.bfloat16


def _params(*semantics):
    return pltpu.CompilerParams(dimension_semantics=semantics, vmem_limit_bytes=VMEM_LIMIT)


def _dot(a, b):
    return jnp.dot(a, b, preferred_element_type=F32)


def _dot_nt(a, b):
    return lax.dot_general(a, b, (((1,), (1,)), ((), ())), preferred_element_type=F32)


def _dot_tn(a, b):
    return lax.dot_general(a, b, (((0,), (0,)), ((), ())), preferred_element_type=F32)


def _rmsnorm_kernel(x_ref, g_ref, o_ref):
    x = x_ref[...]
    ms = jnp.mean(x * x, axis=-1, keepdims=True)
    o_ref[...] = (x * lax.rsqrt(ms + EPS) * g_ref[...]).astype(o_ref.dtype)


def _rmsnorm(x, gain, tm=256):
    m, d = x.shape
    return pl.pallas_call(
        _rmsnorm_kernel,
        grid=(m // tm,),
        in_specs=[pl.BlockSpec((tm, d), lambda i: (i, 0)),
                  pl.BlockSpec((1, d), lambda i: (0, 0))],
        out_specs=pl.BlockSpec((tm, d), lambda i: (i, 0)),
        out_shape=jax.ShapeDtypeStruct((m, d), BF16),
        compiler_params=_params("parallel"),
        name="rmsnorm",
    )(x, gain.reshape(1, d))


def _post_kernel(y_ref, x_ref, gp_ref, gn_ref, xo_ref, ho_ref):
    y = y_ref[...].astype(F32)
    yn = y * lax.rsqrt(jnp.mean(y * y, axis=-1, keepdims=True) + EPS) * gp_ref[...]
    xn = x_ref[...] + yn
    xo_ref[...] = xn
    ho_ref[...] = (xn * lax.rsqrt(jnp.mean(xn * xn, axis=-1, keepdims=True) + EPS)
                   * gn_ref[...]).astype(ho_ref.dtype)


def _post_last_kernel(y_ref, x_ref, gp_ref, xo_ref):
    y = y_ref[...].astype(F32)
    yn = y * lax.rsqrt(jnp.mean(y * y, axis=-1, keepdims=True) + EPS) * gp_ref[...]
    xo_ref[...] = x_ref[...] + yn


def _post(y, x, g_post, g_next, tm=256):
    m, d = x.shape
    row = pl.BlockSpec((tm, d), lambda i: (i, 0))
    vec = pl.BlockSpec((1, d), lambda i: (0, 0))
    if g_next is None:
        return pl.pallas_call(
            _post_last_kernel, grid=(m // tm,),
            in_specs=[row, row, vec], out_specs=row,
            out_shape=jax.ShapeDtypeStruct((m, d), F32),
            compiler_params=_params("parallel"), name="post_last",
        )(y, x, g_post.reshape(1, d)), None
    return pl.pallas_call(
        _post_kernel, grid=(m // tm,),
        in_specs=[row, row, vec, vec], out_specs=[row, row],
        out_shape=[jax.ShapeDtypeStruct((m, d), F32), jax.ShapeDtypeStruct((m, d), BF16)],
        compiler_params=_params("parallel"), name="post",
    )(y, x, g_post.reshape(1, d), g_next.reshape(1, d))


def _matmul_nt_kernel(a_ref, b_ref, o_ref):
    o_ref[...] = _dot_nt(a_ref[...], b_ref[...]).astype(o_ref.dtype)


def _matmul_nt(a, b_t, out_dtype, tm=1024, tn=1024, name="matmul"):
    m, k = a.shape
    n = b_t.shape[0]
    tm, tn = min(tm, m), min(tn, n)
    return pl.pallas_call(
        _matmul_nt_kernel,
        grid=(m // tm, n // tn),
        in_specs=[pl.BlockSpec((tm, k), lambda i, j: (i, 0)),
                  pl.BlockSpec((tn, k), lambda i, j: (j, 0))],
        out_specs=pl.BlockSpec((tm, tn), lambda i, j: (i, j)),
        out_shape=jax.ShapeDtypeStruct((m, n), out_dtype),
        compiler_params=_params("parallel", "arbitrary"), name=name,
    )(a, b_t)


def _matmul_ws_kernel(a_ref, w_hbm, o_ref, wq_ref, stage_ref, sem, *, relu2, transposed, layer,
                      row0, n_ch):
    j, i = pl.program_id(0), pl.program_id(1)
    ch = stage_ref.shape[1]
    tile = wq_ref.shape[1] if transposed else wq_ref.shape[2]

    def chunk_copy(col, c, slot):
        if transposed:
            src = w_hbm.at[layer, pl.ds(row0 + col * tile + c * ch, ch), :]
        else:
            src = w_hbm.at[layer, pl.ds(c * ch, ch), pl.ds(row0 + col * tile, tile)]
        return pltpu.make_async_copy(src, stage_ref.at[slot], sem.at[slot])

    def convert(buf, c, slot):
        wq_ref[buf, pl.ds(pl.multiple_of(c * ch, ch), ch), :] = stage_ref[slot].astype(BF16)

    @pl.when((j == 0) & (i == 0))
    def _():
        chunk_copy(0, 0, 0).start()
        if n_ch > 1:
            chunk_copy(0, 1, 1).start()
        for c in range(n_ch):
            chunk_copy(0, c, c % 2).wait()
            convert(0, c, c % 2)
            if c + 2 < n_ch:
                chunk_copy(0, c + 2, c % 2).start()

    has_next = j + 1 < pl.num_programs(0)
    nxt = lax.rem(j + 1, 2)

    @pl.when(has_next & (i >= 1) & (i <= n_ch))
    def _():
        slot = lax.rem(i - 1, 2)
        chunk_copy(j + 1, i - 1, slot).wait()
        convert(nxt, i - 1, slot)

    @pl.when(has_next & (i < n_ch))
    def _():
        chunk_copy(j + 1, i, lax.rem(i, 2)).start()

    for buf in range(2):
        @pl.when(lax.rem(j, 2) == buf)
        def _():
            wq = wq_ref[buf]
            acc = _dot_nt(a_ref[...], wq) if transposed else _dot(a_ref[...], wq)
            if relu2:
                acc = jnp.square(jnp.maximum(acc, 0.0))
            o_ref[...] = acc.astype(o_ref.dtype)


def _matmul_ws(a, w, layer, col0, n, out_dtype, tm=1024, tn=1024, relu2=False, transposed=False,
               name="matmul_ws"):
    m, k = a.shape
    tm, tn = min(tm, m), min(tn, n)
    assert n % tn == 0 and m % tm == 0 and col0 % 16 == 0
    n_i = m // tm
    assert n_i >= 2, "the next tile's weights are staged across the row tiles of the current one"
    rows = tn if transposed else k
    n_ch = min(8, n_i - 1)
    while rows % n_ch or (rows // n_ch) % 16:
        n_ch -= 1
    ch = rows // n_ch
    wq_shape = (2, tn, k) if transposed else (2, k, tn)
    return pl.pallas_call(
        functools.partial(_matmul_ws_kernel, relu2=relu2, transposed=transposed, layer=layer,
                          row0=col0, n_ch=n_ch),
        grid=(n // tn, n_i),
        in_specs=[pl.BlockSpec((tm, k), lambda j, i: (i, 0)), pl.BlockSpec(memory_space=pl.ANY)],
        out_specs=pl.BlockSpec((tm, tn), lambda j, i: (i, j)),
        out_shape=jax.ShapeDtypeStruct((m, n), out_dtype),
        scratch_shapes=[pltpu.VMEM(wq_shape, BF16), pltpu.VMEM((2, ch, wq_shape[2]), F32),
                        pltpu.SemaphoreType.DMA((2,))],
        compiler_params=_params("arbitrary", "arbitrary"), name=name,
    )(a, w)


def _matmul_ktiled_kernel(a_ref, w_ref, o_ref, acc_ref):
    kk = pl.program_id(2)

    @pl.when(kk == 0)
    def _():
        acc_ref[...] = jnp.zeros_like(acc_ref)

    acc_ref[...] += _dot(a_ref[...], w_ref[...].astype(BF16))

    @pl.when(kk == pl.num_programs(2) - 1)
    def _():
        o_ref[...] = acc_ref[...].astype(o_ref.dtype)


def _matmul_ktiled(a, w, layer, out_dtype, tm=2048, tn=1024, tk=1024, name="matmul_k"):
    m, k = a.shape
    n = w.shape[2]
    tm, tn, tk = min(tm, m), min(tn, n), min(tk, k)
    return pl.pallas_call(
        _matmul_ktiled_kernel,
        grid=(m // tm, n // tn, k // tk),
        in_specs=[pl.BlockSpec((tm, tk), lambda i, j, kk: (i, kk)),
                  pl.BlockSpec((None, tk, tn), lambda i, j, kk: (layer, kk, j))],
        out_specs=pl.BlockSpec((tm, tn), lambda i, j, kk: (i, j)),
        out_shape=jax.ShapeDtypeStruct((m, n), out_dtype),
        scratch_shapes=[pltpu.VMEM((tm, tn), F32)],
        compiler_params=_params("parallel", "parallel", "arbitrary"), name=name,
    )(a, w)


def _cast_rows_kernel(w_ref, o_ref):
    o_ref[...] = w_ref[0].astype(o_ref.dtype)


def _cast_rows(w, row0, n, tr=512):
    depth, _, k = w.shape
    tr = min(tr, n)
    assert n % tr == 0 and row0 % 16 == 0
    return pl.pallas_call(
        _cast_rows_kernel,
        grid=(depth, n // tr),
        in_specs=[pl.BlockSpec((pl.Element(1), pl.Element(tr), pl.Element(k)),
                               lambda l, j: (l, pl.multiple_of(row0 + j * tr, 16), 0))],
        out_specs=pl.BlockSpec((None, tr, k), lambda l, j: (l, j, 0)),
        out_shape=jax.ShapeDtypeStruct((depth, n, k), BF16),
        compiler_params=_params("parallel", "parallel"), name="cast_rows",
    )(w)


def _merge_kernel(yr_ref, yn_ref, h_ref, wru_ref, wnu_ref, wgr_ref, wgn_ref, o_ref):
    h = h_ref[...]
    a = _dot(yr_ref[...], wru_ref[...])
    b = _dot(yn_ref[...], wnu_ref[...])
    gr = jax.nn.sigmoid(_dot_nt(h, wgr_ref[...]))
    gn = jax.nn.sigmoid(_dot_nt(h, wgn_ref[...]))
    o_ref[...] = (gr * a + gn * b).astype(o_ref.dtype)


def _merge(y_ret, y_nsa, h, w_ru, w_nu, w_g_t, layer, tm=512, tn=512):
    m, d = h.shape
    kr, kn = y_ret.shape[1], y_nsa.shape[1]
    n = w_ru.shape[2]
    tm, tn = min(tm, m), min(tn, n)
    lhs = lambda kk: pl.BlockSpec((tm, kk), lambda i, j: (i, 0))
    rhs = lambda kk: pl.BlockSpec((None, kk, tn), lambda i, j: (layer, 0, j))
    rhs_t = lambda j0: pl.BlockSpec((None, tn, d), lambda i, j: (layer, j0 + j, 0))
    return pl.pallas_call(
        _merge_kernel, grid=(m // tm, n // tn),
        in_specs=[lhs(kr), lhs(kn), lhs(d), rhs(kr), rhs(kn), rhs_t(0), rhs_t(n // tn)],
        out_specs=pl.BlockSpec((tm, tn), lambda i, j: (i, j)),
        out_shape=jax.ShapeDtypeStruct((m, n), BF16),
        compiler_params=_params("parallel", "arbitrary"), name="merge",
    )(y_ret, y_nsa, h, w_ru, w_nu, w_g_t, w_g_t)


def _retention_kernel(q_ref, k_ref, v_ref, g_ref, di_ref, dq_ref, dk_ref, dc_ref, o_ref,
                      state_ref, *, n_chunks, chunk, k_scale):
    state_ref[...] = jnp.zeros_like(state_ref)
    hp = state_ref.shape[0]
    dk = state_ref.shape[1]

    def body(n, carry):
        rows = pl.ds(pl.multiple_of(n * chunk, chunk), chunk)
        for hh in range(hp):
            cols = slice(hh * dk, (hh + 1) * dk)
            q = q_ref[rows, cols]
            kf = k_ref[rows, cols].astype(F32) * k_scale
            v = v_ref[rows, cols]
            s = _dot_nt(q, kf.astype(BF16)) * di_ref[hh]
            state = state_ref[hh]
            o = _dot(s.astype(BF16), v) + _dot(q, state.astype(BF16)) * dq_ref[hh]
            state_ref[hh] = state * dc_ref[hh, 0:1, :] + _dot_tn((kf * dk_ref[hh]).astype(BF16), v)
            mu = jnp.mean(o, axis=-1, keepdims=True)
            oc = o - mu
            var = jnp.mean(oc * oc, axis=-1, keepdims=True)
            gate = g_ref[rows, cols].astype(F32)
            y = gate * jax.nn.sigmoid(gate) * (oc * lax.rsqrt(var + EPS))
            o_ref[rows, cols] = y.astype(o_ref.dtype)
        return carry

    lax.fori_loop(0, n_chunks, body, 0, unroll=4)


def _retention(proj, batch, seq, dk, hp=4):
    heads, chunk = RET_HEADS, RET_CHUNK
    assert heads % hp == 0
    dv = dk
    n_chunks = seq // chunk
    log_gamma = jnp.log1p(-jnp.exp2(-5.0 - jnp.arange(heads, dtype=F32)))
    pos = jnp.arange(chunk, dtype=F32)
    rel = pos[:, None] - pos[None, :]
    d_intra = jnp.where(rel >= 0, jnp.exp(log_gamma[:, None, None] * jnp.maximum(rel, 0.0)), 0.0)
    d_query = jnp.broadcast_to(jnp.exp(log_gamma[:, None] * (pos + 1.0))[..., None], (heads, chunk, dv))
    d_key = jnp.broadcast_to(jnp.exp(log_gamma[:, None] * (chunk - 1.0 - pos))[..., None], (heads, chunk, dk))
    d_chunk = jnp.broadcast_to(jnp.exp(log_gamma * chunk)[:, None, None], (heads, 8, dv))

    col = lambda off: pl.BlockSpec((seq, hp * dk), lambda b, h: (b, off // hp + h))
    tab = lambda r, c: pl.BlockSpec((hp, r, c), lambda b, h: (h, 0, 0))
    return pl.pallas_call(
        functools.partial(_retention_kernel, n_chunks=n_chunks, chunk=chunk, k_scale=dk ** -0.5),
        grid=(batch, heads // hp),
        in_specs=[col(0), col(heads), col(2 * heads), col(3 * heads),
                  tab(chunk, chunk), tab(chunk, dv), tab(chunk, dk), tab(8, dv)],
        out_specs=pl.BlockSpec((seq, hp * dv), lambda b, h: (b, h)),
        out_shape=jax.ShapeDtypeStruct((batch * seq, heads * dv), BF16),
        scratch_shapes=[pltpu.VMEM((hp, dk, dv), F32)],
        compiler_params=_params("parallel", "parallel"), name="retention",
    )(proj, proj, proj, proj, d_intra, d_query, d_key, d_chunk)


def _compress_kernel(x_ref, pos_ref, w1_ref, w2_ref, o_ref, x32_ref, *, n_cmp, n_chunks):
    dh = NSA_DH
    half = CMP_BLOCK // 2
    x32_ref[...] = x_ref[...].astype(F32)
    lo = jnp.zeros((n_chunks, w1_ref.shape[1]), F32)
    hi = jnp.zeros((n_chunks, w1_ref.shape[1]), F32)
    for l in range(half):
        xl = x32_ref[pl.ds(l, n_chunks, stride=CMP_STRIDE), :]
        a = (xl + pos_ref[l:l + 1, :]).astype(BF16)
        b = (xl + pos_ref[half + l:half + l + 1, :]).astype(BF16)
        lo = lo + _dot(a, w1_ref[l * dh:(l + 1) * dh, :])
        hi = hi + _dot(b, w1_ref[(half + l) * dh:(half + l + 1) * dh, :])
    pre = lo + pltpu.roll(hi, n_chunks - 1, axis=0)
    hid = pre * jax.nn.sigmoid(pre)
    out = _dot(hid.astype(BF16), w2_ref[...])
    row = lax.broadcasted_iota(jnp.int32, out.shape, 0)
    o_ref[...] = jnp.where(row < n_cmp, out, 0.0).astype(o_ref.dtype)


def _compress(proj, col_blk, pos, w1, w2, batch, seq):
    kvh, dh = NSA_KV_HEADS, NSA_DH
    n_chunks = seq // CMP_STRIDE
    n_cmp = (seq - CMP_BLOCK) // CMP_STRIDE + 1
    assert CMP_BLOCK == 2 * CMP_STRIDE and n_cmp == n_chunks - 1
    hidden = w1.shape[-1]
    return pl.pallas_call(
        functools.partial(_compress_kernel, n_cmp=n_cmp, n_chunks=n_chunks),
        grid=(2, batch, kvh),
        in_specs=[pl.BlockSpec((seq, dh), lambda c, b, g: (b, col_blk + kvh * c + g)),
                  pl.BlockSpec((None, CMP_BLOCK, dh), lambda c, b, g: (c, 0, 0)),
                  pl.BlockSpec((None, CMP_BLOCK * dh, hidden), lambda c, b, g: (c, 0, 0)),
                  pl.BlockSpec((None, hidden, dh), lambda c, b, g: (c, 0, 0))],
        out_specs=pl.BlockSpec((None, None, None, n_chunks, dh), lambda c, b, g: (c, b, g, 0, 0)),
        out_shape=jax.ShapeDtypeStruct((2, batch, kvh, n_chunks, dh), BF16),
        scratch_shapes=[pltpu.VMEM((seq, dh), F32)],
        compiler_params=_params("parallel", "parallel", "parallel"), name="nsa_compress",
    )(proj, pos, w1, w2)


def _nsa_kernel(q_ref, ks_ref, vs_ref, kw_ref, vw_ref, kc_ref, vc_ref, gl_ref, slope_ref, qx_ref,
                ovt_ref, kxs_ref, kxw_ref, band_ref, o_ref, osel_ref, *, tq, tk, n_cmp, n_sel, k_top):
    dh = NSA_DH
    grp = NSA_GROUP
    sc = (dh ** -0.5) * LOG2E
    span = WINDOW + tq
    t0 = pl.program_id(2) * tq
    q = (q_ref[...].astype(F32) * sc).astype(BF16)
    q4 = jnp.concatenate([q[:, r * dh:(r + 1) * dh] for r in range(grp)], axis=0)
    slope_row = slope_ref[0:1, :]
    tile4 = lambda a: jnp.concatenate([a] * grp, axis=1)

    kc = kc_ref[...]
    vc = vc_ref[...]
    nc = kc.shape[0]
    n_idx = lax.broadcasted_iota(jnp.int32, (nc, tq), 0)
    t_idx = t0 + lax.broadcasted_iota(jnp.int32, (nc, tq), 1)
    dist_c = t_idx - (CMP_STRIDE * n_idx + (CMP_BLOCK - 1))
    dist_c = jnp.where(n_idx < n_cmp, dist_c, -1)
    dist_c = tile4(jnp.where(dist_c >= 0, dist_c.astype(F32), MASK_DIST))
    s = _dot_nt(kc, q4) - slope_row * dist_c
    m = jnp.max(s, axis=0, keepdims=True)
    e = jnp.where(dist_c < MASK_DIST, jnp.exp2(s - m), 0.0)
    l = jnp.sum(e, axis=0, keepdims=True)
    p = e / jnp.where(l > 0.0, l, 1.0)
    o_cmp = _dot_tn(vc, p.astype(BF16))
    p_sum = p[:, 0:tq]
    for r in range(1, grp):
        p_sum = p_sum + p[:, r * tq:(r + 1) * tq]

    p_hi = p_sum.astype(BF16)
    p_lo = (p_sum - p_hi.astype(F32)).astype(BF16)
    ovt = ovt_ref[...]
    imp = _dot(ovt, p_hi) + _dot(ovt, p_lo)
    blk = lax.broadcasted_iota(jnp.int32, (n_sel, tq), 0)
    t_sel = t0 + lax.broadcasted_iota(jnp.int32, (n_sel, tq), 1)
    back = jnp.right_shift(t_sel, SEL_SHIFT) - blk
    forced = jnp.where(blk == 0, 0, jnp.where(back < 0, SEL_LOCAL, back)) < SEL_LOCAL
    imp = jnp.where(forced, FORCED_SCORE, jnp.where(blk * SEL_BLOCK > t_sel, -1.0, imp))
    rank = jnp.zeros((n_sel, tq), F32)
    for j in range(n_sel):
        cand = imp[j:j + 1, :]
        tie = jnp.where(blk > j, 1.0, 0.0)
        rank = rank + jnp.where(cand > imp, 1.0, jnp.where(cand == imp, tie, 0.0))
    unsel = jnp.where(rank < k_top, 0.0, -MASK_POW2)
    unsel = jnp.concatenate([jnp.zeros((SEL_LANE0, tq), F32), unsel,
                             jnp.zeros((LANES - SEL_LANE0 - n_sel, tq), F32)], axis=0).T
    q_aug = jnp.concatenate(
        [q4, jnp.concatenate([(unsel + qx_ref[r:r + 1, :]).astype(BF16) for r in range(grp)], axis=0)],
        axis=1)

    case = jnp.minimum(pl.program_id(2), WINDOW // tq)
    w0 = pl.multiple_of(jnp.maximum(t0 - WINDOW, 0), tq)
    k_aug = jnp.concatenate([kw_ref[pl.ds(w0, span), :], kxw_ref[pl.ds(w0, span), :]], axis=1)
    s = _dot_nt(k_aug, q_aug) + tile4(band_ref[case])
    p = jnp.exp2(s - jnp.max(s, axis=0, keepdims=True))
    o_win = _dot_tn(vw_ref[pl.ds(w0, span), :], p.astype(BF16)) / jnp.sum(p, axis=0, keepdims=True)

    def sel_branch(nk):
        n_chunks = nk // tk
        m = l = acc = None
        for c in range(n_chunks):
            rows = slice(c * tk, (c + 1) * tk)
            k_aug = jnp.concatenate([ks_ref[rows, :], kxs_ref[rows, :]], axis=1)
            s = _dot_nt(k_aug, q_aug)
            if c == n_chunks - 1:
                ahead = (lax.broadcasted_iota(jnp.int32, (tk, tq), 0) + (c * tk - t0)
                         > lax.broadcasted_iota(jnp.int32, (tk, tq), 1))
                s = s + tile4(jnp.where(ahead, NEG_INF, 0.0))
            m_c = jnp.max(s, axis=0, keepdims=True)
            m_new = m_c if c == 0 else jnp.maximum(m, m_c)
            p = jnp.exp2(s - m_new).astype(BF16)
            l_c = jnp.sum(p.astype(F32), axis=0, keepdims=True)
            pv = _dot_tn(vs_ref[rows, :], p)
            if c == 0:
                l, acc = l_c, pv
            else:
                alpha = jnp.exp2(m - m_new)
                l, acc = alpha * l + l_c, alpha * acc + pv
            m = m_new
        return acc / l

    n_kt = (t0 + tq + tk - 1) // tk
    for n in range(1, ks_ref.shape[0] // tk + 1):
        @pl.when(n_kt == n)
        def _():
            osel_ref[...] = sel_branch(n * tk)

    o_sel = osel_ref[...]

    gate = jax.nn.sigmoid(gl_ref[...].T)
    for r in range(grp):
        cols = slice(r * tq, (r + 1) * tq)
        out = (gate[3 * r:3 * r + 1, :] * o_cmp[:, cols]
               + gate[3 * r + 1:3 * r + 2, :] * o_sel[:, cols]
               + gate[3 * r + 2:3 * r + 3, :] * o_win[:, cols])
        o_ref[:, r * dh:(r + 1) * dh] = out.T.astype(o_ref.dtype)


def _nsa(proj, q_blk, kv_blk, cmp_kv, gate_logits, batch, seq, tq=512, tk=512):
    dh, kvh, grp = NSA_DH, NSA_KV_HEADS, NSA_GROUP
    n_cmp = (seq - CMP_BLOCK) // CMP_STRIDE + 1
    nc = cmp_kv.shape[3]
    n_sel = seq // SEL_BLOCK
    k_top = min(SEL_TOPK, n_sel)
    span = WINDOW + tq
    n_case = WINDOW // tq + 1
    assert seq % tk == 0 and seq >= span and WINDOW % tq == 0 and tk % tq == 0 and tk % SEL_BLOCK == 0
    assert SEL_LANE0 + n_sel <= LANES and seq <= 256 * 256 and dh == LANES

    slopes = jnp.exp2(-8.0 * (jnp.arange(NSA_HEADS, dtype=F32) + 1.0) / NSA_HEADS).reshape(kvh, grp)
    slope_tab = jnp.broadcast_to(jnp.repeat(slopes * LOG2E, tq, axis=1)[:, None, :], (kvh, 8, grp * tq))
    pieces, rest = [], slopes * LOG2E
    for _ in range(SLOPE_PIECES):
        piece = rest.astype(BF16).astype(F32)
        pieces += [piece, piece]
        rest = rest - piece
    qx = jnp.zeros((kvh, 8, LANES), F32).at[:, :grp, :2 * SLOPE_PIECES].set(jnp.stack(pieces, axis=-1))
    key = jnp.arange(seq)
    key_cols = jnp.stack([(key >> 8) * 256, key & 255] * SLOPE_PIECES, axis=1).astype(F32)
    kxw = jnp.zeros((seq, LANES), F32).at[:, :2 * SLOPE_PIECES].set(key_cols)
    kxs = kxw.at[key, SEL_LANE0 + key // SEL_BLOCK].set(1.0)
    kxs, kxw = kxs.astype(BF16), kxw.astype(BF16)
    cmp_start = CMP_STRIDE * jnp.arange(n_cmp)
    sel_start = SEL_BLOCK * jnp.arange(n_sel)
    overlap = jnp.clip(jnp.minimum(cmp_start[:, None] + CMP_BLOCK, sel_start[None, :] + SEL_BLOCK)
                       - jnp.maximum(cmp_start[:, None], sel_start[None, :]), 0).astype(F32) / CMP_BLOCK
    ovt = jnp.zeros((n_sel, nc), F32).at[:, :n_cmp].set(overlap.T).astype(BF16)
    dist_w = (jnp.arange(n_case)[:, None, None] * tq + jnp.arange(tq)[None, None, :]
              - jnp.arange(span)[None, :, None])
    band = jnp.where((dist_w >= 0) & (dist_w < WINDOW), 0.0, NEG_INF).astype(F32)

    nq = seq // tq
    kv_spec = lambda c: pl.BlockSpec((seq, dh), lambda b, g, i: (b, kv_blk + kvh * c + g))
    cmp_spec = lambda c: pl.BlockSpec((None, None, None, nc, dh), lambda b, g, i: (c, b, g, 0, 0))
    whole = lambda a: pl.BlockSpec(a.shape, lambda b, g, i: (0,) * a.ndim)
    return pl.pallas_call(
        functools.partial(_nsa_kernel, tq=tq, tk=tk, n_cmp=n_cmp, n_sel=n_sel, k_top=k_top),
        grid=(batch, kvh, nq),
        in_specs=[pl.BlockSpec((tq, grp * dh), lambda b, g, i: (b * nq + i, q_blk + g)),
                  kv_spec(0), kv_spec(1), kv_spec(2), kv_spec(3), cmp_spec(0), cmp_spec(1),
                  pl.BlockSpec((tq, LANES), lambda b, g, i: (b * nq + i, g)),
                  pl.BlockSpec((None, 8, grp * tq), lambda b, g, i: (g, 0, 0)),
                  pl.BlockSpec((None, 8, LANES), lambda b, g, i: (g, 0, 0)),
                  whole(ovt), whole(kxs), whole(kxw), whole(band)],
        out_specs=pl.BlockSpec((tq, grp * dh), lambda b, g, i: (b * nq + i, g)),
        out_shape=jax.ShapeDtypeStruct((batch * seq, kvh * grp * dh), BF16),
        scratch_shapes=[pltpu.VMEM((dh, grp * tq), F32)],
        compiler_params=_params("parallel", "parallel", "arbitrary"), name="nsa_attention",
    )(proj, proj, proj, proj, proj, cmp_kv, cmp_kv, gate_logits, slope_tab, qx, ovt, kxs, kxw, band)


def kernel(x, mix_norm_pre, w_in, cmp_pos_k, cmp_w1_k, cmp_w2_k, cmp_pos_v, cmp_w1_v, cmp_w2_v,
           w_ret_up, w_nsa_up, w_out, mix_norm_post, mlp_norm_pre, w_mlp_in, w_mlp_out,
           mlp_norm_post):
    batch, seq, d = x.shape
    depth = w_in.shape[0]
    ret_w = w_ret_up.shape[1]
    nsa_w = w_nsa_up.shape[1]
    kvh, dh = NSA_KV_HEADS, NSA_DH
    kv_w = kvh * dh
    n_gate = 3 * NSA_HEADS
    c_qn = 4 * ret_w
    c_kv = c_qn + nsa_w
    c_gate = c_kv + 6 * kv_w
    c_gr = c_gate + n_gate
    c_gn = c_gr + d

    w_in_t = jnp.swapaxes(w_in, 1, 2)
    per_grp = n_gate // kvh
    w_gate_t = jnp.pad(w_in_t[:, c_gate:c_gr, :].reshape(depth, kvh, per_grp, d),
                       ((0, 0), (0, 0), (0, LANES - per_grp), (0, 0))).reshape(depth, kvh * LANES, d).astype(BF16)
    w_g_t = _cast_rows(w_in_t, c_gr, 2 * d)
    w_ru = w_ret_up.astype(BF16)
    w_nu = w_nsa_up.astype(BF16)
    cmp_pos = jnp.stack([cmp_pos_k, cmp_pos_v], axis=1)
    cmp_w1 = jnp.stack([cmp_w1_k, cmp_w1_v], axis=1).astype(BF16)
    cmp_w2 = jnp.stack([cmp_w2_k, cmp_w2_v], axis=1).astype(BF16)

    xs = x.reshape(batch * seq, d)
    h = _rmsnorm(xs, mix_norm_pre[0])
    for l in range(depth):
        proj = _matmul_ws(h, w_in_t, l, 0, c_gate, BF16, transposed=True, name="proj_in")
        gate_logits = _matmul_nt(h, w_gate_t[l], F32, name="proj_nsa_gate")
        y_ret = _retention(proj, batch, seq, ret_w // RET_HEADS)
        cmp_kv = _compress(proj, c_kv // dh, cmp_pos[l], cmp_w1[l], cmp_w2[l], batch, seq)
        y_nsa = _nsa(proj, c_qn // (NSA_GROUP * dh), (c_kv + 2 * kv_w) // dh, cmp_kv, gate_logits,
                     batch, seq)
        merged = _merge(y_ret, y_nsa, h, w_ru, w_nu, w_g_t, l)
        mix = _matmul_ws(merged, w_out, l, 0, d, BF16, name="proj_out")
        xs, h = _post(mix, xs, mix_norm_post[l], mlp_norm_pre[l])
        u = _matmul_ws(h, w_mlp_in, l, 0, w_mlp_in.shape[2], BF16, relu2=True, name="mlp_in")
        y = _matmul_ktiled(u, w_mlp_out, l, BF16, name="mlp_out")
        xs, h = _post(y, xs, mlp_norm_post[l], mix_norm_pre[l + 1] if l + 1 < depth else None)
    return xs.reshape(batch, seq, d)
```

```python
import functools
import math

import numpy as np
import jax
import jax.numpy as jnp
from jax import lax
from jax.experimental import pallas as pl
from jax.experimental.pallas import tpu as pltpu

RET_HEADS = 8
RET_CHUNK = 128
NSA_HEADS = 16
NSA_KV_HEADS = 4
NSA_DH = 128
CMP_BLOCK = 32
CMP_STRIDE = 16
SEL_BLOCK = 64
SEL_TOPK = 16
SEL_LOCAL = 2
WINDOW = 512
EPS = 1e-6
NEG_INF = -1e30
FORCED_SCORE = 1e4

NSA_GROUP = NSA_HEADS // NSA_KV_HEADS
SEL_SHIFT = SEL_BLOCK.bit_length() - 1
LANES = 128
VMEM_LIMIT = 56 * 1024 * 1024
LOG2E = math.log2(math.e)
MASK_DIST = 1e32
MASK_POW2 = 2.0 ** 100
SLOPE_PIECES = 3
SEL_LANE0 = 32

F32 = jnp.float32
BF16 = jnp.bfloat16


def _params(*semantics):
    return pltpu.CompilerParams(dimension_semantics=semantics, vmem_limit_bytes=VMEM_LIMIT)


def _dot(a, b):
    return jnp.dot(a, b, preferred_element_type=F32)


def _dot_nt(a, b):
    return lax.dot_general(a, b, (((1,), (1,)), ((), ())), preferred_element_type=F32)


def _dot_tn(a, b):
    return lax.dot_general(a, b, (((0,), (0,)), ((), ())), preferred_element_type=F32)


def _rmsnorm_kernel(x_ref, g_ref, o_ref):
    x = x_ref[...]
    ms = jnp.mean(x * x, axis=-1, keepdims=True)
    o_ref[...] = (x * lax.rsqrt(ms + EPS) * g_ref[...]).astype(o_ref.dtype)


def _rmsnorm(x, gain, tm=256):
    m, d = x.shape
    return pl.pallas_call(
        _rmsnorm_kernel,
        grid=(m // tm,),
        in_specs=[pl.BlockSpec((tm, d), lambda i: (i, 0)),
                  pl.BlockSpec((1, d), lambda i: (0, 0))],
        out_specs=pl.BlockSpec((tm, d), lambda i: (i, 0)),
        out_shape=jax.ShapeDtypeStruct((m, d), BF16),
        compiler_params=_params("parallel"),
        name="rmsnorm",
    )(x, gain.reshape(1, d))


def _post_kernel(y_ref, x_ref, gp_ref, gn_ref, xo_ref, ho_ref):
    y = y_ref[...].astype(F32)
    yn = y * lax.rsqrt(jnp.mean(y * y, axis=-1, keepdims=True) + EPS) * gp_ref[...]
    xn = x_ref[...] + yn
    xo_ref[...] = xn
    ho_ref[...] = (xn * lax.rsqrt(jnp.mean(xn * xn, axis=-1, keepdims=True) + EPS)
                   * gn_ref[...]).astype(ho_ref.dtype)


def _post_last_kernel(y_ref, x_ref, gp_ref, xo_ref):
    y = y_ref[...].astype(F32)
    yn = y * lax.rsqrt(jnp.mean(y * y, axis=-1, keepdims=True) + EPS) * gp_ref[...]
    xo_ref[...] = x_ref[...] + yn


def _post(y, x, g_post, g_next, tm=256):
    m, d = x.shape
    row = pl.BlockSpec((tm, d), lambda i: (i, 0))
    vec = pl.BlockSpec((1, d), lambda i: (0, 0))
    if g_next is None:
        return pl.pallas_call(
            _post_last_kernel, grid=(m // tm,),
            in_specs=[row, row, vec], out_specs=row,
            out_shape=jax.ShapeDtypeStruct((m, d), F32),
            compiler_params=_params("parallel"), name="post_last",
        )(y, x, g_post.reshape(1, d)), None
    return pl.pallas_call(
        _post_kernel, grid=(m // tm,),
        in_specs=[row, row, vec, vec], out_specs=[row, row],
        out_shape=[jax.ShapeDtypeStruct((m, d), F32), jax.ShapeDtypeStruct((m, d), BF16)],
        compiler_params=_params("parallel"), name="post",
    )(y, x, g_post.reshape(1, d), g_next.reshape(1, d))


def _matmul_nt_kernel(a_ref, b_ref, o_ref):
    o_ref[...] = _dot_nt(a_ref[...], b_ref[...]).astype(o_ref.dtype)


def _matmul_nt(a, b_t, out_dtype, tm=1024, tn=1024, name="matmul"):
    m, k = a.shape
    n = b_t.shape[0]
    tm, tn = min(tm, m), min(tn, n)
    return pl.pallas_call(
        _matmul_nt_kernel,
        grid=(m // tm, n // tn),
        in_specs=[pl.BlockSpec((tm, k), lambda i, j: (i, 0)),
                  pl.BlockSpec((tn, k), lambda i, j: (j, 0))],
        out_specs=pl.BlockSpec((tm, tn), lambda i, j: (i, j)),
        out_shape=jax.ShapeDtypeStruct((m, n), out_dtype),
        compiler_params=_params("parallel", "arbitrary"), name=name,
    )(a, b_t)


def _matmul_ws_kernel(a_ref, w_hbm, o_ref, wq_ref, stage_ref, sem, *, relu2, transposed, layer,
                      row0, n_ch):
    j, i = pl.program_id(0), pl.program_id(1)
    ch = stage_ref.shape[1]
    tile = wq_ref.shape[1] if transposed else wq_ref.shape[2]

    def chunk_copy(col, c, slot):
        if transposed:
            src = w_hbm.at[layer, pl.ds(row0 + col * tile + c * ch, ch), :]
        else:
            src = w_hbm.at[layer, pl.ds(c * ch, ch), pl.ds(row0 + col * tile, tile)]
        return pltpu.make_async_copy(src, stage_ref.at[slot], sem.at[slot])

    def convert(buf, c, slot):
        wq_ref[buf, pl.ds(pl.multiple_of(c * ch, ch), ch), :] = stage_ref[slot].astype(BF16)

    @pl.when((j == 0) & (i == 0))
    def _():
        chunk_copy(0, 0, 0).start()
        if n_ch > 1:
            chunk_copy(0, 1, 1).start()
        for c in range(n_ch):
            chunk_copy(0, c, c % 2).wait()
            convert(0, c, c % 2)
            if c + 2 < n_ch:
                chunk_copy(0, c + 2, c % 2).start()

    has_next = j + 1 < pl.num_programs(0)
    nxt = lax.rem(j + 1, 2)

    @pl.when(has_next & (i >= 1) & (i <= n_ch))
    def _():
        slot = lax.rem(i - 1, 2)
        chunk_copy(j + 1, i - 1, slot).wait()
        convert(nxt, i - 1, slot)

    @pl.when(has_next & (i < n_ch))
    def _():
        chunk_copy(j + 1, i, lax.rem(i, 2)).start()

    for buf in range(2):
        @pl.when(lax.rem(j, 2) == buf)
        def _():
            wq = wq_ref[buf]
            acc = _dot_nt(a_ref[...], wq) if transposed else _dot(a_ref[...], wq)
            if relu2:
                acc = jnp.square(jnp.maximum(acc, 0.0))
            o_ref[...] = acc.astype(o_ref.dtype)


def _matmul_ws(a, w, layer, col0, n, out_dtype, tm=1024, tn=1024, relu2=False, transposed=False,
               name="matmul_ws"):
    m, k = a.shape
    tm, tn = min(tm, m), min(tn, n)
    assert n % tn == 0 and m % tm == 0 and col0 % 16 == 0
    n_i = m // tm
    assert n_i >= 2, "the next tile's weights are staged across the row tiles of the current one"
    rows = tn if transposed else k
    n_ch = min(8, n_i - 1)
    while rows % n_ch or (rows // n_ch) % 16:
        n_ch -= 1
    ch = rows // n_ch
    wq_shape = (2, tn, k) if transposed else (2, k, tn)
    return pl.pallas_call(
        functools.partial(_matmul_ws_kernel, relu2=relu2, transposed=transposed, layer=layer,
                          row0=col0, n_ch=n_ch),
        grid=(n // tn, n_i),
        in_specs=[pl.BlockSpec((tm, k), lambda j, i: (i, 0)), pl.BlockSpec(memory_space=pl.ANY)],
        out_specs=pl.BlockSpec((tm, tn), lambda j, i: (i, j)),
        out_shape=jax.ShapeDtypeStruct((m, n), out_dtype),
        scratch_shapes=[pltpu.VMEM(wq_shape, BF16), pltpu.VMEM((2, ch, wq_shape[2]), F32),
                        pltpu.SemaphoreType.DMA((2,))],
        compiler_params=_params("arbitrary", "arbitrary"), name=name,
    )(a, w)


def _matmul_ktiled_kernel(a_ref, w_ref, o_ref, acc_ref):
    kk = pl.program_id(2)

    @pl.when(kk == 0)
    def _():
        acc_ref[...] = jnp.zeros_like(acc_ref)

    acc_ref[...] += _dot(a_ref[...], w_ref[...].astype(BF16))

    @pl.when(kk == pl.num_programs(2) - 1)
    def _():
        o_ref[...] = acc_ref[...].astype(o_ref.dtype)


def _matmul_ktiled(a, w, layer, out_dtype, tm=2048, tn=1024, tk=1024, name="matmul_k"):
    m, k = a.shape
    n = w.shape[2]
    tm, tn, tk = min(tm, m), min(tn, n), min(tk, k)
    return pl.pallas_call(
        _matmul_ktiled_kernel,
        grid=(m // tm, n // tn, k // tk),
        in_specs=[pl.BlockSpec((tm, tk), lambda i, j, kk: (i, kk)),
                  pl.BlockSpec((None, tk, tn), lambda i, j, kk: (layer, kk, j))],
        out_specs=pl.BlockSpec((tm, tn), lambda i, j, kk: (i, j)),
        out_shape=jax.ShapeDtypeStruct((m, n), out_dtype),
        scratch_shapes=[pltpu.VMEM((tm, tn), F32)],
        compiler_params=_params("parallel", "parallel", "arbitrary"), name=name,
    )(a, w)


def _cast_rows_kernel(w_ref, o_ref):
    o_ref[...] = w_ref[0].astype(o_ref.dtype)


def _cast_rows(w, row0, n, tr=512):
    depth, _, k = w.shape
    tr = min(tr, n)
    assert n % tr == 0 and row0 % 16 == 0
    return pl.pallas_call(
        _cast_rows_kernel,
        grid=(depth, n // tr),
        in_specs=[pl.BlockSpec((pl.Element(1), pl.Element(tr), pl.Element(k)),
                               lambda l, j: (l, pl.multiple_of(row0 + j * tr, 16), 0))],
        out_specs=pl.BlockSpec((None, tr, k), lambda l, j: (l, j, 0)),
        out_shape=jax.ShapeDtypeStruct((depth, n, k), BF16),
        compiler_params=_params("parallel", "parallel"), name="cast_rows",
    )(w)


def _merge_kernel(yr_ref, yn_ref, h_ref, wru_ref, wnu_ref, wgr_ref, wgn_ref, o_ref):
    h = h_ref[...]
    a = _dot(yr_ref[...], wru_ref[...])
    b = _dot(yn_ref[...], wnu_ref[...])
    gr = jax.nn.sigmoid(_dot_nt(h, wgr_ref[...]))
    gn = jax.nn.sigmoid(_dot_nt(h, wgn_ref[...]))
    o_ref[...] = (gr * a + gn * b).astype(o_ref.dtype)


def _merge(y_ret, y_nsa, h, w_ru, w_nu, w_g_t, layer, tm=512, tn=512):
    m, d = h.shape
    kr, kn = y_ret.shape[1], y_nsa.shape[1]
    n = w_ru.shape[2]
    tm, tn = min(tm, m), min(tn, n)
    lhs = lambda kk: pl.BlockSpec((tm, kk), lambda i, j: (i, 0))
    rhs = lambda kk: pl.BlockSpec((None, kk, tn), lambda i, j: (layer, 0, j))
    rhs_t = lambda j0: pl.BlockSpec((None, tn, d), lambda i, j: (layer, j0 + j, 0))
    return pl.pallas_call(
        _merge_kernel, grid=(m // tm, n // tn),
        in_specs=[lhs(kr), lhs(kn), lhs(d), rhs(kr), rhs(kn), rhs_t(0), rhs_t(n // tn)],
        out_specs=pl.BlockSpec((tm, tn), lambda i, j: (i, j)),
        out_shape=jax.ShapeDtypeStruct((m, n), BF16),
        compiler_params=_params("parallel", "arbitrary"), name="merge",
    )(y_ret, y_nsa, h, w_ru, w_nu, w_g_t, w_g_t)


def _retention_kernel(q_ref, k_ref, v_ref, g_ref, di_ref, dq_ref, dk_ref, dc_ref, o_ref,
                      state_ref, *, n_chunks, chunk, k_scale):
    state_ref[...] = jnp.zeros_like(state_ref)
    hp = state_ref.shape[0]
    dk = state_ref.shape[1]

    def body(n, carry):
        rows = pl.ds(pl.multiple_of(n * chunk, chunk), chunk)
        for hh in range(hp):
            cols = slice(hh * dk, (hh + 1) * dk)
            q = q_ref[rows, cols]
            kf = k_ref[rows, cols].astype(F32) * k_scale
            v = v_ref[rows, cols]
            s = _dot_nt(q, kf.astype(BF16)) * di_ref[hh]
            state = state_ref[hh]
            o = _dot(s.astype(BF16), v) + _dot(q, state.astype(BF16)) * dq_ref[hh]
            state_ref[hh] = state * dc_ref[hh, 0:1, :] + _dot_tn((kf * dk_ref[hh]).astype(BF16), v)
            mu = jnp.mean(o, axis=-1, keepdims=True)
            oc = o - mu
            var = jnp.mean(oc * oc, axis=-1, keepdims=True)
            gate = g_ref[rows, cols].astype(F32)
            y = gate * jax.nn.sigmoid(gate) * (oc * lax.rsqrt(var + EPS))
            o_ref[rows, cols] = y.astype(o_ref.dtype)
        return carry

    lax.fori_loop(0, n_chunks, body, 0, unroll=4)


def _retention(proj, batch, seq, dk, hp=4):
    heads, chunk = RET_HEADS, RET_CHUNK
    assert heads % hp == 0
    dv = dk
    n_chunks = seq // chunk
    log_gamma = jnp.log1p(-jnp.exp2(-5.0 - jnp.arange(heads, dtype=F32)))
    pos = jnp.arange(chunk, dtype=F32)
    rel = pos[:, None] - pos[None, :]
    d_intra = jnp.where(rel >= 0, jnp.exp(log_gamma[:, None, None] * jnp.maximum(rel, 0.0)), 0.0)
    d_query = jnp.broadcast_to(jnp.exp(log_gamma[:, None] * (pos + 1.0))[..., None], (heads, chunk, dv))
    d_key = jnp.broadcast_to(jnp.exp(log_gamma[:, None] * (chunk - 1.0 - pos))[..., None], (heads, chunk, dk))
    d_chunk = jnp.broadcast_to(jnp.exp(log_gamma * chunk)[:, None, None], (heads, 8, dv))

    col = lambda off: pl.BlockSpec((seq, hp * dk), lambda b, h: (b, off // hp + h))
    tab = lambda r, c: pl.BlockSpec((hp, r, c), lambda b, h: (h, 0, 0))
    return pl.pallas_call(
        functools.partial(_retention_kernel, n_chunks=n_chunks, chunk=chunk, k_scale=dk ** -0.5),
        grid=(batch, heads // hp),
        in_specs=[col(0), col(heads), col(2 * heads), col(3 * heads),
                  tab(chunk, chunk), tab(chunk, dv), tab(chunk, dk), tab(8, dv)],
        out_specs=pl.BlockSpec((seq, hp * dv), lambda b, h: (b, h)),
        out_shape=jax.ShapeDtypeStruct((batch * seq, heads * dv), BF16),
        scratch_shapes=[pltpu.VMEM((hp, dk, dv), F32)],
        compiler_params=_params("parallel", "parallel"), name="retention",
    )(proj, proj, proj, proj, d_intra, d_query, d_key, d_chunk)


def _compress_kernel(x_ref, pos_ref, w1_ref, w2_ref, o_ref, x32_ref, *, n_cmp, n_chunks):
    dh = NSA_DH
    half = CMP_BLOCK // 2
    x32_ref[...] = x_ref[...].astype(F32)
    lo = jnp.zeros((n_chunks, w1_ref.shape[1]), F32)
    hi = jnp.zeros((n_chunks, w1_ref.shape[1]), F32)
    for l in range(half):
        xl = x32_ref[pl.ds(l, n_chunks, stride=CMP_STRIDE), :]
        a = (xl + pos_ref[l:l + 1, :]).astype(BF16)
        b = (xl + pos_ref[half + l:half + l + 1, :]).astype(BF16)
        lo = lo + _dot(a, w1_ref[l * dh:(l + 1) * dh, :])
        hi = hi + _dot(b, w1_ref[(half + l) * dh:(half + l + 1) * dh, :])
    pre = lo + pltpu.roll(hi, n_chunks - 1, axis=0)
    hid = pre * jax.nn.sigmoid(pre)
    out = _dot(hid.astype(BF16), w2_ref[...])
    row = lax.broadcasted_iota(jnp.int32, out.shape, 0)
    o_ref[...] = jnp.where(row < n_cmp, out, 0.0).astype(o_ref.dtype)


def _compress(proj, col_blk, pos, w1, w2, batch, seq):
    kvh, dh = NSA_KV_HEADS, NSA_DH
    n_chunks = seq // CMP_STRIDE
    n_cmp = (seq - CMP_BLOCK) // CMP_STRIDE + 1
    assert CMP_BLOCK == 2 * CMP_STRIDE and n_cmp == n_chunks - 1
    hidden = w1.shape[-1]
    return pl.pallas_call(
        functools.partial(_compress_kernel, n_cmp=n_cmp, n_chunks=n_chunks),
        grid=(2, batch, kvh),
        in_specs=[pl.BlockSpec((seq, dh), lambda c, b, g: (b, col_blk + kvh * c + g)),
                  pl.BlockSpec((None, CMP_BLOCK, dh), lambda c, b, g: (c, 0, 0)),
                  pl.BlockSpec((None, CMP_BLOCK * dh, hidden), lambda c, b, g: (c, 0, 0)),
                  pl.BlockSpec((None, hidden, dh), lambda c, b, g: (c, 0, 0))],
        out_specs=pl.BlockSpec((None, None, None, n_chunks, dh), lambda c, b, g: (c, b, g, 0, 0)),
        out_shape=jax.ShapeDtypeStruct((2, batch, kvh, n_chunks, dh), BF16),
        scratch_shapes=[pltpu.VMEM((seq, dh), F32)],
        compiler_params=_params("parallel", "parallel", "parallel"), name="nsa_compress",
    )(proj, pos, w1, w2)


def _nsa_kernel(q_ref, ks_ref, vs_ref, kw_ref, vw_ref, kc_ref, vc_ref, gl_ref, slope_ref, qx_ref,
                ovt_ref, kxs_ref, kxw_ref, band_ref, o_ref, osel_ref, *, tq, tk, tw, n_cmp, n_sel, k_top):
    dh = NSA_DH
    grp = NSA_GROUP
    sc = (dh ** -0.5) * LOG2E
    t0 = pl.program_id(2) * tq
    q = (q_ref[...].astype(F32) * sc).astype(BF16)
    q4 = jnp.concatenate([q[:, r * dh:(r + 1) * dh] for r in range(grp)], axis=0)
    slope_row = slope_ref[0:1, :]
    tile4 = lambda a: jnp.concatenate([a] * grp, axis=1)

    kc = kc_ref[...]
    vc = vc_ref[...]
    nc = kc.shape[0]
    n_idx = lax.broadcasted_iota(jnp.int32, (nc, tq), 0)
    t_idx = t0 + lax.broadcasted_iota(jnp.int32, (nc, tq), 1)
    dist_c = t_idx - (CMP_STRIDE * n_idx + (CMP_BLOCK - 1))
    dist_c = jnp.where(n_idx < n_cmp, dist_c, -1)
    dist_c = tile4(jnp.where(dist_c >= 0, dist_c.astype(F32), MASK_DIST))
    s = _dot_nt(kc, q4) - slope_row * dist_c
    m = jnp.max(s, axis=0, keepdims=True)
    e = jnp.where(dist_c < MASK_DIST, jnp.exp2(s - m), 0.0)
    l = jnp.sum(e, axis=0, keepdims=True)
    p = e / jnp.where(l > 0.0, l, 1.0)
    o_cmp = _dot_tn(vc, p.astype(BF16))
    p_sum = p[:, 0:tq]
    for r in range(1, grp):
        p_sum = p_sum + p[:, r * tq:(r + 1) * tq]

    p_hi = p_sum.astype(BF16)
    p_lo = (p_sum - p_hi.astype(F32)).astype(BF16)
    ovt = ovt_ref[...]
    imp = _dot(ovt, p_hi) + _dot(ovt, p_lo)
    blk = lax.broadcasted_iota(jnp.int32, (n_sel, tq), 0)
    t_sel = t0 + lax.broadcasted_iota(jnp.int32, (n_sel, tq), 1)
    back = jnp.right_shift(t_sel, SEL_SHIFT) - blk
    forced = jnp.where(blk == 0, 0, jnp.where(back < 0, SEL_LOCAL, back)) < SEL_LOCAL
    imp = jnp.where(forced, FORCED_SCORE, jnp.where(blk * SEL_BLOCK > t_sel, -1.0, imp))
    rank = jnp.zeros((n_sel, tq), F32)
    for j in range(n_sel):
        cand = imp[j:j + 1, :]
        tie = jnp.where(blk > j, 1.0, 0.0)
        rank = rank + jnp.where(cand > imp, 1.0, jnp.where(cand == imp, tie, 0.0))
    unsel = jnp.where(rank < k_top, 0.0, -MASK_POW2)
    unsel = jnp.concatenate([jnp.zeros((SEL_LANE0, tq), F32), unsel,
                             jnp.zeros((LANES - SEL_LANE0 - n_sel, tq), F32)], axis=0).T
    q_aug = jnp.concatenate(
        [q4, jnp.concatenate([(unsel + qx_ref[r:r + 1, :]).astype(BF16) for r in range(grp)], axis=0)],
        axis=1)

    span = WINDOW + tw
    o_parts = []
    for u in range(tq // tw):
        tu = t0 + u * tw
        case = jnp.minimum(tu // tw, WINDOW // tw)
        w0 = pl.multiple_of(jnp.maximum(tu - WINDOW, 0), tw)
        q_u = jnp.concatenate([q_aug[r * tq + u * tw:r * tq + (u + 1) * tw, :] for r in range(grp)], axis=0)
        k_aug = jnp.concatenate([kw_ref[pl.ds(w0, span), :], kxw_ref[pl.ds(w0, span), :]], axis=1)
        s = _dot_nt(k_aug, q_u) + jnp.concatenate([band_ref[case]] * grp, axis=1)
        p = jnp.exp2(s - jnp.max(s, axis=0, keepdims=True))
        o_parts.append(_dot_tn(vw_ref[pl.ds(w0, span), :], p.astype(BF16))
                       / jnp.sum(p, axis=0, keepdims=True))
    o_win = jnp.concatenate([o_parts[u][:, r * tw:(r + 1) * tw]
                             for r in range(grp) for u in range(tq // tw)], axis=1)

    def sel_branch(nk):
        n_chunks = nk // tk

        def scores(c):
            rows = slice(c * tk, (c + 1) * tk)
            k_aug = jnp.concatenate([ks_ref[rows, :], kxs_ref[rows, :]], axis=1)
            s = _dot_nt(k_aug, q_aug)
            if c == n_chunks - 1:
                ahead = (lax.broadcasted_iota(jnp.int32, (tk, tq), 0) + (c * tk - t0)
                         > lax.broadcasted_iota(jnp.int32, (tk, tq), 1))
                s = s + tile4(jnp.where(ahead, NEG_INF, 0.0))
            return s

        m = l = acc = None
        s_next = scores(0)
        for c in range(n_chunks):
            rows = slice(c * tk, (c + 1) * tk)
            s = s_next
            if c + 1 < n_chunks:
                s_next = scores(c + 1)
            m_c = jnp.max(s, axis=0, keepdims=True)
            m_new = m_c if c == 0 else jnp.maximum(m, m_c)
            p = jnp.exp2(s - m_new)
            l_c = jnp.sum(p, axis=0, keepdims=True)
            pv = _dot_tn(vs_ref[rows, :], p.astype(BF16))
            if c == 0:
                l, acc = l_c, pv
            else:
                alpha = jnp.exp2(m - m_new)
                l, acc = alpha * l + l_c, alpha * acc + pv
            m = m_new
        return acc / l

    n_kt = (t0 + tq + tk - 1) // tk
    for n in range(1, ks_ref.shape[0] // tk + 1):
        @pl.when(n_kt == n)
        def _():
            osel_ref[...] = sel_branch(n * tk)

    o_sel = osel_ref[...]

    gate = jax.nn.sigmoid(gl_ref[...].T)
    for r in range(grp):
        cols = slice(r * tq, (r + 1) * tq)
        out = (gate[3 * r:3 * r + 1, :] * o_cmp[:, cols]
               + gate[3 * r + 1:3 * r + 2, :] * o_sel[:, cols]
               + gate[3 * r + 2:3 * r + 3, :] * o_win[:, cols])
        o_ref[:, r * dh:(r + 1) * dh] = out.T.astype(o_ref.dtype)


def _nsa(proj, q_blk, kv_blk, cmp_kv, gate_logits, batch, seq, tq=512, tk=512, tw=256):
    dh, kvh, grp = NSA_DH, NSA_KV_HEADS, NSA_GROUP
    n_cmp = (seq - CMP_BLOCK) // CMP_STRIDE + 1
    nc = cmp_kv.shape[3]
    n_sel = seq // SEL_BLOCK
    k_top = min(SEL_TOPK, n_sel)
    tw = min(tw, tq)
    span = WINDOW + tw
    n_case = WINDOW // tw + 1
    assert seq % tk == 0 and seq >= span and WINDOW % tw == 0 and tq % tw == 0
    assert tk % tq == 0 and tk % SEL_BLOCK == 0
    assert SEL_LANE0 + n_sel <= LANES and seq <= 256 * 256 and dh == LANES

    slopes = jnp.exp2(-8.0 * (jnp.arange(NSA_HEADS, dtype=F32) + 1.0) / NSA_HEADS).reshape(kvh, grp)
    slope_tab = jnp.broadcast_to(jnp.repeat(slopes * LOG2E, tq, axis=1)[:, None, :], (kvh, 8, grp * tq))
    pieces, rest = [], slopes * LOG2E
    for _ in range(SLOPE_PIECES):
        piece = rest.astype(BF16).astype(F32)
        pieces += [piece, piece]
        rest = rest - piece
    qx = jnp.zeros((kvh, 8, LANES), F32).at[:, :grp, :2 * SLOPE_PIECES].set(jnp.stack(pieces, axis=-1))
    key = jnp.arange(seq)
    key_cols = jnp.stack([(key >> 8) * 256, key & 255] * SLOPE_PIECES, axis=1).astype(F32)
    kxw = jnp.zeros((seq, LANES), F32).at[:, :2 * SLOPE_PIECES].set(key_cols)
    kxs = kxw.at[key, SEL_LANE0 + key // SEL_BLOCK].set(1.0)
    kxs, kxw = kxs.astype(BF16), kxw.astype(BF16)
    cmp_start = CMP_STRIDE * jnp.arange(n_cmp)
    sel_start = SEL_BLOCK * jnp.arange(n_sel)
    overlap = jnp.clip(jnp.minimum(cmp_start[:, None] + CMP_BLOCK, sel_start[None, :] + SEL_BLOCK)
                       - jnp.maximum(cmp_start[:, None], sel_start[None, :]), 0).astype(F32) / CMP_BLOCK
    ovt = jnp.zeros((n_sel, nc), F32).at[:, :n_cmp].set(overlap.T).astype(BF16)
    dist_w = (jnp.arange(n_case)[:, None, None] * tw + jnp.arange(tw)[None, None, :]
              - jnp.arange(span)[None, :, None])
    band = jnp.where((dist_w >= 0) & (dist_w < WINDOW), 0.0, NEG_INF).astype(F32)

    nq = seq // tq
    kv_spec = lambda c: pl.BlockSpec((seq, dh), lambda b, g, i: (b, kv_blk + kvh * c + g))
    cmp_spec = lambda c: pl.BlockSpec((None, None, None, nc, dh), lambda b, g, i: (c, b, g, 0, 0))
    whole = lambda a: pl.BlockSpec(a.shape, lambda b, g, i: (0,) * a.ndim)
    return pl.pallas_call(
        functools.partial(_nsa_kernel, tq=tq, tk=tk, tw=tw, n_cmp=n_cmp, n_sel=n_sel, k_top=k_top),
        grid=(batch, kvh, nq),
        in_specs=[pl.BlockSpec((tq, grp * dh), lambda b, g, i: (b * nq + i, q_blk + g)),
                  kv_spec(0), kv_spec(1), kv_spec(2), kv_spec(3), cmp_spec(0), cmp_spec(1),
                  pl.BlockSpec((tq, LANES), lambda b, g, i: (b * nq + i, g)),
                  pl.BlockSpec((None, 8, grp * tq), lambda b, g, i: (g, 0, 0)),
                  pl.BlockSpec((None, 8, LANES), lambda b, g, i: (g, 0, 0)),
                  whole(ovt), whole(kxs), whole(kxw), whole(band)],
        out_specs=pl.BlockSpec((tq, grp * dh), lambda b, g, i: (b * nq + i, g)),
        out_shape=jax.ShapeDtypeStruct((batch * seq, kvh * grp * dh), BF16),
        scratch_shapes=[pltpu.VMEM((dh, grp * tq), F32)],
        compiler_params=_params("parallel", "parallel", "arbitrary"), name="nsa_attention",
    )(proj, proj, proj, proj, proj, cmp_kv, cmp_kv, gate_logits, slope_tab, qx, ovt, kxs, kxw, band)


def kernel(x, mix_norm_pre, w_in, cmp_pos_k, cmp_w1_k, cmp_w2_k, cmp_pos_v, cmp_w1_v, cmp_w2_v,
           w_ret_up, w_nsa_up, w_out, mix_norm_post, mlp_norm_pre, w_mlp_in, w_mlp_out,
           mlp_norm_post):
    batch, seq, d = x.shape
    depth = w_in.shape[0]
    ret_w = w_ret_up.shape[1]
    nsa_w = w_nsa_up.shape[1]
    kvh, dh = NSA_KV_HEADS, NSA_DH
    kv_w = kvh * dh
    n_gate = 3 * NSA_HEADS
    c_qn = 4 * ret_w
    c_kv = c_qn + nsa_w
    c_gate = c_kv + 6 * kv_w
    c_gr = c_gate + n_gate
    c_gn = c_gr + d

    w_in_t = jnp.swapaxes(w_in, 1, 2)
    per_grp = n_gate // kvh
    w_gate_t = jnp.pad(w_in_t[:, c_gate:c_gr, :].reshape(depth, kvh, per_grp, d),
                       ((0, 0), (0, 0), (0, LANES - per_grp), (0, 0))).reshape(depth, kvh * LANES, d).astype(BF16)
    w_g_t = _cast_rows(w_in_t, c_gr, 2 * d)
    w_ru = w_ret_up.astype(BF16)
    w_nu = w_nsa_up.astype(BF16)
    cmp_pos = jnp.stack([cmp_pos_k, cmp_pos_v], axis=1)
    cmp_w1 = jnp.stack([cmp_w1_k, cmp_w1_v], axis=1).astype(BF16)
    cmp_w2 = jnp.stack([cmp_w2_k, cmp_w2_v], axis=1).astype(BF16)

    xs = x.reshape(batch * seq, d)
    h = _rmsnorm(xs, mix_norm_pre[0])
    for l in range(depth):
        proj = _matmul_ws(h, w_in_t, l, 0, c_gate, BF16, transposed=True, name="proj_in")
        gate_logits = _matmul_nt(h, w_gate_t[l], F32, name="proj_nsa_gate")
        y_ret = _retention(proj, batch, seq, ret_w // RET_HEADS)
        cmp_kv = _compress(proj, c_kv // dh, cmp_pos[l], cmp_w1[l], cmp_w2[l], batch, seq)
        y_nsa = _nsa(proj, c_qn // (NSA_GROUP * dh), (c_kv + 2 * kv_w) // dh, cmp_kv, gate_logits,
                     batch, seq)
        merged = _merge(y_ret, y_nsa, h, w_ru, w_nu, w_g_t, l)
        mix = _matmul_ws(merged, w_out, l, 0, d, BF16, name="proj_out")
        xs, h = _post(mix, xs, mix_norm_post[l], mlp_norm_pre[l])
        u = _matmul_ws(h, w_mlp_in, l, 0, w_mlp_in.shape[2], BF16, relu2=True, name="mlp_in")
        y = _matmul_ktiled(u, w_mlp_out, l, BF16, name="mlp_out")
        xs, h = _post(y, xs, mlp_norm_post[l], mix_norm_pre[l + 1] if l + 1 < depth else None)
    return xs.reshape(batch, seq, d)
```

```python
import functools
import math

import jax
import jax.numpy as jnp
from jax import lax
from jax.experimental import pallas as pl
from jax.experimental.pallas import tpu as pltpu

RET_HEADS = 8
RET_CHUNK = 128
NSA_HEADS = 16
NSA_KV_HEADS = 4
NSA_DH = 128
CMP_BLOCK = 32
CMP_STRIDE = 16
SEL_BLOCK = 64
SEL_TOPK = 16
SEL_LOCAL = 2
WINDOW = 512
EPS = 1e-6
NEG_INF = -1e30
FORCED_SCORE = 1e4

NSA_GROUP = NSA_HEADS // NSA_KV_HEADS
SEL_SHIFT = SEL_BLOCK.bit_length() - 1
LANES = 128
VMEM_LIMIT = 56 * 1024 * 1024
LOG2E = math.log2(math.e)
MASK_DIST = 1e32
MASK_POW2 = 2.0 ** 100
SLOPE_PIECES = 3
SEL_LANE0 = 32

F32 = jnp.float32
BF16 = jnp.bfloat16


def _params(*semantics):
    return pltpu.CompilerParams(dimension_semantics=semantics, vmem_limit_bytes=VMEM_LIMIT)


def _dot(a, b):
    return jnp.dot(a, b, preferred_element_type=F32)


def _dot_nt(a, b):
    return lax.dot_general(a, b, (((1,), (1,)), ((), ())), preferred_element_type=F32)


def _dot_tn(a, b):
    return lax.dot_general(a, b, (((0,), (0,)), ((), ())), preferred_element_type=F32)


def _rmsnorm_kernel(x_ref, g_ref, o_ref):
    x = x_ref[...]
    ms = jnp.mean(x * x, axis=-1, keepdims=True)
    o_ref[...] = (x * lax.rsqrt(ms + EPS) * g_ref[...]).astype(o_ref.dtype)


def _rmsnorm(x, gain, tm=256):
    m, d = x.shape
    return pl.pallas_call(
        _rmsnorm_kernel,
        grid=(m // tm,),
        in_specs=[pl.BlockSpec((tm, d), lambda i: (i, 0)),
                  pl.BlockSpec((1, d), lambda i: (0, 0))],
        out_specs=pl.BlockSpec((tm, d), lambda i: (i, 0)),
        out_shape=jax.ShapeDtypeStruct((m, d), BF16),
        compiler_params=_params("parallel"),
        name="rmsnorm",
    )(x, gain.reshape(1, d))


def _post_kernel(y_ref, x_ref, gp_ref, gn_ref, xo_ref, ho_ref):
    y = y_ref[...].astype(F32)
    yn = y * lax.rsqrt(jnp.mean(y * y, axis=-1, keepdims=True) + EPS) * gp_ref[...]
    xn = x_ref[...] + yn
    xo_ref[...] = xn
    ho_ref[...] = (xn * lax.rsqrt(jnp.mean(xn * xn, axis=-1, keepdims=True) + EPS)
                   * gn_ref[...]).astype(ho_ref.dtype)


def _post_last_kernel(y_ref, x_ref, gp_ref, xo_ref):
    y = y_ref[...].astype(F32)
    yn = y * lax.rsqrt(jnp.mean(y * y, axis=-1, keepdims=True) + EPS) * gp_ref[...]
    xo_ref[...] = x_ref[...] + yn


def _post(y, x, g_post, g_next, tm=256):
    m, d = x.shape
    row = pl.BlockSpec((tm, d), lambda i: (i, 0))
    vec = pl.BlockSpec((1, d), lambda i: (0, 0))
    if g_next is None:
        return pl.pallas_call(
            _post_last_kernel, grid=(m // tm,),
            in_specs=[row, row, vec], out_specs=row,
            out_shape=jax.ShapeDtypeStruct((m, d), F32),
            compiler_params=_params("parallel"), name="post_last",
        )(y, x, g_post.reshape(1, d)), None
    return pl.pallas_call(
        _post_kernel, grid=(m // tm,),
        in_specs=[row, row, vec, vec], out_specs=[row, row],
        out_shape=[jax.ShapeDtypeStruct((m, d), F32), jax.ShapeDtypeStruct((m, d), BF16)],
        compiler_params=_params("parallel"), name="post",
    )(y, x, g_post.reshape(1, d), g_next.reshape(1, d))


def _matmul_nt_kernel(a_ref, b_ref, o_ref):
    o_ref[...] = _dot_nt(a_ref[...], b_ref[...]).astype(o_ref.dtype)


def _matmul_nt(a, b_t, out_dtype, tm=1024, tn=1024, name="matmul"):
    m, k = a.shape
    n = b_t.shape[0]
    tm, tn = min(tm, m), min(tn, n)
    return pl.pallas_call(
        _matmul_nt_kernel,
        grid=(m // tm, n // tn),
        in_specs=[pl.BlockSpec((tm, k), lambda i, j: (i, 0)),
                  pl.BlockSpec((tn, k), lambda i, j: (j, 0))],
        out_specs=pl.BlockSpec((tm, tn), lambda i, j: (i, j)),
        out_shape=jax.ShapeDtypeStruct((m, n), out_dtype),
        compiler_params=_params("parallel", "arbitrary"), name=name,
    )(a, b_t)


def _stage_weights(j, i, n_ch, wq_ref, stage_ref, start, wait):
    ch = stage_ref.shape[1]

    def convert(buf, c, slot):
        wq_ref[buf, pl.ds(pl.multiple_of(c * ch, ch), ch), :] = stage_ref[slot].astype(BF16)

    @pl.when((j == 0) & (i == 0))
    def _():
        start(0, 0, 0)
        if n_ch > 1:
            start(0, 1, 1)
        for c in range(n_ch):
            wait(0, c, c % 2)
            convert(0, c, c % 2)
            if c + 2 < n_ch:
                start(0, c + 2, c % 2)

    has_next = j + 1 < pl.num_programs(0)

    @pl.when(has_next & (i >= 1) & (i <= n_ch))
    def _():
        slot = lax.rem(i - 1, 2)
        wait(j + 1, i - 1, slot)
        convert(lax.rem(j + 1, 2), i - 1, slot)

    @pl.when(has_next & (i < n_ch))
    def _():
        start(j + 1, i, lax.rem(i, 2))


def _stage_chunks(rows, n_i):
    n_ch = min(8, n_i - 1)
    while rows % n_ch or (rows // n_ch) % 16:
        n_ch -= 1
    return n_ch


def _matmul_ws_kernel(a_ref, w_hbm, o_ref, wq_ref, stage_ref, sem, *, relu2, transposed, layer,
                      row0, n_ch, split, gap):
    j, i = pl.program_id(0), pl.program_id(1)
    ch = stage_ref.shape[1]
    tile = wq_ref.shape[1] if transposed else wq_ref.shape[2]

    def chunk_copy(col, c, slot):
        first = row0 + col * tile
        if split is not None:
            first = first + jnp.where(col >= split, gap, 0)
        if transposed:
            src = w_hbm.at[layer, pl.ds(pl.multiple_of(first + c * ch, 16), ch), :]
        else:
            src = w_hbm.at[layer, pl.ds(c * ch, ch), pl.ds(first, tile)]
        return pltpu.make_async_copy(src, stage_ref.at[slot], sem.at[slot])

    _stage_weights(j, i, n_ch, wq_ref, stage_ref,
                   lambda col, c, slot: chunk_copy(col, c, slot).start(),
                   lambda col, c, slot: chunk_copy(col, c, slot).wait())

    def compute(buf, gate):
        wq = wq_ref[buf]
        acc = _dot_nt(a_ref[...], wq) if transposed else _dot(a_ref[...], wq)
        if relu2:
            acc = jnp.square(jnp.maximum(acc, 0.0))
        if gate:
            acc = jax.nn.sigmoid(acc)
        o_ref[...] = acc.astype(o_ref.dtype)

    for buf in range(2):
        if split is None:
            pl.when(lax.rem(j, 2) == buf)(functools.partial(compute, buf, False))
        else:
            pl.when((lax.rem(j, 2) == buf) & (j < split))(functools.partial(compute, buf, False))
            pl.when((lax.rem(j, 2) == buf) & (j >= split))(functools.partial(compute, buf, True))


def _matmul_ws(a, w, layer, col0, n, out_dtype, tm=1024, tn=1024, relu2=False, transposed=False,
               split=None, gap=0, name="matmul_ws"):
    m, k = a.shape
    tm, tn = min(tm, m), min(tn, n)
    assert n % tn == 0 and m % tm == 0 and col0 % 16 == 0 and gap % 16 == 0
    n_i = m // tm
    assert n_i >= 2, "the next tile's weights are staged across the row tiles of the current one"
    n_ch = _stage_chunks(tn if transposed else k, n_i)
    wq_shape = (2, tn, k) if transposed else (2, k, tn)
    return pl.pallas_call(
        functools.partial(_matmul_ws_kernel, relu2=relu2, transposed=transposed, layer=layer,
                          row0=col0, n_ch=n_ch, split=split, gap=gap),
        grid=(n // tn, n_i),
        in_specs=[pl.BlockSpec((tm, k), lambda j, i: (i, 0)), pl.BlockSpec(memory_space=pl.ANY)],
        out_specs=pl.BlockSpec((tm, tn), lambda j, i: (i, j)),
        out_shape=jax.ShapeDtypeStruct((m, n), out_dtype),
        scratch_shapes=[pltpu.VMEM(wq_shape, BF16),
                        pltpu.VMEM((2, wq_shape[1] // n_ch, wq_shape[2]), F32),
                        pltpu.SemaphoreType.DMA((2,))],
        compiler_params=_params("arbitrary", "arbitrary"), name=name,
    )(a, w)


def _up_merge_kernel(yr_ref, yn_ref, gr_ref, gn_ref, wr_hbm, wn_hbm, o_ref, wq_ref, stage_ref, sem,
                     *, layer, n_ch):
    j, i = pl.program_id(0), pl.program_id(1)
    ch = stage_ref.shape[1]
    tile = wq_ref.shape[2]
    kr = yr_ref.shape[1]
    half = kr // ch

    def chunk_copy(w_hbm, col, c, slot):
        src = w_hbm.at[layer, pl.ds(pl.multiple_of(c * ch, 16), ch), pl.ds(col * tile, tile)]
        return pltpu.make_async_copy(src, stage_ref.at[slot], sem.at[slot])

    def both(method):
        def go(col, c, slot):
            if isinstance(c, int):
                w_hbm, cc = (wr_hbm, c) if c < half else (wn_hbm, c - half)
                getattr(chunk_copy(w_hbm, col, cc, slot), method)()
            else:
                pl.when(c < half)(lambda: getattr(chunk_copy(wr_hbm, col, c, slot), method)())
                pl.when(c >= half)(lambda: getattr(chunk_copy(wn_hbm, col, c - half, slot), method)())
        return go

    _stage_weights(j, i, n_ch, wq_ref, stage_ref, both("start"), both("wait"))

    for buf in range(2):
        @pl.when(lax.rem(j, 2) == buf)
        def _():
            a = _dot(yr_ref[...], wq_ref[buf, 0:kr, :])
            b = _dot(yn_ref[...], wq_ref[buf, kr:, :])
            o_ref[...] = (gr_ref[...].astype(F32) * a + gn_ref[...].astype(F32) * b).astype(o_ref.dtype)


def _up_merge(y_ret, y_nsa, gates, g_col, w_ru, w_nu, layer, tm=1024, tn=1024):
    m, kr = y_ret.shape
    kn = y_nsa.shape[1]
    n = w_ru.shape[2]
    tm, tn = min(tm, m), min(tn, n)
    while m // tm < 3 and tm % 32 == 0:
        tm //= 2
    n_i, n_j = m // tm, n // tn
    assert m % tm == 0 and n % tn == 0 and n_i >= 3 and kr == kn and g_col % tn == 0
    g_blk = g_col // tn
    n_ch = 2 * _stage_chunks(kr, (n_i + 1) // 2)
    ch = (kr + kn) // n_ch
    return pl.pallas_call(
        functools.partial(_up_merge_kernel, layer=layer, n_ch=n_ch),
        grid=(n_j, n_i),
        in_specs=[pl.BlockSpec((tm, kr), lambda j, i: (i, 0)),
                  pl.BlockSpec((tm, kn), lambda j, i: (i, 0)),
                  pl.BlockSpec((tm, tn), lambda j, i: (i, g_blk + j)),
                  pl.BlockSpec((tm, tn), lambda j, i: (i, g_blk + n_j + j)),
                  pl.BlockSpec(memory_space=pl.ANY), pl.BlockSpec(memory_space=pl.ANY)],
        out_specs=pl.BlockSpec((tm, tn), lambda j, i: (i, j)),
        out_shape=jax.ShapeDtypeStruct((m, n), BF16),
        scratch_shapes=[pltpu.VMEM((2, kr + kn, tn), BF16), pltpu.VMEM((2, ch, tn), F32),
                        pltpu.SemaphoreType.DMA((2,))],
        compiler_params=_params("arbitrary", "arbitrary"), name="up_merge",
    )(y_ret, y_nsa, gates, gates, w_ru, w_nu)


def _matmul_ktiled_kernel(a_ref, w_ref, o_ref, acc_ref):
    kk = pl.program_id(2)

    @pl.when(kk == 0)
    def _():
        acc_ref[...] = jnp.zeros_like(acc_ref)

    acc_ref[...] += _dot(a_ref[...], w_ref[...].astype(BF16))

    @pl.when(kk == pl.num_programs(2) - 1)
    def _():
        o_ref[...] = acc_ref[...].astype(o_ref.dtype)


def _matmul_ktiled(a, w, layer, out_dtype, tm=2048, tn=1024, tk=1024, name="matmul_k"):
    m, k = a.shape
    n = w.shape[2]
    tm, tn, tk = min(tm, m), min(tn, n), min(tk, k)
    return pl.pallas_call(
        _matmul_ktiled_kernel,
        grid=(m // tm, n // tn, k // tk),
        in_specs=[pl.BlockSpec((tm, tk), lambda i, j, kk: (i, kk)),
                  pl.BlockSpec((None, tk, tn), lambda i, j, kk: (layer, kk, j))],
        out_specs=pl.BlockSpec((tm, tn), lambda i, j, kk: (i, j)),
        out_shape=jax.ShapeDtypeStruct((m, n), out_dtype),
        scratch_shapes=[pltpu.VMEM((tm, tn), F32)],
        compiler_params=_params("parallel", "parallel", "arbitrary"), name=name,
    )(a, w)


def _retention_kernel(q_ref, k_ref, v_ref, g_ref, di_ref, dq_ref, dk_ref, dc_ref, o_ref,
                      state_ref, *, n_chunks, chunk, k_scale):
    state_ref[...] = jnp.zeros_like(state_ref)
    hp = state_ref.shape[0]
    dk = state_ref.shape[1]

    def body(n, carry):
        rows = pl.ds(pl.multiple_of(n * chunk, chunk), chunk)
        for hh in range(hp):
            cols = slice(hh * dk, (hh + 1) * dk)
            q = q_ref[rows, cols]
            kf = k_ref[rows, cols].astype(F32) * k_scale
            v = v_ref[rows, cols]
            s = _dot_nt(q, kf.astype(BF16)) * di_ref[hh]
            state = state_ref[hh]
            o = _dot(s.astype(BF16), v) + _dot(q, state.astype(BF16)) * dq_ref[hh]
            state_ref[hh] = state * dc_ref[hh, 0:1, :] + _dot_tn((kf * dk_ref[hh]).astype(BF16), v)
            mu = jnp.mean(o, axis=-1, keepdims=True)
            oc = o - mu
            var = jnp.mean(oc * oc, axis=-1, keepdims=True)
            gate = g_ref[rows, cols].astype(F32)
            y = gate * jax.nn.sigmoid(gate) * (oc * lax.rsqrt(var + EPS))
            o_ref[rows, cols] = y.astype(o_ref.dtype)
        return carry

    lax.fori_loop(0, n_chunks, body, 0, unroll=4)


def _retention(proj, batch, seq, dk, hp=4):
    heads, chunk = RET_HEADS, RET_CHUNK
    assert heads % hp == 0
    dv = dk
    n_chunks = seq // chunk
    log_gamma = jnp.log1p(-jnp.exp2(-5.0 - jnp.arange(heads, dtype=F32)))
    pos = jnp.arange(chunk, dtype=F32)
    rel = pos[:, None] - pos[None, :]
    d_intra = jnp.where(rel >= 0, jnp.exp(log_gamma[:, None, None] * jnp.maximum(rel, 0.0)), 0.0)
    d_query = jnp.broadcast_to(jnp.exp(log_gamma[:, None] * (pos + 1.0))[..., None], (heads, chunk, dv))
    d_key = jnp.broadcast_to(jnp.exp(log_gamma[:, None] * (chunk - 1.0 - pos))[..., None], (heads, chunk, dk))
    d_chunk = jnp.broadcast_to(jnp.exp(log_gamma * chunk)[:, None, None], (heads, 8, dv))

    col = lambda off: pl.BlockSpec((seq, hp * dk), lambda b, h: (b, off // hp + h))
    tab = lambda r, c: pl.BlockSpec((hp, r, c), lambda b, h: (h, 0, 0))
    return pl.pallas_call(
        functools.partial(_retention_kernel, n_chunks=n_chunks, chunk=chunk, k_scale=dk ** -0.5),
        grid=(batch, heads // hp),
        in_specs=[col(0), col(heads), col(2 * heads), col(3 * heads),
                  tab(chunk, chunk), tab(chunk, dv), tab(chunk, dk), tab(8, dv)],
        out_specs=pl.BlockSpec((seq, hp * dv), lambda b, h: (b, h)),
        out_shape=jax.ShapeDtypeStruct((batch * seq, heads * dv), BF16),
        scratch_shapes=[pltpu.VMEM((hp, dk, dv), F32)],
        compiler_params=_params("parallel", "parallel"), name="retention",
    )(proj, proj, proj, proj, d_intra, d_query, d_key, d_chunk)


def _compress_kernel(x_ref, pos_ref, w1_ref, w2_ref, o_ref, x32_ref, *, n_cmp, n_chunks):
    dh = NSA_DH
    half = CMP_BLOCK // 2
    x32_ref[...] = x_ref[...].astype(F32)
    lo = jnp.zeros((n_chunks, w1_ref.shape[1]), F32)
    hi = jnp.zeros((n_chunks, w1_ref.shape[1]), F32)
    for l in range(half):
        xl = x32_ref[pl.ds(l, n_chunks, stride=CMP_STRIDE), :]
        a = (xl + pos_ref[l:l + 1, :]).astype(BF16)
        b = (xl + pos_ref[half + l:half + l + 1, :]).astype(BF16)
        lo = lo + _dot(a, w1_ref[l * dh:(l + 1) * dh, :])
        hi = hi + _dot(b, w1_ref[(half + l) * dh:(half + l + 1) * dh, :])
    pre = lo + pltpu.roll(hi, n_chunks - 1, axis=0)
    hid = pre * jax.nn.sigmoid(pre)
    out = _dot(hid.astype(BF16), w2_ref[...])
    row = lax.broadcasted_iota(jnp.int32, out.shape, 0)
    o_ref[...] = jnp.where(row < n_cmp, out, 0.0).astype(o_ref.dtype)


def _compress(proj, col_blk, pos, w1, w2, batch, seq):
    kvh, dh = NSA_KV_HEADS, NSA_DH
    n_chunks = seq // CMP_STRIDE
    n_cmp = (seq - CMP_BLOCK) // CMP_STRIDE + 1
    assert CMP_BLOCK == 2 * CMP_STRIDE and n_cmp == n_chunks - 1
    hidden = w1.shape[-1]
    return pl.pallas_call(
        functools.partial(_compress_kernel, n_cmp=n_cmp, n_chunks=n_chunks),
        grid=(2, batch, kvh),
        in_specs=[pl.BlockSpec((seq, dh), lambda c, b, g: (b, col_blk + kvh * c + g)),
                  pl.BlockSpec((None, CMP_BLOCK, dh), lambda c, b, g: (c, 0, 0)),
                  pl.BlockSpec((None, CMP_BLOCK * dh, hidden), lambda c, b, g: (c, 0, 0)),
                  pl.BlockSpec((None, hidden, dh), lambda c, b, g: (c, 0, 0))],
        out_specs=pl.BlockSpec((None, None, None, n_chunks, dh), lambda c, b, g: (c, b, g, 0, 0)),
        out_shape=jax.ShapeDtypeStruct((2, batch, kvh, n_chunks, dh), BF16),
        scratch_shapes=[pltpu.VMEM((seq, dh), F32)],
        compiler_params=_params("parallel", "parallel", "parallel"), name="nsa_compress",
    )(proj, pos, w1, w2)


def _nsa_kernel(q_ref, ks_ref, vs_ref, kw_ref, vw_ref, kc_ref, vc_ref, gl_ref, slope_ref, qx_ref,
                ovt_ref, kxs_ref, kxw_ref, band_ref, o_ref, osel_ref, *, tq, tk, tw, n_cmp, n_sel, k_top):
    dh = NSA_DH
    grp = NSA_GROUP
    sc = (dh ** -0.5) * LOG2E
    t0 = pl.program_id(2) * tq
    q = (q_ref[...].astype(F32) * sc).astype(BF16)
    q4 = jnp.concatenate([q[:, r * dh:(r + 1) * dh] for r in range(grp)], axis=0)
    slope_row = slope_ref[0:1, :]
    tile4 = lambda a: jnp.concatenate([a] * grp, axis=1)

    kc = kc_ref[...]
    vc = vc_ref[...]
    nc = kc.shape[0]
    n_idx = lax.broadcasted_iota(jnp.int32, (nc, tq), 0)
    t_idx = t0 + lax.broadcasted_iota(jnp.int32, (nc, tq), 1)
    dist_c = t_idx - (CMP_STRIDE * n_idx + (CMP_BLOCK - 1))
    dist_c = jnp.where(n_idx < n_cmp, dist_c, -1)
    dist_c = tile4(jnp.where(dist_c >= 0, dist_c.astype(F32), MASK_DIST))
    s = _dot_nt(kc, q4) - slope_row * dist_c
    m = jnp.max(s, axis=0, keepdims=True)
    e = jnp.where(dist_c < MASK_DIST, jnp.exp2(s - m), 0.0)
    l = jnp.sum(e, axis=0, keepdims=True)
    p = e / jnp.where(l > 0.0, l, 1.0)
    o_cmp = _dot_tn(vc, p.astype(BF16))
    p_sum = p[:, 0:tq]
    for r in range(1, grp):
        p_sum = p_sum + p[:, r * tq:(r + 1) * tq]

    p_hi = p_sum.astype(BF16)
    p_lo = (p_sum - p_hi.astype(F32)).astype(BF16)
    ovt = ovt_ref[...]
    imp = _dot(ovt, p_hi) + _dot(ovt, p_lo)
    blk = lax.broadcasted_iota(jnp.int32, (n_sel, tq), 0)
    t_sel = t0 + lax.broadcasted_iota(jnp.int32, (n_sel, tq), 1)
    back = jnp.right_shift(t_sel, SEL_SHIFT) - blk
    forced = jnp.where(blk == 0, 0, jnp.where(back < 0, SEL_LOCAL, back)) < SEL_LOCAL
    imp = jnp.where(forced, FORCED_SCORE, jnp.where(blk * SEL_BLOCK > t_sel, -1.0, imp))
    rank = jnp.zeros((n_sel, tq), F32)
    for j in range(n_sel):
        cand = imp[j:j + 1, :]
        tie = jnp.where(blk > j, 1.0, 0.0)
        rank = rank + jnp.where(cand > imp, 1.0, jnp.where(cand == imp, tie, 0.0))
    unsel = jnp.where(rank < k_top, 0.0, -MASK_POW2)
    unsel = jnp.concatenate([jnp.zeros((SEL_LANE0, tq), F32), unsel,
                             jnp.zeros((LANES - SEL_LANE0 - n_sel, tq), F32)], axis=0).T
    q_aug = jnp.concatenate(
        [q4, jnp.concatenate([(unsel + qx_ref[r:r + 1, :]).astype(BF16) for r in range(grp)], axis=0)],
        axis=1)

    span = WINDOW + tw
    o_parts = []
    for u in range(tq // tw):
        tu = t0 + u * tw
        case = jnp.minimum(tu // tw, WINDOW // tw)
        w0 = pl.multiple_of(jnp.maximum(tu - WINDOW, 0), tw)
        q_u = jnp.concatenate([q_aug[r * tq + u * tw:r * tq + (u + 1) * tw, :] for r in range(grp)], axis=0)
        k_aug = jnp.concatenate([kw_ref[pl.ds(w0, span), :], kxw_ref[pl.ds(w0, span), :]], axis=1)
        s = _dot_nt(k_aug, q_u) + jnp.concatenate([band_ref[case]] * grp, axis=1)
        p = jnp.exp2(s - jnp.max(s, axis=0, keepdims=True))
        o_parts.append(_dot_tn(vw_ref[pl.ds(w0, span), :], p.astype(BF16))
                       / jnp.sum(p, axis=0, keepdims=True))
    o_win = jnp.concatenate([o_parts[u][:, r * tw:(r + 1) * tw]
                             for r in range(grp) for u in range(tq // tw)], axis=1)

    def sel_branch(nk):
        n_chunks = nk // tk

        def scores(c):
            rows = slice(c * tk, (c + 1) * tk)
            k_aug = jnp.concatenate([ks_ref[rows, :], kxs_ref[rows, :]], axis=1)
            s = _dot_nt(k_aug, q_aug)
            if c == n_chunks - 1:
                ahead = (lax.broadcasted_iota(jnp.int32, (tk, tq), 0) + (c * tk - t0)
                         > lax.broadcasted_iota(jnp.int32, (tk, tq), 1))
                s = s + tile4(jnp.where(ahead, NEG_INF, 0.0))
            return s

        m = l = acc = None
        s_next = scores(0)
        for c in range(n_chunks):
            rows = slice(c * tk, (c + 1) * tk)
            s = s_next
            if c + 1 < n_chunks:
                s_next = scores(c + 1)
            m_c = jnp.max(s, axis=0, keepdims=True)
            m_new = m_c if c == 0 else jnp.maximum(m, m_c)
            p = jnp.exp2(s - m_new)
            l_c = jnp.sum(p, axis=0, keepdims=True)
            pv = _dot_tn(vs_ref[rows, :], p.astype(BF16))
            if c == 0:
                l, acc = l_c, pv
            else:
                alpha = jnp.exp2(m - m_new)
                l, acc = alpha * l + l_c, alpha * acc + pv
            m = m_new
        return acc / l

    n_kt = (t0 + tq + tk - 1) // tk
    for n in range(1, ks_ref.shape[0] // tk + 1):
        @pl.when(n_kt == n)
        def _():
            osel_ref[...] = sel_branch(n * tk)

    o_sel = osel_ref[...]

    gate = jax.nn.sigmoid(gl_ref[...].T)
    for r in range(grp):
        cols = slice(r * tq, (r + 1) * tq)
        out = (gate[3 * r:3 * r + 1, :] * o_cmp[:, cols]
               + gate[3 * r + 1:3 * r + 2, :] * o_sel[:, cols]
               + gate[3 * r + 2:3 * r + 3, :] * o_win[:, cols])
        o_ref[:, r * dh:(r + 1) * dh] = out.T.astype(o_ref.dtype)


def _nsa(proj, q_blk, kv_blk, cmp_kv, gate_logits, batch, seq, tq=512, tk=512, tw=256):
    dh, kvh, grp = NSA_DH, NSA_KV_HEADS, NSA_GROUP
    n_cmp = (seq - CMP_BLOCK) // CMP_STRIDE + 1
    nc = cmp_kv.shape[3]
    n_sel = seq // SEL_BLOCK
    k_top = min(SEL_TOPK, n_sel)
    tw = min(tw, tq)
    span = WINDOW + tw
    n_case = WINDOW // tw + 1
    assert seq % tk == 0 and seq >= span and WINDOW % tw == 0 and tq % tw == 0
    assert tk % tq == 0 and tk % SEL_BLOCK == 0
    assert SEL_LANE0 + n_sel <= LANES and seq <= 256 * 256 and dh == LANES

    slopes = jnp.exp2(-8.0 * (jnp.arange(NSA_HEADS, dtype=F32) + 1.0) / NSA_HEADS).reshape(kvh, grp)
    slope_tab = jnp.broadcast_to(jnp.repeat(slopes * LOG2E, tq, axis=1)[:, None, :], (kvh, 8, grp * tq))
    pieces, rest = [], slopes * LOG2E
    for _ in range(SLOPE_PIECES):
        piece = rest.astype(BF16).astype(F32)
        pieces += [piece, piece]
        rest = rest - piece
    qx = jnp.zeros((kvh, 8, LANES), F32).at[:, :grp, :2 * SLOPE_PIECES].set(jnp.stack(pieces, axis=-1))
    key = jnp.arange(seq)
    key_cols = jnp.stack([(key >> 8) * 256, key & 255] * SLOPE_PIECES, axis=1).astype(F32)
    kxw = jnp.zeros((seq, LANES), F32).at[:, :2 * SLOPE_PIECES].set(key_cols)
    kxs = kxw.at[key, SEL_LANE0 + key // SEL_BLOCK].set(1.0)
    kxs, kxw = kxs.astype(BF16), kxw.astype(BF16)
    cmp_start = CMP_STRIDE * jnp.arange(n_cmp)
    sel_start = SEL_BLOCK * jnp.arange(n_sel)
    overlap = jnp.clip(jnp.minimum(cmp_start[:, None] + CMP_BLOCK, sel_start[None, :] + SEL_BLOCK)
                       - jnp.maximum(cmp_start[:, None], sel_start[None, :]), 0).astype(F32) / CMP_BLOCK
    ovt = jnp.zeros((n_sel, nc), F32).at[:, :n_cmp].set(overlap.T).astype(BF16)
    dist_w = (jnp.arange(n_case)[:, None, None] * tw + jnp.arange(tw)[None, None, :]
              - jnp.arange(span)[None, :, None])
    band = jnp.where((dist_w >= 0) & (dist_w < WINDOW), 0.0, NEG_INF).astype(F32)

    nq = seq // tq
    kv_spec = lambda c: pl.BlockSpec((seq, dh), lambda b, g, i: (b, kv_blk + kvh * c + g))
    cmp_spec = lambda c: pl.BlockSpec((None, None, None, nc, dh), lambda b, g, i: (c, b, g, 0, 0))
    whole = lambda a: pl.BlockSpec(a.shape, lambda b, g, i: (0,) * a.ndim)
    return pl.pallas_call(
        functools.partial(_nsa_kernel, tq=tq, tk=tk, tw=tw, n_cmp=n_cmp, n_sel=n_sel, k_top=k_top),
        grid=(batch, kvh, nq),
        in_specs=[pl.BlockSpec((tq, grp * dh), lambda b, g, i: (b * nq + i, q_blk + g)),
                  kv_spec(0), kv_spec(1), kv_spec(2), kv_spec(3), cmp_spec(0), cmp_spec(1),
                  pl.BlockSpec((tq, LANES), lambda b, g, i: (b * nq + i, g)),
                  pl.BlockSpec((None, 8, grp * tq), lambda b, g, i: (g, 0, 0)),
                  pl.BlockSpec((None, 8, LANES), lambda b, g, i: (g, 0, 0)),
                  whole(ovt), whole(kxs), whole(kxw), whole(band)],
        out_specs=pl.BlockSpec((tq, grp * dh), lambda b, g, i: (b * nq + i, g)),
        out_shape=jax.ShapeDtypeStruct((batch * seq, kvh * grp * dh), BF16),
        scratch_shapes=[pltpu.VMEM((dh, grp * tq), F32)],
        compiler_params=_params("parallel", "parallel", "arbitrary"), name="nsa_attention",
    )(proj, proj, proj, proj, proj, cmp_kv, cmp_kv, gate_logits, slope_tab, qx, ovt, kxs, kxw, band)


def kernel(x, mix_norm_pre, w_in, cmp_pos_k, cmp_w1_k, cmp_w2_k, cmp_pos_v, cmp_w1_v, cmp_w2_v,
           w_ret_up, w_nsa_up, w_out, mix_norm_post, mlp_norm_pre, w_mlp_in, w_mlp_out,
           mlp_norm_post):
    batch, seq, d = x.shape
    depth = w_in.shape[0]
    ret_w = w_ret_up.shape[1]
    nsa_w = w_nsa_up.shape[1]
    kvh, dh = NSA_KV_HEADS, NSA_DH
    kv_w = kvh * dh
    n_gate = 3 * NSA_HEADS
    c_qn = 4 * ret_w
    c_kv = c_qn + nsa_w
    c_gate = c_kv + 6 * kv_w
    c_gr = c_gate + n_gate

    w_in_t = jnp.swapaxes(w_in, 1, 2)
    per_grp = n_gate // kvh
    w_gate_t = jnp.pad(w_in_t[:, c_gate:c_gr, :].reshape(depth, kvh, per_grp, d),
                       ((0, 0), (0, 0), (0, LANES - per_grp), (0, 0))).reshape(depth, kvh * LANES, d).astype(BF16)
    cmp_pos = jnp.stack([cmp_pos_k, cmp_pos_v], axis=1)
    cmp_w1 = jnp.stack([cmp_w1_k, cmp_w1_v], axis=1).astype(BF16)
    cmp_w2 = jnp.stack([cmp_w2_k, cmp_w2_v], axis=1).astype(BF16)

    xs = x.reshape(batch * seq, d)
    h = _rmsnorm(xs, mix_norm_pre[0])
    for l in range(depth):
        tn_in = math.gcd(c_gate, 1024)
        proj = _matmul_ws(h, w_in_t, l, 0, c_gate + 2 * d, BF16, tn=tn_in, transposed=True,
                          split=c_gate // tn_in, gap=n_gate, name="proj_in")
        gate_logits = _matmul_nt(h, w_gate_t[l], F32, name="proj_nsa_gate")
        y_ret = _retention(proj, batch, seq, ret_w // RET_HEADS)
        cmp_kv = _compress(proj, c_kv // dh, cmp_pos[l], cmp_w1[l], cmp_w2[l], batch, seq)
        y_nsa = _nsa(proj, c_qn // (NSA_GROUP * dh), (c_kv + 2 * kv_w) // dh, cmp_kv, gate_logits,
                     batch, seq)
        merged = _up_merge(y_ret, y_nsa, proj, c_gate, w_ret_up, w_nsa_up, l)
        mix = _matmul_ws(merged, w_out, l, 0, d, BF16, name="proj_out")
        xs, h = _post(mix, xs, mix_norm_post[l], mlp_norm_pre[l])
        u = _matmul_ws(h, w_mlp_in, l, 0, w_mlp_in.shape[2], BF16, relu2=True, name="mlp_in")
        y = _matmul_ktiled(u, w_mlp_out, l, BF16, name="mlp_out")
        xs, h = _post(y, xs, mlp_norm_post[l], mix_norm_pre[l + 1] if l + 1 < depth else None)
    return xs.reshape(batch, seq, d)
```

```python
import functools
import math

import jax
import jax.numpy as jnp
from jax import lax
from jax.experimental import pallas as pl
from jax.experimental.pallas import tpu as pltpu

RET_HEADS = 8
RET_CHUNK = 128
NSA_HEADS = 16
NSA_KV_HEADS = 4
NSA_DH = 128
CMP_BLOCK = 32
CMP_STRIDE = 16
SEL_BLOCK = 64
SEL_TOPK = 16
SEL_LOCAL = 2
WINDOW = 512
EPS = 1e-6
NEG_INF = -1e30
FORCED_SCORE = 1e4

NSA_GROUP = NSA_HEADS // NSA_KV_HEADS
SEL_SHIFT = SEL_BLOCK.bit_length() - 1
LANES = 128
ROW_ALIGN = 16
VMEM_LIMIT = 56 * 1024 * 1024
MAX_STAGE_CHUNKS = 8
LOG2E = math.log2(math.e)
MASK_DIST = 1e32
MASK_POW2 = 2.0 ** 100
SLOPE_PIECES = 3
SEL_LANE0 = 32

F32 = jnp.float32
BF16 = jnp.bfloat16


def _params(*semantics):
    return pltpu.CompilerParams(dimension_semantics=semantics, vmem_limit_bytes=VMEM_LIMIT)


def _dot(a, b):
    return jnp.dot(a, b, preferred_element_type=F32)


def _dot_nt(a, b):
    return lax.dot_general(a, b, (((1,), (1,)), ((), ())), preferred_element_type=F32)


def _dot_tn(a, b):
    return lax.dot_general(a, b, (((0,), (0,)), ((), ())), preferred_element_type=F32)


def _rmsnorm_kernel(x_ref, g_ref, o_ref):
    x = x_ref[...]
    ms = jnp.mean(x * x, axis=-1, keepdims=True)
    o_ref[...] = (x * lax.rsqrt(ms + EPS) * g_ref[...]).astype(o_ref.dtype)


def _rmsnorm(x, gain, tm=256):
    m, d = x.shape
    return pl.pallas_call(
        _rmsnorm_kernel,
        grid=(m // tm,),
        in_specs=[pl.BlockSpec((tm, d), lambda i: (i, 0)),
                  pl.BlockSpec((1, d), lambda i: (0, 0))],
        out_specs=pl.BlockSpec((tm, d), lambda i: (i, 0)),
        out_shape=jax.ShapeDtypeStruct((m, d), BF16),
        compiler_params=_params("parallel"),
        name="rmsnorm",
    )(x, gain.reshape(1, d))


def _post_kernel(y_ref, x_ref, gp_ref, gn_ref, xo_ref, ho_ref):
    y = y_ref[...].astype(F32)
    yn = y * lax.rsqrt(jnp.mean(y * y, axis=-1, keepdims=True) + EPS) * gp_ref[...]
    xn = x_ref[...] + yn
    xo_ref[...] = xn
    ho_ref[...] = (xn * lax.rsqrt(jnp.mean(xn * xn, axis=-1, keepdims=True) + EPS)
                   * gn_ref[...]).astype(ho_ref.dtype)


def _post_last_kernel(y_ref, x_ref, gp_ref, xo_ref):
    y = y_ref[...].astype(F32)
    yn = y * lax.rsqrt(jnp.mean(y * y, axis=-1, keepdims=True) + EPS) * gp_ref[...]
    xo_ref[...] = x_ref[...] + yn


def _post(y, x, g_post, g_next, tm=256):
    m, d = x.shape
    row = pl.BlockSpec((tm, d), lambda i: (i, 0))
    vec = pl.BlockSpec((1, d), lambda i: (0, 0))
    if g_next is None:
        return pl.pallas_call(
            _post_last_kernel, grid=(m // tm,),
            in_specs=[row, row, vec], out_specs=row,
            out_shape=jax.ShapeDtypeStruct((m, d), F32),
            compiler_params=_params("parallel"), name="post_last",
        )(y, x, g_post.reshape(1, d)), None
    return pl.pallas_call(
        _post_kernel, grid=(m // tm,),
        in_specs=[row, row, vec, vec], out_specs=[row, row],
        out_shape=[jax.ShapeDtypeStruct((m, d), F32), jax.ShapeDtypeStruct((m, d), BF16)],
        compiler_params=_params("parallel"), name="post",
    )(y, x, g_post.reshape(1, d), g_next.reshape(1, d))


def _matmul_nt_kernel(a_ref, b_ref, o_ref):
    o_ref[...] = _dot_nt(a_ref[...], b_ref[...]).astype(o_ref.dtype)


def _matmul_nt(a, b_t, out_dtype, tm=1024, tn=1024, name="matmul"):
    m, k = a.shape
    n = b_t.shape[0]
    tm, tn = min(tm, m), min(tn, n)
    return pl.pallas_call(
        _matmul_nt_kernel,
        grid=(m // tm, n // tn),
        in_specs=[pl.BlockSpec((tm, k), lambda i, j: (i, 0)),
                  pl.BlockSpec((tn, k), lambda i, j: (j, 0))],
        out_specs=pl.BlockSpec((tm, tn), lambda i, j: (i, j)),
        out_shape=jax.ShapeDtypeStruct((m, n), out_dtype),
        compiler_params=_params("parallel", "arbitrary"), name=name,
    )(a, b_t)


def _stage_weights(j, i, n_ch, wq_ref, stage_ref, start, wait):
    ch = stage_ref.shape[1]

    def convert(buf, c, slot):
        wq_ref[buf, pl.ds(pl.multiple_of(c * ch, ch), ch), :] = stage_ref[slot].astype(BF16)

    @pl.when((j == 0) & (i == 0))
    def _():
        start(0, 0, 0)
        if n_ch > 1:
            start(0, 1, 1)
        for c in range(n_ch):
            wait(0, c, c % 2)
            convert(0, c, c % 2)
            if c + 2 < n_ch:
                start(0, c + 2, c % 2)

    has_next = j + 1 < pl.num_programs(0)

    @pl.when(has_next & (i >= 1) & (i <= n_ch))
    def _():
        slot = lax.rem(i - 1, 2)
        wait(j + 1, i - 1, slot)
        convert(lax.rem(j + 1, 2), i - 1, slot)

    @pl.when(has_next & (i < n_ch))
    def _():
        start(j + 1, i, lax.rem(i, 2))


def _stage_chunks(rows, n_i):
    n_ch = min(MAX_STAGE_CHUNKS, n_i - 1)
    while rows % n_ch or (rows // n_ch) % ROW_ALIGN:
        n_ch -= 1
    return n_ch


def _matmul_ws_kernel(a_ref, w_hbm, o_ref, wq_ref, stage_ref, sem, *, relu2, transposed, layer,
                      row0, n_ch, split, gap):
    j, i = pl.program_id(0), pl.program_id(1)
    ch = stage_ref.shape[1]
    tile = wq_ref.shape[1] if transposed else wq_ref.shape[2]

    def chunk_copy(col, c, slot):
        first = row0 + col * tile
        if split is not None:
            first = first + jnp.where(col >= split, gap, 0)
        if transposed:
            src = w_hbm.at[layer, pl.ds(pl.multiple_of(first + c * ch, ROW_ALIGN), ch), :]
        else:
            src = w_hbm.at[layer, pl.ds(c * ch, ch), pl.ds(first, tile)]
        return pltpu.make_async_copy(src, stage_ref.at[slot], sem.at[slot])

    _stage_weights(j, i, n_ch, wq_ref, stage_ref,
                   lambda col, c, slot: chunk_copy(col, c, slot).start(),
                   lambda col, c, slot: chunk_copy(col, c, slot).wait())

    def compute(buf, gate):
        wq = wq_ref[buf]
        acc = _dot_nt(a_ref[...], wq) if transposed else _dot(a_ref[...], wq)
        if relu2:
            acc = jnp.square(jnp.maximum(acc, 0.0))
        if gate:
            acc = jax.nn.sigmoid(acc)
        o_ref[...] = acc.astype(o_ref.dtype)

    for buf in range(2):
        if split is None:
            pl.when(lax.rem(j, 2) == buf)(functools.partial(compute, buf, False))
        else:
            pl.when((lax.rem(j, 2) == buf) & (j < split))(functools.partial(compute, buf, False))
            pl.when((lax.rem(j, 2) == buf) & (j >= split))(functools.partial(compute, buf, True))


def _matmul_ws(a, w, layer, col0, n, out_dtype, tm=1024, tn=1024, relu2=False, transposed=False,
               split=None, gap=0, name="matmul_ws"):
    m, k = a.shape
    tm, tn = min(tm, m), min(tn, n)
    assert n % tn == 0 and m % tm == 0 and col0 % ROW_ALIGN == 0 and gap % ROW_ALIGN == 0
    n_i = m // tm
    assert n_i >= 2, "the next tile's weights are staged across the row tiles of the current one"
    n_ch = _stage_chunks(tn if transposed else k, n_i)
    wq_shape = (2, tn, k) if transposed else (2, k, tn)
    return pl.pallas_call(
        functools.partial(_matmul_ws_kernel, relu2=relu2, transposed=transposed, layer=layer,
                          row0=col0, n_ch=n_ch, split=split, gap=gap),
        grid=(n // tn, n_i),
        in_specs=[pl.BlockSpec((tm, k), lambda j, i: (i, 0)), pl.BlockSpec(memory_space=pl.ANY)],
        out_specs=pl.BlockSpec((tm, tn), lambda j, i: (i, j)),
        out_shape=jax.ShapeDtypeStruct((m, n), out_dtype),
        scratch_shapes=[pltpu.VMEM(wq_shape, BF16),
                        pltpu.VMEM((2, wq_shape[1] // n_ch, wq_shape[2]), F32),
                        pltpu.SemaphoreType.DMA((2,))],
        compiler_params=_params("arbitrary", "arbitrary"), name=name,
    )(a, w)


def _up_merge_kernel(yr_ref, yn_ref, gr_ref, gn_ref, wr_hbm, wn_hbm, o_ref, wq_ref, stage_ref, sem,
                     *, layer, n_ch):
    j, i = pl.program_id(0), pl.program_id(1)
    ch = stage_ref.shape[1]
    tile = wq_ref.shape[2]
    kr = yr_ref.shape[1]
    half = kr // ch

    def chunk_copy(w_hbm, col, c, slot):
        src = w_hbm.at[layer, pl.ds(pl.multiple_of(c * ch, ROW_ALIGN), ch), pl.ds(col * tile, tile)]
        return pltpu.make_async_copy(src, stage_ref.at[slot], sem.at[slot])

    def both(method):
        def go(col, c, slot):
            if isinstance(c, int):
                w_hbm, cc = (wr_hbm, c) if c < half else (wn_hbm, c - half)
                getattr(chunk_copy(w_hbm, col, cc, slot), method)()
            else:
                pl.when(c < half)(lambda: getattr(chunk_copy(wr_hbm, col, c, slot), method)())
                pl.when(c >= half)(lambda: getattr(chunk_copy(wn_hbm, col, c - half, slot), method)())
        return go

    _stage_weights(j, i, n_ch, wq_ref, stage_ref, both("start"), both("wait"))

    for buf in range(2):
        @pl.when(lax.rem(j, 2) == buf)
        def _():
            a = _dot(yr_ref[...], wq_ref[buf, 0:kr, :])
            b = _dot(yn_ref[...], wq_ref[buf, kr:, :])
            o_ref[...] = (gr_ref[...].astype(F32) * a + gn_ref[...].astype(F32) * b).astype(o_ref.dtype)


def _up_merge(y_ret, y_nsa, gates, g_col, w_ru, w_nu, layer, tm=1024, tn=1024):
    m, kr = y_ret.shape
    kn = y_nsa.shape[1]
    n = w_ru.shape[2]
    tm, tn = min(tm, m), min(tn, n)
    while m // tm < 3 and tm % (2 * ROW_ALIGN) == 0:
        tm //= 2
    n_i, n_j = m // tm, n // tn
    assert m % tm == 0 and n % tn == 0 and n_i >= 3 and kr == kn and g_col % tn == 0
    g_blk = g_col // tn
    n_ch = 2 * _stage_chunks(kr, (n_i + 1) // 2)
    ch = (kr + kn) // n_ch
    return pl.pallas_call(
        functools.partial(_up_merge_kernel, layer=layer, n_ch=n_ch),
        grid=(n_j, n_i),
        in_specs=[pl.BlockSpec((tm, kr), lambda j, i: (i, 0)),
                  pl.BlockSpec((tm, kn), lambda j, i: (i, 0)),
                  pl.BlockSpec((tm, tn), lambda j, i: (i, g_blk + j)),
                  pl.BlockSpec((tm, tn), lambda j, i: (i, g_blk + n_j + j)),
                  pl.BlockSpec(memory_space=pl.ANY), pl.BlockSpec(memory_space=pl.ANY)],
        out_specs=pl.BlockSpec((tm, tn), lambda j, i: (i, j)),
        out_shape=jax.ShapeDtypeStruct((m, n), BF16),
        scratch_shapes=[pltpu.VMEM((2, kr + kn, tn), BF16), pltpu.VMEM((2, ch, tn), F32),
                        pltpu.SemaphoreType.DMA((2,))],
        compiler_params=_params("arbitrary", "arbitrary"), name="up_merge",
    )(y_ret, y_nsa, gates, gates, w_ru, w_nu)


def _matmul_ktiled_kernel(a_ref, w_ref, o_ref, acc_ref):
    kk = pl.program_id(2)

    @pl.when(kk == 0)
    def _():
        acc_ref[...] = jnp.zeros_like(acc_ref)

    acc_ref[...] += _dot(a_ref[...], w_ref[...].astype(BF16))

    @pl.when(kk == pl.num_programs(2) - 1)
    def _():
        o_ref[...] = acc_ref[...].astype(o_ref.dtype)


def _matmul_ktiled(a, w, layer, out_dtype, tm=2048, tn=1024, tk=1024, name="matmul_k"):
    m, k = a.shape
    n = w.shape[2]
    tm, tn, tk = min(tm, m), min(tn, n), min(tk, k)
    return pl.pallas_call(
        _matmul_ktiled_kernel,
        grid=(m // tm, n // tn, k // tk),
        in_specs=[pl.BlockSpec((tm, tk), lambda i, j, kk: (i, kk)),
                  pl.BlockSpec((None, tk, tn), lambda i, j, kk: (layer, kk, j))],
        out_specs=pl.BlockSpec((tm, tn), lambda i, j, kk: (i, j)),
        out_shape=jax.ShapeDtypeStruct((m, n), out_dtype),
        scratch_shapes=[pltpu.VMEM((tm, tn), F32)],
        compiler_params=_params("parallel", "parallel", "arbitrary"), name=name,
    )(a, w)


def _retention_kernel(q_ref, k_ref, v_ref, g_ref, di_ref, dq_ref, dk_ref, dc_ref, o_ref,
                      state_ref, *, n_chunks, chunk, k_scale):
    state_ref[...] = jnp.zeros_like(state_ref)
    hp = state_ref.shape[0]
    dk = state_ref.shape[1]

    def body(n, carry):
        rows = pl.ds(pl.multiple_of(n * chunk, chunk), chunk)
        for hh in range(hp):
            cols = slice(hh * dk, (hh + 1) * dk)
            q = q_ref[rows, cols]
            kf = k_ref[rows, cols].astype(F32) * k_scale
            v = v_ref[rows, cols]
            s = _dot_nt(q, kf.astype(BF16)) * di_ref[hh]
            state = state_ref[hh]
            o = _dot(s.astype(BF16), v) + _dot(q, state.astype(BF16)) * dq_ref[hh]
            state_ref[hh] = state * dc_ref[hh, 0:1, :] + _dot_tn((kf * dk_ref[hh]).astype(BF16), v)
            mu = jnp.mean(o, axis=-1, keepdims=True)
            oc = o - mu
            var = jnp.mean(oc * oc, axis=-1, keepdims=True)
            gate = g_ref[rows, cols].astype(F32)
            y = gate * jax.nn.sigmoid(gate) * (oc * lax.rsqrt(var + EPS))
            o_ref[rows, cols] = y.astype(o_ref.dtype)
        return carry

    lax.fori_loop(0, n_chunks, body, 0, unroll=4)


def _retention(proj, batch, seq, dk, hp=4):
    heads, chunk = RET_HEADS, RET_CHUNK
    assert heads % hp == 0
    dv = dk
    n_chunks = seq // chunk
    log_gamma = jnp.log1p(-jnp.exp2(-5.0 - jnp.arange(heads, dtype=F32)))
    pos = jnp.arange(chunk, dtype=F32)
    rel = pos[:, None] - pos[None, :]
    d_intra = jnp.where(rel >= 0, jnp.exp(log_gamma[:, None, None] * jnp.maximum(rel, 0.0)), 0.0)
    d_query = jnp.broadcast_to(jnp.exp(log_gamma[:, None] * (pos + 1.0))[..., None], (heads, chunk, dv))
    d_key = jnp.broadcast_to(jnp.exp(log_gamma[:, None] * (chunk - 1.0 - pos))[..., None], (heads, chunk, dk))
    d_chunk = jnp.broadcast_to(jnp.exp(log_gamma * chunk)[:, None, None], (heads, 8, dv))

    col = lambda off: pl.BlockSpec((seq, hp * dk), lambda b, h: (b, off // hp + h))
    tab = lambda r, c: pl.BlockSpec((hp, r, c), lambda b, h: (h, 0, 0))
    return pl.pallas_call(
        functools.partial(_retention_kernel, n_chunks=n_chunks, chunk=chunk, k_scale=dk ** -0.5),
        grid=(batch, heads // hp),
        in_specs=[col(0), col(heads), col(2 * heads), col(3 * heads),
                  tab(chunk, chunk), tab(chunk, dv), tab(chunk, dk), tab(8, dv)],
        out_specs=pl.BlockSpec((seq, hp * dv), lambda b, h: (b, h)),
        out_shape=jax.ShapeDtypeStruct((batch * seq, heads * dv), BF16),
        scratch_shapes=[pltpu.VMEM((hp, dk, dv), F32)],
        compiler_params=_params("parallel", "parallel"), name="retention",
    )(proj, proj, proj, proj, d_intra, d_query, d_key, d_chunk)


def _compress_kernel(x_ref, pos_ref, w1_ref, w2_ref, o_ref, x32_ref, *, n_cmp, n_chunks):
    dh = NSA_DH
    half = CMP_BLOCK // 2
    x32_ref[...] = x_ref[...].astype(F32)
    lo = jnp.zeros((n_chunks, w1_ref.shape[1]), F32)
    hi = jnp.zeros((n_chunks, w1_ref.shape[1]), F32)
    for l in range(half):
        xl = x32_ref[pl.ds(l, n_chunks, stride=CMP_STRIDE), :]
        a = (xl + pos_ref[l:l + 1, :]).astype(BF16)
        b = (xl + pos_ref[half + l:half + l + 1, :]).astype(BF16)
        lo = lo + _dot(a, w1_ref[l * dh:(l + 1) * dh, :])
        hi = hi + _dot(b, w1_ref[(half + l) * dh:(half + l + 1) * dh, :])
    pre = lo + pltpu.roll(hi, n_chunks - 1, axis=0)
    hid = pre * jax.nn.sigmoid(pre)
    out = _dot(hid.astype(BF16), w2_ref[...])
    row = lax.broadcasted_iota(jnp.int32, out.shape, 0)
    o_ref[...] = jnp.where(row < n_cmp, out, 0.0).astype(o_ref.dtype)


def _compress(proj, col_blk, pos, w1, w2, batch, seq):
    kvh, dh = NSA_KV_HEADS, NSA_DH
    n_chunks = seq // CMP_STRIDE
    n_cmp = (seq - CMP_BLOCK) // CMP_STRIDE + 1
    assert CMP_BLOCK == 2 * CMP_STRIDE and n_cmp == n_chunks - 1
    hidden = w1.shape[-1]
    return pl.pallas_call(
        functools.partial(_compress_kernel, n_cmp=n_cmp, n_chunks=n_chunks),
        grid=(2, batch, kvh),
        in_specs=[pl.BlockSpec((seq, dh), lambda c, b, g: (b, col_blk + kvh * c + g)),
                  pl.BlockSpec((None, CMP_BLOCK, dh), lambda c, b, g: (c, 0, 0)),
                  pl.BlockSpec((None, CMP_BLOCK * dh, hidden), lambda c, b, g: (c, 0, 0)),
                  pl.BlockSpec((None, hidden, dh), lambda c, b, g: (c, 0, 0))],
        out_specs=pl.BlockSpec((None, None, None, n_chunks, dh), lambda c, b, g: (c, b, g, 0, 0)),
        out_shape=jax.ShapeDtypeStruct((2, batch, kvh, n_chunks, dh), BF16),
        scratch_shapes=[pltpu.VMEM((seq, dh), F32)],
        compiler_params=_params("parallel", "parallel", "parallel"), name="nsa_compress",
    )(proj, pos, w1, w2)


def _nsa_kernel(q_ref, ks_ref, vs_ref, kw_ref, vw_ref, kc_ref, vc_ref, gl_ref, slope_ref, qx_ref,
                ovt_ref, kxs_ref, kxw_ref, band_ref, o_ref, osel_ref, gate_ref,
                *, tq, tk, tw, n_cmp, n_sel, k_top):
    dh = NSA_DH
    grp = NSA_GROUP
    sc = (dh ** -0.5) * LOG2E
    t0 = pl.program_id(2) * tq
    q = (q_ref[...].astype(F32) * sc).astype(BF16)
    q4 = jnp.concatenate([q[:, r * dh:(r + 1) * dh] for r in range(grp)], axis=0)
    slope_row = slope_ref[0:1, :]
    tile4 = lambda a: jnp.concatenate([a] * grp, axis=1)

    kc = kc_ref[...]
    vc = vc_ref[...]
    nc = kc.shape[0]
    n_idx = lax.broadcasted_iota(jnp.int32, (nc, tq), 0)
    t_idx = t0 + lax.broadcasted_iota(jnp.int32, (nc, tq), 1)
    dist_c = t_idx - (CMP_STRIDE * n_idx + (CMP_BLOCK - 1))
    dist_c = jnp.where(n_idx < n_cmp, dist_c, -1)
    dist_c = tile4(jnp.where(dist_c >= 0, dist_c.astype(F32), MASK_DIST))
    s = _dot_nt(kc, q4) - slope_row * dist_c
    m = jnp.max(s, axis=0, keepdims=True)
    e = jnp.where(dist_c < MASK_DIST, jnp.exp2(s - m), 0.0)
    l = jnp.sum(e, axis=0, keepdims=True)
    p = e / jnp.where(l > 0.0, l, 1.0)
    o_cmp = _dot_tn(vc, p.astype(BF16))
    p_sum = p[:, 0:tq]
    for r in range(1, grp):
        p_sum = p_sum + p[:, r * tq:(r + 1) * tq]

    p_hi = p_sum.astype(BF16)
    p_lo = (p_sum - p_hi.astype(F32)).astype(BF16)
    ovt = ovt_ref[...]
    imp = _dot(ovt, p_hi) + _dot(ovt, p_lo)
    blk = lax.broadcasted_iota(jnp.int32, (n_sel, tq), 0)
    t_sel = t0 + lax.broadcasted_iota(jnp.int32, (n_sel, tq), 1)
    back = jnp.right_shift(t_sel, SEL_SHIFT) - blk
    forced = jnp.where(blk == 0, 0, jnp.where(back < 0, SEL_LOCAL, back)) < SEL_LOCAL
    imp = jnp.where(forced, FORCED_SCORE, jnp.where(blk * SEL_BLOCK > t_sel, -1.0, imp))
    rank = jnp.zeros((n_sel, tq), F32)
    for j in range(n_sel):
        cand = imp[j:j + 1, :]
        tie = jnp.where(blk > j, 1.0, 0.0)
        rank = rank + jnp.where(cand > imp, 1.0, jnp.where(cand == imp, tie, 0.0))
    unsel = jnp.where(rank < k_top, 0.0, -MASK_POW2)
    unsel = jnp.concatenate([jnp.zeros((SEL_LANE0, tq), F32), unsel,
                             jnp.zeros((LANES - SEL_LANE0 - n_sel, tq), F32)], axis=0).T
    q_aug = jnp.concatenate(
        [q4, jnp.concatenate([(unsel + qx_ref[r:r + 1, :]).astype(BF16) for r in range(grp)], axis=0)],
        axis=1)

    span = WINDOW + tw
    o_parts = []
    for u in range(tq // tw):
        tu = t0 + u * tw
        case = jnp.minimum(tu // tw, WINDOW // tw)
        w0 = pl.multiple_of(jnp.maximum(tu - WINDOW, 0), tw)
        q_u = jnp.concatenate([q_aug[r * tq + u * tw:r * tq + (u + 1) * tw, :] for r in range(grp)], axis=0)
        k_aug = jnp.concatenate([kw_ref[pl.ds(w0, span), :], kxw_ref[pl.ds(w0, span), :]], axis=1)
        s = _dot_nt(k_aug, q_u) + jnp.concatenate([band_ref[case]] * grp, axis=1)
        p = jnp.exp2(s - jnp.max(s, axis=0, keepdims=True))
        o_parts.append(_dot_tn(vw_ref[pl.ds(w0, span), :], p.astype(BF16))
                       / jnp.sum(p, axis=0, keepdims=True))
    o_win = jnp.concatenate([o_parts[u][:, r * tw:(r + 1) * tw]
                             for r in range(grp) for u in range(tq // tw)], axis=1)

    def sel_branch(nk):
        n_chunks = nk // tk

        def scores(c):
            rows = slice(c * tk, (c + 1) * tk)
            k_aug = jnp.concatenate([ks_ref[rows, :], kxs_ref[rows, :]], axis=1)
            s = _dot_nt(k_aug, q_aug)
            if c == n_chunks - 1:
                ahead = (lax.broadcasted_iota(jnp.int32, (tk, tq), 0) + (c * tk - t0)
                         > lax.broadcasted_iota(jnp.int32, (tk, tq), 1))
                s = s + tile4(jnp.where(ahead, NEG_INF, 0.0))
            return s

        m = l = acc = None
        s_next = scores(0)
        for c in range(n_chunks):
            rows = slice(c * tk, (c + 1) * tk)
            s = s_next
            if c + 1 < n_chunks:
                s_next = scores(c + 1)
            m_c = jnp.max(s, axis=0, keepdims=True)
            m_new = m_c if c == 0 else jnp.maximum(m, m_c)
            p = jnp.exp2(s - m_new)
            l_c = jnp.sum(p, axis=0, keepdims=True)
            pv = _dot_tn(vs_ref[rows, :], p.astype(BF16))
            if c == 0:
                l, acc = l_c, pv
            else:
                alpha = jnp.exp2(m - m_new)
                l, acc = alpha * l + l_c, alpha * acc + pv
            m = m_new
        return acc / l

    n_kt = (t0 + tq + tk - 1) // tk
    for n in range(1, ks_ref.shape[0] // tk + 1):
        @pl.when(n_kt == n)
        def _():
            osel_ref[...] = sel_branch(n * tk)

    o_sel = osel_ref[...]

    gate_ref[...] = jax.nn.sigmoid(gl_ref[...].T)
    row0 = 3 * grp * pl.program_id(1)
    for r in range(grp):
        cols = slice(r * tq, (r + 1) * tq)
        out = (gate_ref[pl.ds(row0 + 3 * r, 1), :] * o_cmp[:, cols]
               + gate_ref[pl.ds(row0 + 3 * r + 1, 1), :] * o_sel[:, cols]
               + gate_ref[pl.ds(row0 + 3 * r + 2, 1), :] * o_win[:, cols])
        o_ref[:, r * dh:(r + 1) * dh] = out.T.astype(o_ref.dtype)


def _nsa(proj, q_blk, kv_blk, cmp_kv, gate_logits, batch, seq, tq=512, tk=512, tw=256):
    dh, kvh, grp = NSA_DH, NSA_KV_HEADS, NSA_GROUP
    n_cmp = (seq - CMP_BLOCK) // CMP_STRIDE + 1
    nc = cmp_kv.shape[3]
    n_sel = seq // SEL_BLOCK
    k_top = min(SEL_TOPK, n_sel)
    tw = min(tw, tq)
    span = WINDOW + tw
    n_case = WINDOW // tw + 1
    assert seq % tk == 0 and seq >= span and WINDOW % tw == 0 and tq % tw == 0
    assert tk % tq == 0 and tk % SEL_BLOCK == 0
    assert SEL_LANE0 + n_sel <= LANES and seq <= 256 * 256 and dh == LANES

    slopes = jnp.exp2(-8.0 * (jnp.arange(NSA_HEADS, dtype=F32) + 1.0) / NSA_HEADS).reshape(kvh, grp)
    slope_tab = jnp.broadcast_to(jnp.repeat(slopes * LOG2E, tq, axis=1)[:, None, :], (kvh, 8, grp * tq))
    pieces, rest = [], slopes * LOG2E
    for _ in range(SLOPE_PIECES):
        piece = rest.astype(BF16).astype(F32)
        pieces += [piece, piece]
        rest = rest - piece
    qx = jnp.zeros((kvh, 8, LANES), F32).at[:, :grp, :2 * SLOPE_PIECES].set(jnp.stack(pieces, axis=-1))
    key = jnp.arange(seq)
    key_cols = jnp.stack([(key >> 8) * 256, key & 255] * SLOPE_PIECES, axis=1).astype(F32)
    kxw = jnp.zeros((seq, LANES), F32).at[:, :2 * SLOPE_PIECES].set(key_cols)
    kxs = kxw.at[key, SEL_LANE0 + key // SEL_BLOCK].set(1.0)
    kxs, kxw = kxs.astype(BF16), kxw.astype(BF16)
    cmp_start = CMP_STRIDE * jnp.arange(n_cmp)
    sel_start = SEL_BLOCK * jnp.arange(n_sel)
    overlap = jnp.clip(jnp.minimum(cmp_start[:, None] + CMP_BLOCK, sel_start[None, :] + SEL_BLOCK)
                       - jnp.maximum(cmp_start[:, None], sel_start[None, :]), 0).astype(F32) / CMP_BLOCK
    ovt = jnp.zeros((n_sel, nc), F32).at[:, :n_cmp].set(overlap.T).astype(BF16)
    dist_w = (jnp.arange(n_case)[:, None, None] * tw + jnp.arange(tw)[None, None, :]
              - jnp.arange(span)[None, :, None])
    band = jnp.where((dist_w >= 0) & (dist_w < WINDOW), 0.0, NEG_INF).astype(F32)

    nq = seq // tq
    kv_spec = lambda c: pl.BlockSpec((seq, dh), lambda b, g, i: (b, kv_blk + kvh * c + g))
    cmp_spec = lambda c: pl.BlockSpec((None, None, None, nc, dh), lambda b, g, i: (c, b, g, 0, 0))
    whole = lambda a: pl.BlockSpec(a.shape, lambda b, g, i: (0,) * a.ndim)
    return pl.pallas_call(
        functools.partial(_nsa_kernel, tq=tq, tk=tk, tw=tw, n_cmp=n_cmp, n_sel=n_sel, k_top=k_top),
        grid=(batch, kvh, nq),
        in_specs=[pl.BlockSpec((tq, grp * dh), lambda b, g, i: (b * nq + i, q_blk + g)),
                  kv_spec(0), kv_spec(1), kv_spec(2), kv_spec(3), cmp_spec(0), cmp_spec(1),
                  pl.BlockSpec((tq, LANES), lambda b, g, i: (b * nq + i, 0)),
                  pl.BlockSpec((None, 8, grp * tq), lambda b, g, i: (g, 0, 0)),
                  pl.BlockSpec((None, 8, LANES), lambda b, g, i: (g, 0, 0)),
                  whole(ovt), whole(kxs), whole(kxw), whole(band)],
        out_specs=pl.BlockSpec((tq, grp * dh), lambda b, g, i: (b * nq + i, g)),
        out_shape=jax.ShapeDtypeStruct((batch * seq, kvh * grp * dh), BF16),
        scratch_shapes=[pltpu.VMEM((dh, grp * tq), F32), pltpu.VMEM((LANES, tq), F32)],
        compiler_params=_params("parallel", "parallel", "arbitrary"), name="nsa_attention",
    )(proj, proj, proj, proj, proj, cmp_kv, cmp_kv, gate_logits, slope_tab, qx, ovt, kxs, kxw, band)


def kernel(x, mix_norm_pre, w_in, cmp_pos_k, cmp_w1_k, cmp_w2_k, cmp_pos_v, cmp_w1_v, cmp_w2_v,
           w_ret_up, w_nsa_up, w_out, mix_norm_post, mlp_norm_pre, w_mlp_in, w_mlp_out,
           mlp_norm_post):
    batch, seq, d = x.shape
    depth = w_in.shape[0]
    ret_w = w_ret_up.shape[1]
    nsa_w = w_nsa_up.shape[1]
    kvh, dh = NSA_KV_HEADS, NSA_DH
    kv_w = kvh * dh
    n_gate = 3 * NSA_HEADS
    c_qn = 4 * ret_w
    c_kv = c_qn + nsa_w
    c_gate = c_kv + 6 * kv_w
    c_gr = c_gate + n_gate

    w_in_t = jnp.swapaxes(w_in, 1, 2)
    assert n_gate <= LANES
    w_gate_t = jnp.pad(w_in_t[:, c_gate:c_gr, :], ((0, 0), (0, LANES - n_gate), (0, 0))).astype(BF16)
    cmp_pos = jnp.stack([cmp_pos_k, cmp_pos_v], axis=1)
    cmp_w1 = jnp.stack([cmp_w1_k, cmp_w1_v], axis=1).astype(BF16)
    cmp_w2 = jnp.stack([cmp_w2_k, cmp_w2_v], axis=1).astype(BF16)

    xs = x.reshape(batch * seq, d)
    h = _rmsnorm(xs, mix_norm_pre[0])
    for l in range(depth):
        tn_in = math.gcd(c_gate, 1024)
        proj = _matmul_ws(h, w_in_t, l, 0, c_gate + 2 * d, BF16, tn=tn_in, transposed=True,
                          split=c_gate // tn_in, gap=n_gate, name="proj_in")
        gate_logits = _matmul_nt(h, w_gate_t[l], F32, name="proj_nsa_gate")
        y_ret = _retention(proj, batch, seq, ret_w // RET_HEADS)
        cmp_kv = _compress(proj, c_kv // dh, cmp_pos[l], cmp_w1[l], cmp_w2[l], batch, seq)
        y_nsa = _nsa(proj, c_qn // (NSA_GROUP * dh), (c_kv + 2 * kv_w) // dh, cmp_kv, gate_logits,
                     batch, seq)
        merged = _up_merge(y_ret, y_nsa, proj, c_gate, w_ret_up, w_nsa_up, l)
        mix = _matmul_ws(merged, w_out, l, 0, d, BF16, name="proj_out")
        xs, h = _post(mix, xs, mix_norm_post[l], mlp_norm_pre[l])
        u = _matmul_ws(h, w_mlp_in, l, 0, w_mlp_in.shape[2], BF16, relu2=True, name="mlp_in")
        y = _matmul_ktiled(u, w_mlp_out, l, BF16, name="mlp_out")
        xs, h = _post(y, xs, mlp_norm_post[l], mix_norm_pre[l + 1] if l + 1 < depth else None)
    return xs.reshape(batch, seq, d)
```

```python
import functools
import math

import jax
import jax.numpy as jnp
from jax import lax
from jax.experimental import pallas as pl
from jax.experimental.pallas import tpu as pltpu

RET_HEADS = 8
RET_CHUNK = 128
NSA_HEADS = 16
NSA_KV_HEADS = 4
NSA_DH = 128
CMP_BLOCK = 32
CMP_STRIDE = 16
SEL_BLOCK = 64
SEL_TOPK = 16
SEL_LOCAL = 2
WINDOW = 512
EPS = 1e-6
NEG_INF = -1e30
FORCED_SCORE = 1e4

NSA_GROUP = NSA_HEADS // NSA_KV_HEADS
SEL_SHIFT = SEL_BLOCK.bit_length() - 1
LANES = 128
ROW_ALIGN = 16
VMEM_LIMIT = 56 * 1024 * 1024
MAX_STAGE_CHUNKS = 8
LOG2E = math.log2(math.e)
MASK_DIST = 1e32
MASK_POW2 = 2.0 ** 100
SLOPE_PIECES = 3
SEL_LANE0 = 32

F32 = jnp.float32
BF16 = jnp.bfloat16


def _params(*semantics):
    return pltpu.CompilerParams(dimension_semantics=semantics, vmem_limit_bytes=VMEM_LIMIT)


def _dot(a, b):
    return jnp.dot(a, b, preferred_element_type=F32)


def _dot_nt(a, b):
    return lax.dot_general(a, b, (((1,), (1,)), ((), ())), preferred_element_type=F32)


def _dot_tn(a, b):
    return lax.dot_general(a, b, (((0,), (0,)), ((), ())), preferred_element_type=F32)


def _rmsnorm_kernel(x_ref, g_ref, o_ref):
    x = x_ref[...]
    ms = jnp.mean(x * x, axis=-1, keepdims=True)
    o_ref[...] = (x * lax.rsqrt(ms + EPS) * g_ref[...]).astype(o_ref.dtype)


def _rmsnorm(x, gain, tm=256):
    m, d = x.shape
    return pl.pallas_call(
        _rmsnorm_kernel,
        grid=(m // tm,),
        in_specs=[pl.BlockSpec((tm, d), lambda i: (i, 0)),
                  pl.BlockSpec((1, d), lambda i: (0, 0))],
        out_specs=pl.BlockSpec((tm, d), lambda i: (i, 0)),
        out_shape=jax.ShapeDtypeStruct((m, d), BF16),
        compiler_params=_params("parallel"),
        name="rmsnorm",
    )(x, gain.reshape(1, d))


def _post_kernel(y_ref, x_ref, gp_ref, gn_ref, xo_ref, ho_ref):
    y = y_ref[...].astype(F32)
    yn = y * lax.rsqrt(jnp.mean(y * y, axis=-1, keepdims=True) + EPS) * gp_ref[...]
    xn = x_ref[...] + yn
    xo_ref[...] = xn
    ho_ref[...] = (xn * lax.rsqrt(jnp.mean(xn * xn, axis=-1, keepdims=True) + EPS)
                   * gn_ref[...]).astype(ho_ref.dtype)


def _post_last_kernel(y_ref, x_ref, gp_ref, xo_ref):
    y = y_ref[...].astype(F32)
    yn = y * lax.rsqrt(jnp.mean(y * y, axis=-1, keepdims=True) + EPS) * gp_ref[...]
    xo_ref[...] = x_ref[...] + yn


def _post(y, x, g_post, g_next, tm=256):
    m, d = x.shape
    row = pl.BlockSpec((tm, d), lambda i: (i, 0))
    vec = pl.BlockSpec((1, d), lambda i: (0, 0))
    if g_next is None:
        return pl.pallas_call(
            _post_last_kernel, grid=(m // tm,),
            in_specs=[row, row, vec], out_specs=row,
            out_shape=jax.ShapeDtypeStruct((m, d), F32),
            compiler_params=_params("parallel"), name="post_last",
        )(y, x, g_post.reshape(1, d)), None
    return pl.pallas_call(
        _post_kernel, grid=(m // tm,),
        in_specs=[row, row, vec, vec], out_specs=[row, row],
        out_shape=[jax.ShapeDtypeStruct((m, d), F32), jax.ShapeDtypeStruct((m, d), BF16)],
        compiler_params=_params("parallel"), name="post",
    )(y, x, g_post.reshape(1, d), g_next.reshape(1, d))


def _matmul_nt_kernel(a_ref, b_ref, o_ref):
    o_ref[...] = _dot_nt(a_ref[...], b_ref[...]).astype(o_ref.dtype)


def _matmul_nt(a, b_t, out_dtype, tm=1024, tn=1024, name="matmul"):
    m, k = a.shape
    n = b_t.shape[0]
    tm, tn = min(tm, m), min(tn, n)
    return pl.pallas_call(
        _matmul_nt_kernel,
        grid=(m // tm, n // tn),
        in_specs=[pl.BlockSpec((tm, k), lambda i, j: (i, 0)),
                  pl.BlockSpec((tn, k), lambda i, j: (j, 0))],
        out_specs=pl.BlockSpec((tm, tn), lambda i, j: (i, j)),
        out_shape=jax.ShapeDtypeStruct((m, n), out_dtype),
        compiler_params=_params("parallel", "arbitrary"), name=name,
    )(a, b_t)


def _stage_weights(j, i, n_ch, wq_ref, stage_ref, start, wait):
    ch = stage_ref.shape[1]

    def convert(buf, c, slot):
        wq_ref[buf, pl.ds(pl.multiple_of(c * ch, ch), ch), :] = stage_ref[slot].astype(BF16)

    @pl.when((j == 0) & (i == 0))
    def _():
        start(0, 0, 0)
        if n_ch > 1:
            start(0, 1, 1)
        for c in range(n_ch):
            wait(0, c, c % 2)
            convert(0, c, c % 2)
            if c + 2 < n_ch:
                start(0, c + 2, c % 2)

    has_next = j + 1 < pl.num_programs(0)

    @pl.when(has_next & (i >= 1) & (i <= n_ch))
    def _():
        slot = lax.rem(i - 1, 2)
        wait(j + 1, i - 1, slot)
        convert(lax.rem(j + 1, 2), i - 1, slot)

    @pl.when(has_next & (i < n_ch))
    def _():
        start(j + 1, i, lax.rem(i, 2))


def _stage_chunks(rows, n_i):
    n_ch = min(MAX_STAGE_CHUNKS, n_i - 1)
    while rows % n_ch or (rows // n_ch) % ROW_ALIGN:
        n_ch -= 1
    return n_ch


def _matmul_ws_kernel(*refs, relu2, transposed, layer, row0, n_ch, split, gap, side):
    if side:
        a_ref, w_hbm, side_ref, o_ref, side_o_ref, wq_ref, stage_ref, sem = refs
        side_o_ref[...] = side_ref[...].astype(side_o_ref.dtype)
    else:
        a_ref, w_hbm, o_ref, wq_ref, stage_ref, sem = refs
    j, i = pl.program_id(0), pl.program_id(1)
    ch = stage_ref.shape[1]
    tile = wq_ref.shape[1] if transposed else wq_ref.shape[2]

    def chunk_copy(col, c, slot):
        first = row0 + col * tile
        if split is not None:
            first = first + jnp.where(col >= split, gap, 0)
        if transposed:
            src = w_hbm.at[layer, pl.ds(pl.multiple_of(first + c * ch, ROW_ALIGN), ch), :]
        else:
            src = w_hbm.at[layer, pl.ds(c * ch, ch), pl.ds(first, tile)]
        return pltpu.make_async_copy(src, stage_ref.at[slot], sem.at[slot])

    _stage_weights(j, i, n_ch, wq_ref, stage_ref,
                   lambda col, c, slot: chunk_copy(col, c, slot).start(),
                   lambda col, c, slot: chunk_copy(col, c, slot).wait())

    def compute(buf, gate):
        wq = wq_ref[buf]
        acc = _dot_nt(a_ref[...], wq) if transposed else _dot(a_ref[...], wq)
        if relu2:
            acc = jnp.square(jnp.maximum(acc, 0.0))
        if gate:
            acc = jax.nn.sigmoid(acc)
        o_ref[...] = acc.astype(o_ref.dtype)

    for buf in range(2):
        if split is None:
            pl.when(lax.rem(j, 2) == buf)(functools.partial(compute, buf, False))
        else:
            pl.when((lax.rem(j, 2) == buf) & (j < split))(functools.partial(compute, buf, False))
            pl.when((lax.rem(j, 2) == buf) & (j >= split))(functools.partial(compute, buf, True))


def _matmul_ws(a, w, layer, col0, n, out_dtype, tm=1024, tn=1024, relu2=False, transposed=False,
               split=None, gap=0, side=None, name="matmul_ws"):
    m, k = a.shape
    tm, tn = min(tm, m), min(tn, n)
    assert n % tn == 0 and m % tm == 0 and col0 % ROW_ALIGN == 0 and gap % ROW_ALIGN == 0
    n_i = m // tm
    assert n_i >= 2, "the next tile's weights are staged across the row tiles of the current one"
    n_ch = _stage_chunks(tn if transposed else k, n_i)
    wq_shape = (2, tn, k) if transposed else (2, k, tn)
    in_specs = [pl.BlockSpec((tm, k), lambda j, i: (i, 0)), pl.BlockSpec(memory_space=pl.ANY)]
    out_specs = [pl.BlockSpec((tm, tn), lambda j, i: (i, j))]
    out_shape = [jax.ShapeDtypeStruct((m, n), out_dtype)]
    operands = [a, w]
    if side is not None:
        _, rows, cols = side.shape
        steps = (n // tn) * n_i
        sr = rows // steps
        assert rows % steps == 0 and sr % ROW_ALIGN == 0
        in_specs.append(pl.BlockSpec((None, sr, cols), lambda j, i: (layer, j * n_i + i, 0)))
        out_specs.append(pl.BlockSpec((sr, cols), lambda j, i: (j * n_i + i, 0)))
        out_shape.append(jax.ShapeDtypeStruct((rows, cols), BF16))
        operands.append(side)
    out = pl.pallas_call(
        functools.partial(_matmul_ws_kernel, relu2=relu2, transposed=transposed, layer=layer,
                          row0=col0, n_ch=n_ch, split=split, gap=gap, side=side is not None),
        grid=(n // tn, n_i),
        in_specs=in_specs, out_specs=out_specs, out_shape=out_shape,
        scratch_shapes=[pltpu.VMEM(wq_shape, BF16),
                        pltpu.VMEM((2, wq_shape[1] // n_ch, wq_shape[2]), F32),
                        pltpu.SemaphoreType.DMA((2,))],
        compiler_params=_params("arbitrary", "arbitrary"), name=name,
    )(*operands)
    return out if side is not None else out[0]


def _up_merge_kernel(yr_ref, yn_ref, gr_ref, gn_ref, wr_hbm, wn_hbm, o_ref, wq_ref, stage_ref, sem,
                     *, layer, n_ch):
    j, i = pl.program_id(0), pl.program_id(1)
    ch = stage_ref.shape[1]
    tile = wq_ref.shape[2]
    kr = yr_ref.shape[1]
    half = kr // ch

    def chunk_copy(w_hbm, col, c, slot):
        src = w_hbm.at[layer, pl.ds(pl.multiple_of(c * ch, ROW_ALIGN), ch), pl.ds(col * tile, tile)]
        return pltpu.make_async_copy(src, stage_ref.at[slot], sem.at[slot])

    def both(method):
        def go(col, c, slot):
            if isinstance(c, int):
                w_hbm, cc = (wr_hbm, c) if c < half else (wn_hbm, c - half)
                getattr(chunk_copy(w_hbm, col, cc, slot), method)()
            else:
                pl.when(c < half)(lambda: getattr(chunk_copy(wr_hbm, col, c, slot), method)())
                pl.when(c >= half)(lambda: getattr(chunk_copy(wn_hbm, col, c - half, slot), method)())
        return go

    _stage_weights(j, i, n_ch, wq_ref, stage_ref, both("start"), both("wait"))

    for buf in range(2):
        @pl.when(lax.rem(j, 2) == buf)
        def _():
            a = _dot(yr_ref[...], wq_ref[buf, 0:kr, :])
            b = _dot(yn_ref[...], wq_ref[buf, kr:, :])
            o_ref[...] = (gr_ref[...].astype(F32) * a + gn_ref[...].astype(F32) * b).astype(o_ref.dtype)


def _up_merge(y_ret, y_nsa, gates, g_col, w_ru, w_nu, layer, tm=1024, tn=1024):
    m, kr = y_ret.shape
    kn = y_nsa.shape[1]
    n = w_ru.shape[2]
    tm, tn = min(tm, m), min(tn, n)
    while m // tm < 3 and tm % (2 * ROW_ALIGN) == 0:
        tm //= 2
    n_i, n_j = m // tm, n // tn
    assert m % tm == 0 and n % tn == 0 and n_i >= 3 and kr == kn and g_col % tn == 0
    g_blk = g_col // tn
    n_ch = 2 * _stage_chunks(kr, (n_i + 1) // 2)
    ch = (kr + kn) // n_ch
    return pl.pallas_call(
        functools.partial(_up_merge_kernel, layer=layer, n_ch=n_ch),
        grid=(n_j, n_i),
        in_specs=[pl.BlockSpec((tm, kr), lambda j, i: (i, 0)),
                  pl.BlockSpec((tm, kn), lambda j, i: (i, 0)),
                  pl.BlockSpec((tm, tn), lambda j, i: (i, g_blk + j)),
                  pl.BlockSpec((tm, tn), lambda j, i: (i, g_blk + n_j + j)),
                  pl.BlockSpec(memory_space=pl.ANY), pl.BlockSpec(memory_space=pl.ANY)],
        out_specs=pl.BlockSpec((tm, tn), lambda j, i: (i, j)),
        out_shape=jax.ShapeDtypeStruct((m, n), BF16),
        scratch_shapes=[pltpu.VMEM((2, kr + kn, tn), BF16), pltpu.VMEM((2, ch, tn), F32),
                        pltpu.SemaphoreType.DMA((2,))],
        compiler_params=_params("arbitrary", "arbitrary"), name="up_merge",
    )(y_ret, y_nsa, gates, gates, w_ru, w_nu)


def _matmul_ktiled_kernel(a_ref, w_ref, o_ref, acc_ref):
    kk = pl.program_id(2)

    @pl.when(kk == 0)
    def _():
        acc_ref[...] = jnp.zeros_like(acc_ref)

    acc_ref[...] += _dot(a_ref[...], w_ref[...])

    @pl.when(kk == pl.num_programs(2) - 1)
    def _():
        o_ref[...] = acc_ref[...].astype(o_ref.dtype)


def _matmul_ktiled(a, w, out_dtype, tm=2048, tn=1024, tk=2048, name="matmul_k"):
    m, k = a.shape
    n = w.shape[1]
    tm, tn, tk = min(tm, m), min(tn, n), min(tk, k)
    return pl.pallas_call(
        _matmul_ktiled_kernel,
        grid=(m // tm, n // tn, k // tk),
        in_specs=[pl.BlockSpec((tm, tk), lambda i, j, kk: (i, kk)),
                  pl.BlockSpec((tk, tn), lambda i, j, kk: (kk, j))],
        out_specs=pl.BlockSpec((tm, tn), lambda i, j, kk: (i, j)),
        out_shape=jax.ShapeDtypeStruct((m, n), out_dtype),
        scratch_shapes=[pltpu.VMEM((tm, tn), F32)],
        compiler_params=_params("parallel", "parallel", "arbitrary"), name=name,
    )(a, w)


def _retention_kernel(q_ref, k_ref, v_ref, g_ref, di_ref, dq_ref, dk_ref, dc_ref, o_ref,
                      state_ref, *, n_chunks, chunk, k_scale):
    state_ref[...] = jnp.zeros_like(state_ref)
    hp = state_ref.shape[0]
    dk = state_ref.shape[1]

    def body(n, carry):
        rows = pl.ds(pl.multiple_of(n * chunk, chunk), chunk)
        for hh in range(hp):
            cols = slice(hh * dk, (hh + 1) * dk)
            q = q_ref[rows, cols]
            kf = k_ref[rows, cols].astype(F32) * k_scale
            v = v_ref[rows, cols]
            s = _dot_nt(q, kf.astype(BF16)) * di_ref[hh]
            state = state_ref[hh]
            o = _dot(s.astype(BF16), v) + _dot(q, state.astype(BF16)) * dq_ref[hh]
            state_ref[hh] = state * dc_ref[hh, 0:1, :] + _dot_tn((kf * dk_ref[hh]).astype(BF16), v)
            mu = jnp.mean(o, axis=-1, keepdims=True)
            oc = o - mu
            var = jnp.mean(oc * oc, axis=-1, keepdims=True)
            gate = g_ref[rows, cols].astype(F32)
            y = gate * jax.nn.sigmoid(gate) * (oc * lax.rsqrt(var + EPS))
            o_ref[rows, cols] = y.astype(o_ref.dtype)
        return carry

    lax.fori_loop(0, n_chunks, body, 0, unroll=4)


def _retention(proj, batch, seq, dk, hp=4):
    heads, chunk = RET_HEADS, RET_CHUNK
    assert heads % hp == 0
    dv = dk
    n_chunks = seq // chunk
    log_gamma = jnp.log1p(-jnp.exp2(-5.0 - jnp.arange(heads, dtype=F32)))
    pos = jnp.arange(chunk, dtype=F32)
    rel = pos[:, None] - pos[None, :]
    d_intra = jnp.where(rel >= 0, jnp.exp(log_gamma[:, None, None] * jnp.maximum(rel, 0.0)), 0.0)
    d_query = jnp.broadcast_to(jnp.exp(log_gamma[:, None] * (pos + 1.0))[..., None], (heads, chunk, dv))
    d_key = jnp.broadcast_to(jnp.exp(log_gamma[:, None] * (chunk - 1.0 - pos))[..., None], (heads, chunk, dk))
    d_chunk = jnp.broadcast_to(jnp.exp(log_gamma * chunk)[:, None, None], (heads, 8, dv))

    col = lambda off: pl.BlockSpec((seq, hp * dk), lambda b, h: (b, off // hp + h))
    tab = lambda r, c: pl.BlockSpec((hp, r, c), lambda b, h: (h, 0, 0))
    return pl.pallas_call(
        functools.partial(_retention_kernel, n_chunks=n_chunks, chunk=chunk, k_scale=dk ** -0.5),
        grid=(batch, heads // hp),
        in_specs=[col(0), col(heads), col(2 * heads), col(3 * heads),
                  tab(chunk, chunk), tab(chunk, dv), tab(chunk, dk), tab(8, dv)],
        out_specs=pl.BlockSpec((seq, hp * dv), lambda b, h: (b, h)),
        out_shape=jax.ShapeDtypeStruct((batch * seq, heads * dv), BF16),
        scratch_shapes=[pltpu.VMEM((hp, dk, dv), F32)],
        compiler_params=_params("parallel", "parallel"), name="retention",
    )(proj, proj, proj, proj, d_intra, d_query, d_key, d_chunk)


def _compress_kernel(x_ref, pos_ref, w1_ref, w2_ref, o_ref, x32_ref, *, n_cmp, n_chunks):
    dh = NSA_DH
    half = CMP_BLOCK // 2
    x32_ref[...] = x_ref[...].astype(F32)
    lo = jnp.zeros((n_chunks, w1_ref.shape[1]), F32)
    hi = jnp.zeros((n_chunks, w1_ref.shape[1]), F32)
    for l in range(half):
        xl = x32_ref[pl.ds(l, n_chunks, stride=CMP_STRIDE), :]
        a = (xl + pos_ref[l:l + 1, :]).astype(BF16)
        b = (xl + pos_ref[half + l:half + l + 1, :]).astype(BF16)
        lo = lo + _dot(a, w1_ref[l * dh:(l + 1) * dh, :])
        hi = hi + _dot(b, w1_ref[(half + l) * dh:(half + l + 1) * dh, :])
    pre = lo + pltpu.roll(hi, n_chunks - 1, axis=0)
    hid = pre * jax.nn.sigmoid(pre)
    out = _dot(hid.astype(BF16), w2_ref[...])
    row = lax.broadcasted_iota(jnp.int32, out.shape, 0)
    o_ref[...] = jnp.where(row < n_cmp, out, 0.0).astype(o_ref.dtype)


def _compress(proj, col_blk, pos, w1, w2, batch, seq):
    kvh, dh = NSA_KV_HEADS, NSA_DH
    n_chunks = seq // CMP_STRIDE
    n_cmp = (seq - CMP_BLOCK) // CMP_STRIDE + 1
    assert CMP_BLOCK == 2 * CMP_STRIDE and n_cmp == n_chunks - 1
    hidden = w1.shape[-1]
    return pl.pallas_call(
        functools.partial(_compress_kernel, n_cmp=n_cmp, n_chunks=n_chunks),
        grid=(2, batch, kvh),
        in_specs=[pl.BlockSpec((seq, dh), lambda c, b, g: (b, col_blk + kvh * c + g)),
                  pl.BlockSpec((None, CMP_BLOCK, dh), lambda c, b, g: (c, 0, 0)),
                  pl.BlockSpec((None, CMP_BLOCK * dh, hidden), lambda c, b, g: (c, 0, 0)),
                  pl.BlockSpec((None, hidden, dh), lambda c, b, g: (c, 0, 0))],
        out_specs=pl.BlockSpec((None, None, None, n_chunks, dh), lambda c, b, g: (c, b, g, 0, 0)),
        out_shape=jax.ShapeDtypeStruct((2, batch, kvh, n_chunks, dh), BF16),
        scratch_shapes=[pltpu.VMEM((seq, dh), F32)],
        compiler_params=_params("parallel", "parallel", "parallel"), name="nsa_compress",
    )(proj, pos, w1, w2)


def _nsa_kernel(q_ref, ks_ref, vs_ref, kw_ref, vw_ref, kc_ref, vc_ref, gl_ref, slope_ref, qx_ref,
                ovt_ref, kxs_ref, kxw_ref, band_ref, o_ref, osel_ref, gate_ref,
                *, tq, tk, tw, n_cmp, n_sel, k_top):
    dh = NSA_DH
    grp = NSA_GROUP
    sc = (dh ** -0.5) * LOG2E
    t0 = pl.program_id(2) * tq
    q = (q_ref[...].astype(F32) * sc).astype(BF16)
    q4 = jnp.concatenate([q[:, r * dh:(r + 1) * dh] for r in range(grp)], axis=0)
    slope_row = slope_ref[0:1, :]
    tile4 = lambda a: jnp.concatenate([a] * grp, axis=1)

    kc = kc_ref[...]
    vc = vc_ref[...]
    nc = kc.shape[0]
    n_idx = lax.broadcasted_iota(jnp.int32, (nc, tq), 0)
    t_idx = t0 + lax.broadcasted_iota(jnp.int32, (nc, tq), 1)
    dist_c = t_idx - (CMP_STRIDE * n_idx + (CMP_BLOCK - 1))
    dist_c = jnp.where(n_idx < n_cmp, dist_c, -1)
    dist_c = tile4(jnp.where(dist_c >= 0, dist_c.astype(F32), MASK_DIST))
    s = _dot_nt(kc, q4) - slope_row * dist_c
    m = jnp.max(s, axis=0, keepdims=True)
    e = jnp.where(dist_c < MASK_DIST, jnp.exp2(s - m), 0.0)
    l = jnp.sum(e, axis=0, keepdims=True)
    p = e / jnp.where(l > 0.0, l, 1.0)
    o_cmp = _dot_tn(vc, p.astype(BF16))
    p_sum = p[:, 0:tq]
    for r in range(1, grp):
        p_sum = p_sum + p[:, r * tq:(r + 1) * tq]

    p_hi = p_sum.astype(BF16)
    p_lo = (p_sum - p_hi.astype(F32)).astype(BF16)
    ovt = ovt_ref[...]
    imp = _dot(ovt, p_hi) + _dot(ovt, p_lo)
    blk = lax.broadcasted_iota(jnp.int32, (n_sel, tq), 0)
    t_sel = t0 + lax.broadcasted_iota(jnp.int32, (n_sel, tq), 1)
    back = jnp.right_shift(t_sel, SEL_SHIFT) - blk
    forced = jnp.where(blk == 0, 0, jnp.where(back < 0, SEL_LOCAL, back)) < SEL_LOCAL
    imp = jnp.where(forced, FORCED_SCORE, jnp.where(blk * SEL_BLOCK > t_sel, -1.0, imp))
    rank = jnp.zeros((n_sel, tq), F32)
    for j in range(n_sel):
        cand = imp[j:j + 1, :]
        tie = jnp.where(blk > j, 1.0, 0.0)
        rank = rank + jnp.where(cand > imp, 1.0, jnp.where(cand == imp, tie, 0.0))
    unsel = jnp.where(rank < k_top, 0.0, -MASK_POW2)
    unsel = jnp.concatenate([jnp.zeros((SEL_LANE0, tq), F32), unsel,
                             jnp.zeros((LANES - SEL_LANE0 - n_sel, tq), F32)], axis=0).T
    q_aug = jnp.concatenate(
        [q4, jnp.concatenate([(unsel + qx_ref[r:r + 1, :]).astype(BF16) for r in range(grp)], axis=0)],
        axis=1)

    span = WINDOW + tw
    o_parts = []
    for u in range(tq // tw):
        tu = t0 + u * tw
        case = jnp.minimum(tu // tw, WINDOW // tw)
        w0 = pl.multiple_of(jnp.maximum(tu - WINDOW, 0), tw)
        q_u = jnp.concatenate([q_aug[r * tq + u * tw:r * tq + (u + 1) * tw, :] for r in range(grp)], axis=0)
        k_aug = jnp.concatenate([kw_ref[pl.ds(w0, span), :], kxw_ref[pl.ds(w0, span), :]], axis=1)
        s = _dot_nt(k_aug, q_u) + jnp.concatenate([band_ref[case]] * grp, axis=1)
        p = jnp.exp2(s - jnp.max(s, axis=0, keepdims=True))
        o_parts.append(_dot_tn(vw_ref[pl.ds(w0, span), :], p.astype(BF16))
                       / jnp.sum(p, axis=0, keepdims=True))
    o_win = jnp.concatenate([o_parts[u][:, r * tw:(r + 1) * tw]
                             for r in range(grp) for u in range(tq // tw)], axis=1)

    def sel_branch(nk):
        n_chunks = nk // tk

        def scores(c):
            rows = slice(c * tk, (c + 1) * tk)
            k_aug = jnp.concatenate([ks_ref[rows, :], kxs_ref[rows, :]], axis=1)
            s = _dot_nt(k_aug, q_aug)
            if c == n_chunks - 1:
                ahead = (lax.broadcasted_iota(jnp.int32, (tk, tq), 0) + (c * tk - t0)
                         > lax.broadcasted_iota(jnp.int32, (tk, tq), 1))
                s = s + tile4(jnp.where(ahead, NEG_INF, 0.0))
            return s

        m = l = acc = None
        s_next = scores(0)
        for c in range(n_chunks):
            rows = slice(c * tk, (c + 1) * tk)
            s = s_next
            if c + 1 < n_chunks:
                s_next = scores(c + 1)
            m_c = jnp.max(s, axis=0, keepdims=True)
            m_new = m_c if c == 0 else jnp.maximum(m, m_c)
            p = jnp.exp2(s - m_new)
            l_c = jnp.sum(p, axis=0, keepdims=True)
            pv = _dot_tn(vs_ref[rows, :], p.astype(BF16))
            if c == 0:
                l, acc = l_c, pv
            else:
                alpha = jnp.exp2(m - m_new)
                l, acc = alpha * l + l_c, alpha * acc + pv
            m = m_new
        return acc / l

    n_kt = (t0 + tq + tk - 1) // tk
    for n in range(1, ks_ref.shape[0] // tk + 1):
        @pl.when(n_kt == n)
        def _():
            osel_ref[...] = sel_branch(n * tk)

    o_sel = osel_ref[...]

    gate_ref[...] = jax.nn.sigmoid(gl_ref[...].T)
    row0 = 3 * grp * pl.program_id(1)
    for r in range(grp):
        cols = slice(r * tq, (r + 1) * tq)
        out = (gate_ref[pl.ds(row0 + 3 * r, 1), :] * o_cmp[:, cols]
               + gate_ref[pl.ds(row0 + 3 * r + 1, 1), :] * o_sel[:, cols]
               + gate_ref[pl.ds(row0 + 3 * r + 2, 1), :] * o_win[:, cols])
        o_ref[:, r * dh:(r + 1) * dh] = out.T.astype(o_ref.dtype)


def _nsa(proj, q_blk, kv_blk, cmp_kv, gate_logits, batch, seq, tq=512, tk=512, tw=256):
    dh, kvh, grp = NSA_DH, NSA_KV_HEADS, NSA_GROUP
    n_cmp = (seq - CMP_BLOCK) // CMP_STRIDE + 1
    nc = cmp_kv.shape[3]
    n_sel = seq // SEL_BLOCK
    k_top = min(SEL_TOPK, n_sel)
    tw = min(tw, tq)
    span = WINDOW + tw
    n_case = WINDOW // tw + 1
    assert seq % tk == 0 and seq >= span and WINDOW % tw == 0 and tq % tw == 0
    assert tk % tq == 0 and tk % SEL_BLOCK == 0
    assert SEL_LANE0 + n_sel <= LANES and seq <= 256 * 256 and dh == LANES

    slopes = jnp.exp2(-8.0 * (jnp.arange(NSA_HEADS, dtype=F32) + 1.0) / NSA_HEADS).reshape(kvh, grp)
    slope_tab = jnp.broadcast_to(jnp.repeat(slopes * LOG2E, tq, axis=1)[:, None, :], (kvh, 8, grp * tq))
    pieces, rest = [], slopes * LOG2E
    for _ in range(SLOPE_PIECES):
        piece = rest.astype(BF16).astype(F32)
        pieces += [piece, piece]
        rest = rest - piece
    qx = jnp.zeros((kvh, 8, LANES), F32).at[:, :grp, :2 * SLOPE_PIECES].set(jnp.stack(pieces, axis=-1))
    key = jnp.arange(seq)
    key_cols = jnp.stack([(key >> 8) * 256, key & 255] * SLOPE_PIECES, axis=1).astype(F32)
    kxw = jnp.zeros((seq, LANES), F32).at[:, :2 * SLOPE_PIECES].set(key_cols)
    kxs = kxw.at[key, SEL_LANE0 + key // SEL_BLOCK].set(1.0)
    kxs, kxw = kxs.astype(BF16), kxw.astype(BF16)
    cmp_start = CMP_STRIDE * jnp.arange(n_cmp)
    sel_start = SEL_BLOCK * jnp.arange(n_sel)
    overlap = jnp.clip(jnp.minimum(cmp_start[:, None] + CMP_BLOCK, sel_start[None, :] + SEL_BLOCK)
                       - jnp.maximum(cmp_start[:, None], sel_start[None, :]), 0).astype(F32) / CMP_BLOCK
    ovt = jnp.zeros((n_sel, nc), F32).at[:, :n_cmp].set(overlap.T).astype(BF16)
    dist_w = (jnp.arange(n_case)[:, None, None] * tw + jnp.arange(tw)[None, None, :]
              - jnp.arange(span)[None, :, None])
    band = jnp.where((dist_w >= 0) & (dist_w < WINDOW), 0.0, NEG_INF).astype(F32)

    nq = seq // tq
    kv_spec = lambda c: pl.BlockSpec((seq, dh), lambda b, g, i: (b, kv_blk + kvh * c + g))
    cmp_spec = lambda c: pl.BlockSpec((None, None, None, nc, dh), lambda b, g, i: (c, b, g, 0, 0))
    whole = lambda a: pl.BlockSpec(a.shape, lambda b, g, i: (0,) * a.ndim)
    return pl.pallas_call(
        functools.partial(_nsa_kernel, tq=tq, tk=tk, tw=tw, n_cmp=n_cmp, n_sel=n_sel, k_top=k_top),
        grid=(batch, kvh, nq),
        in_specs=[pl.BlockSpec((tq, grp * dh), lambda b, g, i: (b * nq + i, q_blk + g)),
                  kv_spec(0), kv_spec(1), kv_spec(2), kv_spec(3), cmp_spec(0), cmp_spec(1),
                  pl.BlockSpec((tq, LANES), lambda b, g, i: (b * nq + i, 0)),
                  pl.BlockSpec((None, 8, grp * tq), lambda b, g, i: (g, 0, 0)),
                  pl.BlockSpec((None, 8, LANES), lambda b, g, i: (g, 0, 0)),
                  whole(ovt), whole(kxs), whole(kxw), whole(band)],
        out_specs=pl.BlockSpec((tq, grp * dh), lambda b, g, i: (b * nq + i, g)),
        out_shape=jax.ShapeDtypeStruct((batch * seq, kvh * grp * dh), BF16),
        scratch_shapes=[pltpu.VMEM((dh, grp * tq), F32), pltpu.VMEM((LANES, tq), F32)],
        compiler_params=_params("parallel", "parallel", "arbitrary"), name="nsa_attention",
    )(proj, proj, proj, proj, proj, cmp_kv, cmp_kv, gate_logits, slope_tab, qx, ovt, kxs, kxw, band)


def kernel(x, mix_norm_pre, w_in, cmp_pos_k, cmp_w1_k, cmp_w2_k, cmp_pos_v, cmp_w1_v, cmp_w2_v,
           w_ret_up, w_nsa_up, w_out, mix_norm_post, mlp_norm_pre, w_mlp_in, w_mlp_out,
           mlp_norm_post):
    batch, seq, d = x.shape
    depth = w_in.shape[0]
    ret_w = w_ret_up.shape[1]
    nsa_w = w_nsa_up.shape[1]
    kvh, dh = NSA_KV_HEADS, NSA_DH
    kv_w = kvh * dh
    n_gate = 3 * NSA_HEADS
    c_qn = 4 * ret_w
    c_kv = c_qn + nsa_w
    c_gate = c_kv + 6 * kv_w
    c_gr = c_gate + n_gate

    w_in_t = jnp.swapaxes(w_in, 1, 2)
    assert n_gate <= LANES
    w_gate_t = jnp.pad(w_in_t[:, c_gate:c_gr, :], ((0, 0), (0, LANES - n_gate), (0, 0))).astype(BF16)
    cmp_pos = jnp.stack([cmp_pos_k, cmp_pos_v], axis=1)
    cmp_w1 = jnp.stack([cmp_w1_k, cmp_w1_v], axis=1).astype(BF16)
    cmp_w2 = jnp.stack([cmp_w2_k, cmp_w2_v], axis=1).astype(BF16)

    xs = x.reshape(batch * seq, d)
    h = _rmsnorm(xs, mix_norm_pre[0])
    for l in range(depth):
        tn_in = math.gcd(c_gate, 1024)
        proj = _matmul_ws(h, w_in_t, l, 0, c_gate + 2 * d, BF16, tn=tn_in, transposed=True,
                          split=c_gate // tn_in, gap=n_gate, name="proj_in")
        gate_logits = _matmul_nt(h, w_gate_t[l], F32, name="proj_nsa_gate")
        y_ret = _retention(proj, batch, seq, ret_w // RET_HEADS)
        cmp_kv = _compress(proj, c_kv // dh, cmp_pos[l], cmp_w1[l], cmp_w2[l], batch, seq)
        y_nsa = _nsa(proj, c_qn // (NSA_GROUP * dh), (c_kv + 2 * kv_w) // dh, cmp_kv, gate_logits,
                     batch, seq)
        merged = _up_merge(y_ret, y_nsa, proj, c_gate, w_ret_up, w_nsa_up, l)
        mix = _matmul_ws(merged, w_out, l, 0, d, BF16, name="proj_out")
        xs, h = _post(mix, xs, mix_norm_post[l], mlp_norm_pre[l])
        u, w_mo = _matmul_ws(h, w_mlp_in, l, 0, w_mlp_in.shape[2], BF16, relu2=True, side=w_mlp_out,
                             name="mlp_in")
        y = _matmul_ktiled(u, w_mo, BF16, name="mlp_out")
        xs, h = _post(y, xs, mlp_norm_post[l], mix_norm_pre[l + 1] if l + 1 < depth else None)
    return xs.reshape(batch, seq, d)
```

```python
import functools
import math

import jax
import jax.numpy as jnp
from jax import lax
from jax.experimental import pallas as pl
from jax.experimental.pallas import tpu as pltpu

RET_HEADS = 8
RET_CHUNK = 128
NSA_HEADS = 16
NSA_KV_HEADS = 4
NSA_DH = 128
CMP_BLOCK = 32
CMP_STRIDE = 16
SEL_BLOCK = 64
SEL_TOPK = 16
SEL_LOCAL = 2
WINDOW = 512
EPS = 1e-6
NEG_INF = -1e30
FORCED_SCORE = 1e4

NSA_GROUP = NSA_HEADS // NSA_KV_HEADS
SEL_SHIFT = SEL_BLOCK.bit_length() - 1
LANES = 128
ROW_ALIGN = 16
VMEM_LIMIT = 56 * 1024 * 1024
MAX_STAGE_CHUNKS = 8
LOG2E = math.log2(math.e)
MASK_DIST = 1e32
MASK_POW2 = 2.0 ** 100
SLOPE_PIECES = 3
SEL_LANE0 = 32

F32 = jnp.float32
BF16 = jnp.bfloat16


def _params(*semantics):
    return pltpu.CompilerParams(dimension_semantics=semantics, vmem_limit_bytes=VMEM_LIMIT)


def _dot(a, b):
    return jnp.dot(a, b, preferred_element_type=F32)


def _dot_nt(a, b):
    return lax.dot_general(a, b, (((1,), (1,)), ((), ())), preferred_element_type=F32)


def _dot_tn(a, b):
    return lax.dot_general(a, b, (((0,), (0,)), ((), ())), preferred_element_type=F32)


def _rmsnorm_kernel(x_ref, g_ref, o_ref):
    x = x_ref[...]
    ms = jnp.mean(x * x, axis=-1, keepdims=True)
    o_ref[...] = (x * lax.rsqrt(ms + EPS) * g_ref[...]).astype(o_ref.dtype)


def _rmsnorm(x, gain, tm=256):
    m, d = x.shape
    return pl.pallas_call(
        _rmsnorm_kernel,
        grid=(m // tm,),
        in_specs=[pl.BlockSpec((tm, d), lambda i: (i, 0)),
                  pl.BlockSpec((1, d), lambda i: (0, 0))],
        out_specs=pl.BlockSpec((tm, d), lambda i: (i, 0)),
        out_shape=jax.ShapeDtypeStruct((m, d), BF16),
        compiler_params=_params("parallel"),
        name="rmsnorm",
    )(x, gain.reshape(1, d))


def _post_kernel(y_ref, x_ref, gp_ref, gn_ref, xo_ref, ho_ref):
    y = y_ref[...].astype(F32)
    yn = y * lax.rsqrt(jnp.mean(y * y, axis=-1, keepdims=True) + EPS) * gp_ref[...]
    xn = x_ref[...] + yn
    xo_ref[...] = xn
    ho_ref[...] = (xn * lax.rsqrt(jnp.mean(xn * xn, axis=-1, keepdims=True) + EPS)
                   * gn_ref[...]).astype(ho_ref.dtype)


def _post_last_kernel(y_ref, x_ref, gp_ref, xo_ref):
    y = y_ref[...].astype(F32)
    yn = y * lax.rsqrt(jnp.mean(y * y, axis=-1, keepdims=True) + EPS) * gp_ref[...]
    xo_ref[...] = x_ref[...] + yn


def _post(y, x, g_post, g_next, tm=256):
    m, d = x.shape
    row = pl.BlockSpec((tm, d), lambda i: (i, 0))
    vec = pl.BlockSpec((1, d), lambda i: (0, 0))
    if g_next is None:
        return pl.pallas_call(
            _post_last_kernel, grid=(m // tm,),
            in_specs=[row, row, vec], out_specs=row,
            out_shape=jax.ShapeDtypeStruct((m, d), F32),
            compiler_params=_params("parallel"), name="post_last",
        )(y, x, g_post.reshape(1, d)), None
    return pl.pallas_call(
        _post_kernel, grid=(m // tm,),
        in_specs=[row, row, vec, vec], out_specs=[row, row],
        out_shape=[jax.ShapeDtypeStruct((m, d), F32), jax.ShapeDtypeStruct((m, d), BF16)],
        compiler_params=_params("parallel"), name="post",
    )(y, x, g_post.reshape(1, d), g_next.reshape(1, d))


def _matmul_nt_kernel(a_ref, b_ref, o_ref):
    o_ref[...] = _dot_nt(a_ref[...], b_ref[...]).astype(o_ref.dtype)


def _matmul_nt(a, b_t, out_dtype, tm=1024, tn=1024, name="matmul"):
    m, k = a.shape
    n = b_t.shape[0]
    tm, tn = min(tm, m), min(tn, n)
    return pl.pallas_call(
        _matmul_nt_kernel,
        grid=(m // tm, n // tn),
        in_specs=[pl.BlockSpec((tm, k), lambda i, j: (i, 0)),
                  pl.BlockSpec((tn, k), lambda i, j: (j, 0))],
        out_specs=pl.BlockSpec((tm, tn), lambda i, j: (i, j)),
        out_shape=jax.ShapeDtypeStruct((m, n), out_dtype),
        compiler_params=_params("parallel", "arbitrary"), name=name,
    )(a, b_t)


def _stage_weights(j, i, n_ch, wq_ref, stage_ref, start, wait):
    ch = stage_ref.shape[1]

    def convert(buf, c, slot):
        wq_ref[buf, pl.ds(pl.multiple_of(c * ch, ch), ch), :] = stage_ref[slot].astype(BF16)

    @pl.when((j == 0) & (i == 0))
    def _():
        start(0, 0, 0)
        if n_ch > 1:
            start(0, 1, 1)
        for c in range(n_ch):
            wait(0, c, c % 2)
            convert(0, c, c % 2)
            if c + 2 < n_ch:
                start(0, c + 2, c % 2)

    has_next = j + 1 < pl.num_programs(0)

    @pl.when(has_next & (i >= 1) & (i <= n_ch))
    def _():
        slot = lax.rem(i - 1, 2)
        wait(j + 1, i - 1, slot)
        convert(lax.rem(j + 1, 2), i - 1, slot)

    @pl.when(has_next & (i < n_ch))
    def _():
        start(j + 1, i, lax.rem(i, 2))


def _stage_chunks(rows, n_i):
    n_ch = min(MAX_STAGE_CHUNKS, n_i - 1)
    while rows % n_ch or (rows // n_ch) % ROW_ALIGN:
        n_ch -= 1
    return n_ch


def _matmul_ws_kernel(*refs, relu2, transposed, layer, row0, n_ch, split, gap, side, col_scale):
    if side:
        a_ref, w_hbm, side_ref, o_ref, side_o_ref, wq_ref, stage_ref, sem = refs
        side_o_ref[...] = side_ref[...].astype(side_o_ref.dtype)
    else:
        a_ref, w_hbm, o_ref, wq_ref, stage_ref, sem = refs
    j, i = pl.program_id(0), pl.program_id(1)
    ch = stage_ref.shape[1]
    tile = wq_ref.shape[1] if transposed else wq_ref.shape[2]

    def chunk_copy(col, c, slot):
        first = row0 + col * tile
        if split is not None:
            first = first + jnp.where(col >= split, gap, 0)
        if transposed:
            src = w_hbm.at[layer, pl.ds(pl.multiple_of(first + c * ch, ROW_ALIGN), ch), :]
        else:
            src = w_hbm.at[layer, pl.ds(c * ch, ch), pl.ds(first, tile)]
        return pltpu.make_async_copy(src, stage_ref.at[slot], sem.at[slot])

    _stage_weights(j, i, n_ch, wq_ref, stage_ref,
                   lambda col, c, slot: chunk_copy(col, c, slot).start(),
                   lambda col, c, slot: chunk_copy(col, c, slot).wait())

    def compute(buf, gate):
        wq = wq_ref[buf]
        acc = _dot_nt(a_ref[...], wq) if transposed else _dot(a_ref[...], wq)
        if relu2:
            acc = jnp.square(jnp.maximum(acc, 0.0))
        if gate:
            acc = jax.nn.sigmoid(acc)
        elif col_scale is not None:
            lo, hi, factor = col_scale
            acc = acc * jnp.where((j >= lo) & (j < hi), factor, 1.0).astype(F32)
        o_ref[...] = acc.astype(o_ref.dtype)

    for buf in range(2):
        if split is None:
            pl.when(lax.rem(j, 2) == buf)(functools.partial(compute, buf, False))
        else:
            pl.when((lax.rem(j, 2) == buf) & (j < split))(functools.partial(compute, buf, False))
            pl.when((lax.rem(j, 2) == buf) & (j >= split))(functools.partial(compute, buf, True))


def _matmul_ws(a, w, layer, col0, n, out_dtype, tm=1024, tn=1024, relu2=False, transposed=False,
               split=None, gap=0, side=None, col_scale=None, name="matmul_ws"):
    m, k = a.shape
    tm, tn = min(tm, m), min(tn, n)
    assert n % tn == 0 and m % tm == 0 and col0 % ROW_ALIGN == 0 and gap % ROW_ALIGN == 0
    n_i = m // tm
    assert n_i >= 2, "the next tile's weights are staged across the row tiles of the current one"
    n_ch = _stage_chunks(tn if transposed else k, n_i)
    wq_shape = (2, tn, k) if transposed else (2, k, tn)
    in_specs = [pl.BlockSpec((tm, k), lambda j, i: (i, 0)), pl.BlockSpec(memory_space=pl.ANY)]
    out_specs = [pl.BlockSpec((tm, tn), lambda j, i: (i, j))]
    out_shape = [jax.ShapeDtypeStruct((m, n), out_dtype)]
    operands = [a, w]
    if side is not None:
        _, rows, cols = side.shape
        steps = (n // tn) * n_i
        sr = rows // steps
        assert rows % steps == 0 and sr % ROW_ALIGN == 0
        in_specs.append(pl.BlockSpec((None, sr, cols), lambda j, i: (layer, j * n_i + i, 0)))
        out_specs.append(pl.BlockSpec((sr, cols), lambda j, i: (j * n_i + i, 0)))
        out_shape.append(jax.ShapeDtypeStruct((rows, cols), BF16))
        operands.append(side)
    out = pl.pallas_call(
        functools.partial(_matmul_ws_kernel, relu2=relu2, transposed=transposed, layer=layer,
                          row0=col0, n_ch=n_ch, split=split, gap=gap, side=side is not None,
                          col_scale=col_scale),
        grid=(n // tn, n_i),
        in_specs=in_specs, out_specs=out_specs, out_shape=out_shape,
        scratch_shapes=[pltpu.VMEM(wq_shape, BF16),
                        pltpu.VMEM((2, wq_shape[1] // n_ch, wq_shape[2]), F32),
                        pltpu.SemaphoreType.DMA((2,))],
        compiler_params=_params("arbitrary", "arbitrary"), name=name,
    )(*operands)
    return out if side is not None else out[0]


def _up_merge_kernel(yr_ref, yn_ref, gr_ref, gn_ref, wr_hbm, wn_hbm, o_ref, wq_ref, stage_ref, sem,
                     *, layer, n_ch):
    j, i = pl.program_id(0), pl.program_id(1)
    ch = stage_ref.shape[1]
    tile = wq_ref.shape[2]
    kr = yr_ref.shape[1]
    half = kr // ch

    def chunk_copy(w_hbm, col, c, slot):
        src = w_hbm.at[layer, pl.ds(pl.multiple_of(c * ch, ROW_ALIGN), ch), pl.ds(col * tile, tile)]
        return pltpu.make_async_copy(src, stage_ref.at[slot], sem.at[slot])

    def both(method):
        def go(col, c, slot):
            if isinstance(c, int):
                w_hbm, cc = (wr_hbm, c) if c < half else (wn_hbm, c - half)
                getattr(chunk_copy(w_hbm, col, cc, slot), method)()
            else:
                pl.when(c < half)(lambda: getattr(chunk_copy(wr_hbm, col, c, slot), method)())
                pl.when(c >= half)(lambda: getattr(chunk_copy(wn_hbm, col, c - half, slot), method)())
        return go

    _stage_weights(j, i, n_ch, wq_ref, stage_ref, both("start"), both("wait"))

    for buf in range(2):
        @pl.when(lax.rem(j, 2) == buf)
        def _():
            a = _dot(yr_ref[...], wq_ref[buf, 0:kr, :])
            b = _dot(yn_ref[...], wq_ref[buf, kr:, :])
            o_ref[...] = (gr_ref[...].astype(F32) * a + gn_ref[...].astype(F32) * b).astype(o_ref.dtype)


def _up_merge(y_ret, y_nsa, gates, g_col, w_ru, w_nu, layer, tm=1024, tn=1024):
    m, kr = y_ret.shape
    kn = y_nsa.shape[1]
    n = w_ru.shape[2]
    tm, tn = min(tm, m), min(tn, n)
    while m // tm < 3 and tm % (2 * ROW_ALIGN) == 0:
        tm //= 2
    n_i, n_j = m // tm, n // tn
    assert m % tm == 0 and n % tn == 0 and n_i >= 3 and kr == kn and g_col % tn == 0
    g_blk = g_col // tn
    n_ch = 2 * _stage_chunks(kr, (n_i + 1) // 2)
    ch = (kr + kn) // n_ch
    return pl.pallas_call(
        functools.partial(_up_merge_kernel, layer=layer, n_ch=n_ch),
        grid=(n_j, n_i),
        in_specs=[pl.BlockSpec((tm, kr), lambda j, i: (i, 0)),
                  pl.BlockSpec((tm, kn), lambda j, i: (i, 0)),
                  pl.BlockSpec((tm, tn), lambda j, i: (i, g_blk + j)),
                  pl.BlockSpec((tm, tn), lambda j, i: (i, g_blk + n_j + j)),
                  pl.BlockSpec(memory_space=pl.ANY), pl.BlockSpec(memory_space=pl.ANY)],
        out_specs=pl.BlockSpec((tm, tn), lambda j, i: (i, j)),
        out_shape=jax.ShapeDtypeStruct((m, n), BF16),
        scratch_shapes=[pltpu.VMEM((2, kr + kn, tn), BF16), pltpu.VMEM((2, ch, tn), F32),
                        pltpu.SemaphoreType.DMA((2,))],
        compiler_params=_params("arbitrary", "arbitrary"), name="up_merge",
    )(y_ret, y_nsa, gates, gates, w_ru, w_nu)


def _matmul_ktiled_kernel(a_ref, w_ref, o_ref, acc_ref):
    kk = pl.program_id(2)

    @pl.when(kk == 0)
    def _():
        acc_ref[...] = jnp.zeros_like(acc_ref)

    acc_ref[...] += _dot(a_ref[...], w_ref[...])

    @pl.when(kk == pl.num_programs(2) - 1)
    def _():
        o_ref[...] = acc_ref[...].astype(o_ref.dtype)


def _matmul_ktiled(a, w, out_dtype, tm=2048, tn=1024, tk=2048, name="matmul_k"):
    m, k = a.shape
    n = w.shape[1]
    tm, tn, tk = min(tm, m), min(tn, n), min(tk, k)
    return pl.pallas_call(
        _matmul_ktiled_kernel,
        grid=(m // tm, n // tn, k // tk),
        in_specs=[pl.BlockSpec((tm, tk), lambda i, j, kk: (i, kk)),
                  pl.BlockSpec((tk, tn), lambda i, j, kk: (kk, j))],
        out_specs=pl.BlockSpec((tm, tn), lambda i, j, kk: (i, j)),
        out_shape=jax.ShapeDtypeStruct((m, n), out_dtype),
        scratch_shapes=[pltpu.VMEM((tm, tn), F32)],
        compiler_params=_params("parallel", "parallel", "arbitrary"), name=name,
    )(a, w)


def _retention_kernel(q_ref, k_ref, v_ref, g_ref, di_ref, dq_ref, dk_ref, dc_ref, o_ref,
                      state_ref, *, n_chunks, chunk, k_scale):
    state_ref[...] = jnp.zeros_like(state_ref)
    hp = state_ref.shape[0]
    dk = state_ref.shape[1]

    def body(n, carry):
        rows = pl.ds(pl.multiple_of(n * chunk, chunk), chunk)
        for hh in range(hp):
            cols = slice(hh * dk, (hh + 1) * dk)
            q = q_ref[rows, cols]
            kf = k_ref[rows, cols].astype(F32) * k_scale
            v = v_ref[rows, cols]
            s = _dot_nt(q, kf.astype(BF16)) * di_ref[hh]
            state = state_ref[hh]
            o = _dot(s.astype(BF16), v) + _dot(q, state.astype(BF16)) * dq_ref[hh]
            state_ref[hh] = state * dc_ref[hh, 0:1, :] + _dot_tn((kf * dk_ref[hh]).astype(BF16), v)
            mu = jnp.mean(o, axis=-1, keepdims=True)
            oc = o - mu
            var = jnp.mean(oc * oc, axis=-1, keepdims=True)
            gate = g_ref[rows, cols].astype(F32)
            y = gate * jax.nn.sigmoid(gate) * (oc * lax.rsqrt(var + EPS))
            o_ref[rows, cols] = y.astype(o_ref.dtype)
        return carry

    lax.fori_loop(0, n_chunks, body, 0, unroll=4)


def _retention(proj, batch, seq, dk, hp=4):
    heads, chunk = RET_HEADS, RET_CHUNK
    assert heads % hp == 0
    dv = dk
    n_chunks = seq // chunk
    log_gamma = jnp.log1p(-jnp.exp2(-5.0 - jnp.arange(heads, dtype=F32)))
    pos = jnp.arange(chunk, dtype=F32)
    rel = pos[:, None] - pos[None, :]
    d_intra = jnp.where(rel >= 0, jnp.exp(log_gamma[:, None, None] * jnp.maximum(rel, 0.0)), 0.0)
    d_query = jnp.broadcast_to(jnp.exp(log_gamma[:, None] * (pos + 1.0))[..., None], (heads, chunk, dv))
    d_key = jnp.broadcast_to(jnp.exp(log_gamma[:, None] * (chunk - 1.0 - pos))[..., None], (heads, chunk, dk))
    d_chunk = jnp.broadcast_to(jnp.exp(log_gamma * chunk)[:, None, None], (heads, 8, dv))

    col = lambda off: pl.BlockSpec((seq, hp * dk), lambda b, h: (b, off // hp + h))
    tab = lambda r, c: pl.BlockSpec((hp, r, c), lambda b, h: (h, 0, 0))
    return pl.pallas_call(
        functools.partial(_retention_kernel, n_chunks=n_chunks, chunk=chunk, k_scale=dk ** -0.5),
        grid=(batch, heads // hp),
        in_specs=[col(0), col(heads), col(2 * heads), col(3 * heads),
                  tab(chunk, chunk), tab(chunk, dv), tab(chunk, dk), tab(8, dv)],
        out_specs=pl.BlockSpec((seq, hp * dv), lambda b, h: (b, h)),
        out_shape=jax.ShapeDtypeStruct((batch * seq, heads * dv), BF16),
        scratch_shapes=[pltpu.VMEM((hp, dk, dv), F32)],
        compiler_params=_params("parallel", "parallel"), name="retention",
    )(proj, proj, proj, proj, d_intra, d_query, d_key, d_chunk)


def _compress_kernel(x_ref, pos_ref, w1_ref, w2_ref, o_ref, x32_ref, *, n_cmp, n_chunks):
    dh = NSA_DH
    half = CMP_BLOCK // 2
    x32_ref[...] = x_ref[...].astype(F32)
    lo = jnp.zeros((n_chunks, w1_ref.shape[1]), F32)
    hi = jnp.zeros((n_chunks, w1_ref.shape[1]), F32)
    for l in range(half):
        xl = x32_ref[pl.ds(l, n_chunks, stride=CMP_STRIDE), :]
        a = (xl + pos_ref[l:l + 1, :]).astype(BF16)
        b = (xl + pos_ref[half + l:half + l + 1, :]).astype(BF16)
        lo = lo + _dot(a, w1_ref[l * dh:(l + 1) * dh, :])
        hi = hi + _dot(b, w1_ref[(half + l) * dh:(half + l + 1) * dh, :])
    pre = lo + pltpu.roll(hi, n_chunks - 1, axis=0)
    hid = pre * jax.nn.sigmoid(pre)
    out = _dot(hid.astype(BF16), w2_ref[...])
    row = lax.broadcasted_iota(jnp.int32, out.shape, 0)
    o_ref[...] = jnp.where(row < n_cmp, out, 0.0).astype(o_ref.dtype)


def _compress(proj, col_blk, pos, w1, w2, batch, seq):
    kvh, dh = NSA_KV_HEADS, NSA_DH
    n_chunks = seq // CMP_STRIDE
    n_cmp = (seq - CMP_BLOCK) // CMP_STRIDE + 1
    assert CMP_BLOCK == 2 * CMP_STRIDE and n_cmp == n_chunks - 1
    hidden = w1.shape[-1]
    return pl.pallas_call(
        functools.partial(_compress_kernel, n_cmp=n_cmp, n_chunks=n_chunks),
        grid=(2, batch, kvh),
        in_specs=[pl.BlockSpec((seq, dh), lambda c, b, g: (b, col_blk + kvh * c + g)),
                  pl.BlockSpec((None, CMP_BLOCK, dh), lambda c, b, g: (c, 0, 0)),
                  pl.BlockSpec((None, CMP_BLOCK * dh, hidden), lambda c, b, g: (c, 0, 0)),
                  pl.BlockSpec((None, hidden, dh), lambda c, b, g: (c, 0, 0))],
        out_specs=pl.BlockSpec((None, None, None, n_chunks, dh), lambda c, b, g: (c, b, g, 0, 0)),
        out_shape=jax.ShapeDtypeStruct((2, batch, kvh, n_chunks, dh), BF16),
        scratch_shapes=[pltpu.VMEM((seq, dh), F32)],
        compiler_params=_params("parallel", "parallel", "parallel"), name="nsa_compress",
    )(proj, pos, w1, w2)


def _nsa_kernel(q_ref, ks_ref, vs_ref, kw_ref, vw_ref, kc_ref, vc_ref, gl_ref, slope_ref, qx_ref,
                ovt_ref, kxs_ref, kxw_ref, band_ref, o_ref, osel_ref, gate_ref,
                *, tq, tk, tw, n_cmp, n_sel, k_top):
    dh = NSA_DH
    grp = NSA_GROUP
    t0 = pl.program_id(2) * tq
    q = q_ref[...]
    q4 = jnp.concatenate([q[:, r * dh:(r + 1) * dh] for r in range(grp)], axis=0)
    slope_row = slope_ref[0:1, :]
    tile4 = lambda a: jnp.concatenate([a] * grp, axis=1)

    kc = kc_ref[...]
    vc = vc_ref[...]
    nc = kc.shape[0]
    n_idx = lax.broadcasted_iota(jnp.int32, (nc, tq), 0)
    t_idx = t0 + lax.broadcasted_iota(jnp.int32, (nc, tq), 1)
    dist_c = t_idx - (CMP_STRIDE * n_idx + (CMP_BLOCK - 1))
    dist_c = jnp.where(n_idx < n_cmp, dist_c, -1)
    dist_c = tile4(jnp.where(dist_c >= 0, dist_c.astype(F32), MASK_DIST))
    s = _dot_nt(kc, q4) - slope_row * dist_c
    m = jnp.max(s, axis=0, keepdims=True)
    e = jnp.where(dist_c < MASK_DIST, jnp.exp2(s - m), 0.0)
    l = jnp.sum(e, axis=0, keepdims=True)
    p = e / jnp.where(l > 0.0, l, 1.0)
    o_cmp = _dot_tn(vc, p.astype(BF16))
    p_sum = p[:, 0:tq]
    for r in range(1, grp):
        p_sum = p_sum + p[:, r * tq:(r + 1) * tq]

    p_hi = p_sum.astype(BF16)
    p_lo = (p_sum - p_hi.astype(F32)).astype(BF16)
    ovt = ovt_ref[...]
    imp = _dot(ovt, p_hi) + _dot(ovt, p_lo)
    blk = lax.broadcasted_iota(jnp.int32, (n_sel, tq), 0)
    t_sel = t0 + lax.broadcasted_iota(jnp.int32, (n_sel, tq), 1)
    back = jnp.right_shift(t_sel, SEL_SHIFT) - blk
    forced = jnp.where(blk == 0, 0, jnp.where(back < 0, SEL_LOCAL, back)) < SEL_LOCAL
    imp = jnp.where(forced, FORCED_SCORE, jnp.where(blk * SEL_BLOCK > t_sel, -1.0, imp))
    rank = jnp.zeros((n_sel, tq), F32)
    for j in range(n_sel):
        cand = imp[j:j + 1, :]
        tie = jnp.where(blk > j, 1.0, 0.0)
        rank = rank + jnp.where(cand > imp, 1.0, jnp.where(cand == imp, tie, 0.0))
    unsel = jnp.where(rank < k_top, 0.0, -MASK_POW2)
    unsel = jnp.concatenate([jnp.zeros((SEL_LANE0, tq), F32), unsel,
                             jnp.zeros((LANES - SEL_LANE0 - n_sel, tq), F32)], axis=0).T
    q_aug = jnp.concatenate(
        [q4, jnp.concatenate([(unsel + qx_ref[r:r + 1, :]).astype(BF16) for r in range(grp)], axis=0)],
        axis=1)

    span = WINDOW + tw
    o_parts = []
    for u in range(tq // tw):
        tu = t0 + u * tw
        case = jnp.minimum(tu // tw, WINDOW // tw)
        w0 = pl.multiple_of(jnp.maximum(tu - WINDOW, 0), tw)
        q_u = jnp.concatenate([q_aug[r * tq + u * tw:r * tq + (u + 1) * tw, :] for r in range(grp)], axis=0)
        k_aug = jnp.concatenate([kw_ref[pl.ds(w0, span), :], kxw_ref[pl.ds(w0, span), :]], axis=1)
        s = _dot_nt(k_aug, q_u) + jnp.concatenate([band_ref[case]] * grp, axis=1)
        p = jnp.exp2(s - jnp.max(s, axis=0, keepdims=True))
        o_parts.append(_dot_tn(vw_ref[pl.ds(w0, span), :], p.astype(BF16))
                       / jnp.sum(p, axis=0, keepdims=True))
    o_win = jnp.concatenate([o_parts[u][:, r * tw:(r + 1) * tw]
                             for r in range(grp) for u in range(tq // tw)], axis=1)

    def sel_branch(nk):
        n_chunks = nk // tk

        def scores(c):
            rows = slice(c * tk, (c + 1) * tk)
            k_aug = jnp.concatenate([ks_ref[rows, :], kxs_ref[rows, :]], axis=1)
            s = _dot_nt(k_aug, q_aug)
            if c == n_chunks - 1:
                ahead = (lax.broadcasted_iota(jnp.int32, (tk, tq), 0) + (c * tk - t0)
                         > lax.broadcasted_iota(jnp.int32, (tk, tq), 1))
                s = s + tile4(jnp.where(ahead, NEG_INF, 0.0))
            return s

        m = l = acc = None
        s_next = scores(0)
        for c in range(n_chunks):
            rows = slice(c * tk, (c + 1) * tk)
            s = s_next
            if c + 1 < n_chunks:
                s_next = scores(c + 1)
            m_c = jnp.max(s, axis=0, keepdims=True)
            m_new = m_c if c == 0 else jnp.maximum(m, m_c)
            p = jnp.exp2(s - m_new)
            l_c = jnp.sum(p, axis=0, keepdims=True)
            pv = _dot_tn(vs_ref[rows, :], p.astype(BF16))
            if c == 0:
                l, acc = l_c, pv
            else:
                alpha = jnp.exp2(m - m_new)
                l, acc = alpha * l + l_c, alpha * acc + pv
            m = m_new
        return acc / l

    n_kt = (t0 + tq + tk - 1) // tk
    for n in range(1, ks_ref.shape[0] // tk + 1):
        @pl.when(n_kt == n)
        def _():
            osel_ref[...] = sel_branch(n * tk)

    o_sel = osel_ref[...]

    gate_ref[...] = jax.nn.sigmoid(gl_ref[...].T)
    row0 = 3 * grp * pl.program_id(1)
    for r in range(grp):
        cols = slice(r * tq, (r + 1) * tq)
        out = (gate_ref[pl.ds(row0 + 3 * r, 1), :] * o_cmp[:, cols]
               + gate_ref[pl.ds(row0 + 3 * r + 1, 1), :] * o_sel[:, cols]
               + gate_ref[pl.ds(row0 + 3 * r + 2, 1), :] * o_win[:, cols])
        o_ref[:, r * dh:(r + 1) * dh] = out.T.astype(o_ref.dtype)


def _nsa(proj, q_blk, kv_blk, cmp_kv, gate_logits, batch, seq, tq=512, tk=512, tw=256):
    dh, kvh, grp = NSA_DH, NSA_KV_HEADS, NSA_GROUP
    n_cmp = (seq - CMP_BLOCK) // CMP_STRIDE + 1
    nc = cmp_kv.shape[3]
    n_sel = seq // SEL_BLOCK
    k_top = min(SEL_TOPK, n_sel)
    tw = min(tw, tq)
    span = WINDOW + tw
    n_case = WINDOW // tw + 1
    assert seq % tk == 0 and seq >= span and WINDOW % tw == 0 and tq % tw == 0
    assert tk % tq == 0 and tk % SEL_BLOCK == 0
    assert SEL_LANE0 + n_sel <= LANES and seq <= 256 * 256 and dh == LANES

    slopes = jnp.exp2(-8.0 * (jnp.arange(NSA_HEADS, dtype=F32) + 1.0) / NSA_HEADS).reshape(kvh, grp)
    slope_tab = jnp.broadcast_to(jnp.repeat(slopes * LOG2E, tq, axis=1)[:, None, :], (kvh, 8, grp * tq))
    pieces, rest = [], slopes * LOG2E
    for _ in range(SLOPE_PIECES):
        piece = rest.astype(BF16).astype(F32)
        pieces += [piece, piece]
        rest = rest - piece
    qx = jnp.zeros((kvh, 8, LANES), F32).at[:, :grp, :2 * SLOPE_PIECES].set(jnp.stack(pieces, axis=-1))
    key = jnp.arange(seq)
    key_cols = jnp.stack([(key >> 8) * 256, key & 255] * SLOPE_PIECES, axis=1).astype(F32)
    kxw = jnp.zeros((seq, LANES), F32).at[:, :2 * SLOPE_PIECES].set(key_cols)
    kxs = kxw.at[key, SEL_LANE0 + key // SEL_BLOCK].set(1.0)
    kxs, kxw = kxs.astype(BF16), kxw.astype(BF16)
    cmp_start = CMP_STRIDE * jnp.arange(n_cmp)
    sel_start = SEL_BLOCK * jnp.arange(n_sel)
    overlap = jnp.clip(jnp.minimum(cmp_start[:, None] + CMP_BLOCK, sel_start[None, :] + SEL_BLOCK)
                       - jnp.maximum(cmp_start[:, None], sel_start[None, :]), 0).astype(F32) / CMP_BLOCK
    ovt = jnp.zeros((n_sel, nc), F32).at[:, :n_cmp].set(overlap.T).astype(BF16)
    dist_w = (jnp.arange(n_case)[:, None, None] * tw + jnp.arange(tw)[None, None, :]
              - jnp.arange(span)[None, :, None])
    band = jnp.where((dist_w >= 0) & (dist_w < WINDOW), 0.0, NEG_INF).astype(F32)

    nq = seq // tq
    kv_spec = lambda c: pl.BlockSpec((seq, dh), lambda b, g, i: (b, kv_blk + kvh * c + g))
    cmp_spec = lambda c: pl.BlockSpec((None, None, None, nc, dh), lambda b, g, i: (c, b, g, 0, 0))
    whole = lambda a: pl.BlockSpec(a.shape, lambda b, g, i: (0,) * a.ndim)
    return pl.pallas_call(
        functools.partial(_nsa_kernel, tq=tq, tk=tk, tw=tw, n_cmp=n_cmp, n_sel=n_sel, k_top=k_top),
        grid=(batch, kvh, nq),
        in_specs=[pl.BlockSpec((tq, grp * dh), lambda b, g, i: (b * nq + i, q_blk + g)),
                  kv_spec(0), kv_spec(1), kv_spec(2), kv_spec(3), cmp_spec(0), cmp_spec(1),
                  pl.BlockSpec((tq, LANES), lambda b, g, i: (b * nq + i, 0)),
                  pl.BlockSpec((None, 8, grp * tq), lambda b, g, i: (g, 0, 0)),
                  pl.BlockSpec((None, 8, LANES), lambda b, g, i: (g, 0, 0)),
                  whole(ovt), whole(kxs), whole(kxw), whole(band)],
        out_specs=pl.BlockSpec((tq, grp * dh), lambda b, g, i: (b * nq + i, g)),
        out_shape=jax.ShapeDtypeStruct((batch * seq, kvh * grp * dh), BF16),
        scratch_shapes=[pltpu.VMEM((dh, grp * tq), F32), pltpu.VMEM((LANES, tq), F32)],
        compiler_params=_params("parallel", "parallel", "arbitrary"), name="nsa_attention",
    )(proj, proj, proj, proj, proj, cmp_kv, cmp_kv, gate_logits, slope_tab, qx, ovt, kxs, kxw, band)


def kernel(x, mix_norm_pre, w_in, cmp_pos_k, cmp_w1_k, cmp_w2_k, cmp_pos_v, cmp_w1_v, cmp_w2_v,
           w_ret_up, w_nsa_up, w_out, mix_norm_post, mlp_norm_pre, w_mlp_in, w_mlp_out,
           mlp_norm_post):
    batch, seq, d = x.shape
    depth = w_in.shape[0]
    ret_w = w_ret_up.shape[1]
    nsa_w = w_nsa_up.shape[1]
    kvh, dh = NSA_KV_HEADS, NSA_DH
    kv_w = kvh * dh
    n_gate = 3 * NSA_HEADS
    c_qn = 4 * ret_w
    c_kv = c_qn + nsa_w
    c_gate = c_kv + 6 * kv_w
    c_gr = c_gate + n_gate

    w_in_t = jnp.swapaxes(w_in, 1, 2)
    assert n_gate <= LANES
    w_gate_t = jnp.pad(w_in_t[:, c_gate:c_gr, :], ((0, 0), (0, LANES - n_gate), (0, 0))).astype(BF16)
    cmp_pos = jnp.stack([cmp_pos_k, cmp_pos_v], axis=1)
    cmp_w1 = jnp.stack([cmp_w1_k, cmp_w1_v], axis=1).astype(BF16)
    cmp_w2 = jnp.stack([cmp_w2_k, cmp_w2_v], axis=1).astype(BF16)

    xs = x.reshape(batch * seq, d)
    h = _rmsnorm(xs, mix_norm_pre[0])
    for l in range(depth):
        tn_in = math.gcd(c_gate, 1024)
        assert c_qn % tn_in == 0 and c_kv % tn_in == 0
        proj = _matmul_ws(h, w_in_t, l, 0, c_gate + 2 * d, BF16, tn=tn_in, transposed=True,
                          split=c_gate // tn_in, gap=n_gate,
                          col_scale=(c_qn // tn_in, c_kv // tn_in, NSA_DH ** -0.5 * LOG2E), name="proj_in")
        gate_logits = _matmul_nt(h, w_gate_t[l], F32, name="proj_nsa_gate")
        y_ret = _retention(proj, batch, seq, ret_w // RET_HEADS)
        cmp_kv = _compress(proj, c_kv // dh, cmp_pos[l], cmp_w1[l], cmp_w2[l], batch, seq)
        y_nsa = _nsa(proj, c_qn // (NSA_GROUP * dh), (c_kv + 2 * kv_w) // dh, cmp_kv, gate_logits,
                     batch, seq)
        merged = _up_merge(y_ret, y_nsa, proj, c_gate, w_ret_up, w_nsa_up, l)
        mix = _matmul_ws(merged, w_out, l, 0, d, BF16, name="proj_out")
        xs, h = _post(mix, xs, mix_norm_post[l], mlp_norm_pre[l])
        u, w_mo = _matmul_ws(h, w_mlp_in, l, 0, w_mlp_in.shape[2], BF16, relu2=True, side=w_mlp_out,
                             name="mlp_in")
        y = _matmul_ktiled(u, w_mo, BF16, name="mlp_out")
        xs, h = _post(y, xs, mlp_norm_post[l], mix_norm_pre[l + 1] if l + 1 < depth else None)
    return xs.reshape(batch, seq, d)
```

```python
import functools
import math

import jax
import jax.numpy as jnp
from jax import lax
from jax.experimental import pallas as pl
from jax.experimental.pallas import tpu as pltpu

RET_HEADS = 8
RET_CHUNK = 128
NSA_HEADS = 16
NSA_KV_HEADS = 4
NSA_DH = 128
CMP_BLOCK = 32
CMP_STRIDE = 16
SEL_BLOCK = 64
SEL_TOPK = 16
SEL_LOCAL = 2
WINDOW = 512
EPS = 1e-6
NEG_INF = -1e30
FORCED_SCORE = 1e4

NSA_GROUP = NSA_HEADS // NSA_KV_HEADS
SEL_SHIFT = SEL_BLOCK.bit_length() - 1
LANES = 128
ROW_ALIGN = 16
VMEM_LIMIT = 56 * 1024 * 1024
MAX_STAGE_CHUNKS = 8
LOG2E = math.log2(math.e)
MASK_DIST = 1e32
MASK_POW2 = 2.0 ** 100
SLOPE_PIECES = 3
SEL_LANE0 = 32

F32 = jnp.float32
BF16 = jnp.bfloat16


def _params(*semantics):
    return pltpu.CompilerParams(dimension_semantics=semantics, vmem_limit_bytes=VMEM_LIMIT)


def _dot(a, b):
    return jnp.dot(a, b, preferred_element_type=F32)


def _dot_nt(a, b):
    return lax.dot_general(a, b, (((1,), (1,)), ((), ())), preferred_element_type=F32)


def _dot_tn(a, b):
    return lax.dot_general(a, b, (((0,), (0,)), ((), ())), preferred_element_type=F32)


def _rmsnorm_kernel(x_ref, g_ref, wg_ref, o_ref, gl_ref):
    x = x_ref[...]
    ms = jnp.mean(x * x, axis=-1, keepdims=True)
    h = (x * lax.rsqrt(ms + EPS) * g_ref[...]).astype(o_ref.dtype)
    o_ref[...] = h
    gl_ref[...] = _dot_nt(h, wg_ref[...])


def _rmsnorm(x, gain, w_side_t, tm=256):
    m, d = x.shape
    ns = w_side_t.shape[0]
    return pl.pallas_call(
        _rmsnorm_kernel,
        grid=(m // tm,),
        in_specs=[pl.BlockSpec((tm, d), lambda i: (i, 0)),
                  pl.BlockSpec((1, d), lambda i: (0, 0)),
                  pl.BlockSpec((ns, d), lambda i: (0, 0))],
        out_specs=[pl.BlockSpec((tm, d), lambda i: (i, 0)), pl.BlockSpec((tm, ns), lambda i: (i, 0))],
        out_shape=[jax.ShapeDtypeStruct((m, d), BF16), jax.ShapeDtypeStruct((m, ns), F32)],
        compiler_params=_params("parallel"),
        name="rmsnorm",
    )(x, gain.reshape(1, d), w_side_t)


def _post_kernel(*refs, side):
    if side:
        y_ref, x_ref, gp_ref, gn_ref, wg_ref, xo_ref, ho_ref, gl_ref = refs
    else:
        y_ref, x_ref, gp_ref, gn_ref, xo_ref, ho_ref = refs
    y = y_ref[...].astype(F32)
    yn = y * lax.rsqrt(jnp.mean(y * y, axis=-1, keepdims=True) + EPS) * gp_ref[...]
    xn = x_ref[...] + yn
    xo_ref[...] = xn
    h = (xn * lax.rsqrt(jnp.mean(xn * xn, axis=-1, keepdims=True) + EPS) * gn_ref[...]).astype(ho_ref.dtype)
    ho_ref[...] = h
    if side:
        gl_ref[...] = _dot_nt(h, wg_ref[...])


def _post_last_kernel(y_ref, x_ref, gp_ref, xo_ref):
    y = y_ref[...].astype(F32)
    yn = y * lax.rsqrt(jnp.mean(y * y, axis=-1, keepdims=True) + EPS) * gp_ref[...]
    xo_ref[...] = x_ref[...] + yn


def _post(y, x, g_post, g_next, w_side_t=None, tm=256):
    m, d = x.shape
    row = pl.BlockSpec((tm, d), lambda i: (i, 0))
    vec = pl.BlockSpec((1, d), lambda i: (0, 0))
    if g_next is None:
        return pl.pallas_call(
            _post_last_kernel, grid=(m // tm,),
            in_specs=[row, row, vec], out_specs=row,
            out_shape=jax.ShapeDtypeStruct((m, d), F32),
            compiler_params=_params("parallel"), name="post_last",
        )(y, x, g_post.reshape(1, d)), None
    in_specs, operands = [row, row, vec, vec], [y, x, g_post.reshape(1, d), g_next.reshape(1, d)]
    out_specs = [row, row]
    out_shape = [jax.ShapeDtypeStruct((m, d), F32), jax.ShapeDtypeStruct((m, d), BF16)]
    if w_side_t is not None:
        ns = w_side_t.shape[0]
        in_specs.append(pl.BlockSpec((ns, d), lambda i: (0, 0)))
        operands.append(w_side_t)
        out_specs.append(pl.BlockSpec((tm, ns), lambda i: (i, 0)))
        out_shape.append(jax.ShapeDtypeStruct((m, ns), F32))
    return pl.pallas_call(
        functools.partial(_post_kernel, side=w_side_t is not None), grid=(m // tm,),
        in_specs=in_specs, out_specs=out_specs, out_shape=out_shape,
        compiler_params=_params("parallel"), name="post",
    )(*operands)


def _stage_weights(j, i, n_ch, wq_ref, stage_ref, start, wait):
    ch = stage_ref.shape[1]

    def convert(buf, c, slot):
        wq_ref[buf, pl.ds(pl.multiple_of(c * ch, ch), ch), :] = stage_ref[slot].astype(BF16)

    @pl.when((j == 0) & (i == 0))
    def _():
        start(0, 0, 0)
        if n_ch > 1:
            start(0, 1, 1)
        for c in range(n_ch):
            wait(0, c, c % 2)
            convert(0, c, c % 2)
            if c + 2 < n_ch:
                start(0, c + 2, c % 2)

    has_next = j + 1 < pl.num_programs(0)

    @pl.when(has_next & (i >= 1) & (i <= n_ch))
    def _():
        slot = lax.rem(i - 1, 2)
        wait(j + 1, i - 1, slot)
        convert(lax.rem(j + 1, 2), i - 1, slot)

    @pl.when(has_next & (i < n_ch))
    def _():
        start(j + 1, i, lax.rem(i, 2))


def _stage_chunks(rows, n_i):
    n_ch = min(MAX_STAGE_CHUNKS, n_i - 1)
    while rows % n_ch or (rows // n_ch) % ROW_ALIGN:
        n_ch -= 1
    return n_ch


def _matmul_ws_kernel(*refs, relu2, transposed, layer, row0, n_ch, split, gap, side, col_scale):
    if side:
        a_ref, w_hbm, side_ref, o_ref, side_o_ref, wq_ref, stage_ref, sem = refs
        side_o_ref[...] = side_ref[...].astype(side_o_ref.dtype)
    else:
        a_ref, w_hbm, o_ref, wq_ref, stage_ref, sem = refs
    j, i = pl.program_id(0), pl.program_id(1)
    ch = stage_ref.shape[1]
    tile = wq_ref.shape[1] if transposed else wq_ref.shape[2]

    def chunk_copy(col, c, slot):
        first = row0 + col * tile
        if split is not None:
            first = first + jnp.where(col >= split, gap, 0)
        if transposed:
            src = w_hbm.at[layer, pl.ds(pl.multiple_of(first + c * ch, ROW_ALIGN), ch), :]
        else:
            src = w_hbm.at[layer, pl.ds(c * ch, ch), pl.ds(first, tile)]
        return pltpu.make_async_copy(src, stage_ref.at[slot], sem.at[slot])

    _stage_weights(j, i, n_ch, wq_ref, stage_ref,
                   lambda col, c, slot: chunk_copy(col, c, slot).start(),
                   lambda col, c, slot: chunk_copy(col, c, slot).wait())

    def compute(buf, gate):
        wq = wq_ref[buf]
        acc = _dot_nt(a_ref[...], wq) if transposed else _dot(a_ref[...], wq)
        if relu2:
            acc = jnp.square(jnp.maximum(acc, 0.0))
        if gate:
            acc = jax.nn.sigmoid(acc)
        elif col_scale is not None:
            lo, hi, factor = col_scale
            acc = acc * jnp.where((j >= lo) & (j < hi), factor, 1.0).astype(F32)
        o_ref[...] = acc.astype(o_ref.dtype)

    for buf in range(2):
        if split is None:
            pl.when(lax.rem(j, 2) == buf)(functools.partial(compute, buf, False))
        else:
            pl.when((lax.rem(j, 2) == buf) & (j < split))(functools.partial(compute, buf, False))
            pl.when((lax.rem(j, 2) == buf) & (j >= split))(functools.partial(compute, buf, True))


def _matmul_ws(a, w, layer, col0, n, out_dtype, tm=1024, tn=1024, relu2=False, transposed=False,
               split=None, gap=0, side=None, col_scale=None, name="matmul_ws"):
    m, k = a.shape
    tm, tn = min(tm, m), min(tn, n)
    assert n % tn == 0 and m % tm == 0 and col0 % ROW_ALIGN == 0 and gap % ROW_ALIGN == 0
    n_i = m // tm
    assert n_i >= 2, "the next tile's weights are staged across the row tiles of the current one"
    n_ch = _stage_chunks(tn if transposed else k, n_i)
    wq_shape = (2, tn, k) if transposed else (2, k, tn)
    in_specs = [pl.BlockSpec((tm, k), lambda j, i: (i, 0)), pl.BlockSpec(memory_space=pl.ANY)]
    out_specs = [pl.BlockSpec((tm, tn), lambda j, i: (i, j))]
    out_shape = [jax.ShapeDtypeStruct((m, n), out_dtype)]
    operands = [a, w]
    if side is not None:
        _, rows, cols = side.shape
        steps = (n // tn) * n_i
        sr = rows // steps
        assert rows % steps == 0 and sr % ROW_ALIGN == 0
        in_specs.append(pl.BlockSpec((None, sr, cols), lambda j, i: (layer, j * n_i + i, 0)))
        out_specs.append(pl.BlockSpec((sr, cols), lambda j, i: (j * n_i + i, 0)))
        out_shape.append(jax.ShapeDtypeStruct((rows, cols), BF16))
        operands.append(side)
    out = pl.pallas_call(
        functools.partial(_matmul_ws_kernel, relu2=relu2, transposed=transposed, layer=layer,
                          row0=col0, n_ch=n_ch, split=split, gap=gap, side=side is not None,
                          col_scale=col_scale),
        grid=(n // tn, n_i),
        in_specs=in_specs, out_specs=out_specs, out_shape=out_shape,
        scratch_shapes=[pltpu.VMEM(wq_shape, BF16),
                        pltpu.VMEM((2, wq_shape[1] // n_ch, wq_shape[2]), F32),
                        pltpu.SemaphoreType.DMA((2,))],
        compiler_params=_params("arbitrary", "arbitrary"), name=name,
    )(*operands)
    return out if side is not None else out[0]


def _up_merge_kernel(yr_ref, yn_ref, gr_ref, gn_ref, wr_hbm, wn_hbm, o_ref, wq_ref, stage_ref, sem,
                     *, layer, n_ch):
    j, i = pl.program_id(0), pl.program_id(1)
    ch = stage_ref.shape[1]
    tile = wq_ref.shape[2]
    kr = yr_ref.shape[1]
    half = kr // ch

    def chunk_copy(w_hbm, col, c, slot):
        src = w_hbm.at[layer, pl.ds(pl.multiple_of(c * ch, ROW_ALIGN), ch), pl.ds(col * tile, tile)]
        return pltpu.make_async_copy(src, stage_ref.at[slot], sem.at[slot])

    def both(method):
        def go(col, c, slot):
            if isinstance(c, int):
                w_hbm, cc = (wr_hbm, c) if c < half else (wn_hbm, c - half)
                getattr(chunk_copy(w_hbm, col, cc, slot), method)()
            else:
                pl.when(c < half)(lambda: getattr(chunk_copy(wr_hbm, col, c, slot), method)())
                pl.when(c >= half)(lambda: getattr(chunk_copy(wn_hbm, col, c - half, slot), method)())
        return go

    _stage_weights(j, i, n_ch, wq_ref, stage_ref, both("start"), both("wait"))

    for buf in range(2):
        @pl.when(lax.rem(j, 2) == buf)
        def _():
            a = _dot(yr_ref[...], wq_ref[buf, 0:kr, :])
            b = _dot(yn_ref[...], wq_ref[buf, kr:, :])
            o_ref[...] = (gr_ref[...].astype(F32) * a + gn_ref[...].astype(F32) * b).astype(o_ref.dtype)


def _up_merge(y_ret, y_nsa, gates, g_col, w_ru, w_nu, layer, tm=1024, tn=1024):
    m, kr = y_ret.shape
    kn = y_nsa.shape[1]
    n = w_ru.shape[2]
    tm, tn = min(tm, m), min(tn, n)
    while m // tm < 3 and tm % (2 * ROW_ALIGN) == 0:
        tm //= 2
    n_i, n_j = m // tm, n // tn
    assert m % tm == 0 and n % tn == 0 and n_i >= 3 and kr == kn and g_col % tn == 0
    g_blk = g_col // tn
    n_ch = 2 * _stage_chunks(kr, (n_i + 1) // 2)
    ch = (kr + kn) // n_ch
    return pl.pallas_call(
        functools.partial(_up_merge_kernel, layer=layer, n_ch=n_ch),
        grid=(n_j, n_i),
        in_specs=[pl.BlockSpec((tm, kr), lambda j, i: (i, 0)),
                  pl.BlockSpec((tm, kn), lambda j, i: (i, 0)),
                  pl.BlockSpec((tm, tn), lambda j, i: (i, g_blk + j)),
                  pl.BlockSpec((tm, tn), lambda j, i: (i, g_blk + n_j + j)),
                  pl.BlockSpec(memory_space=pl.ANY), pl.BlockSpec(memory_space=pl.ANY)],
        out_specs=pl.BlockSpec((tm, tn), lambda j, i: (i, j)),
        out_shape=jax.ShapeDtypeStruct((m, n), BF16),
        scratch_shapes=[pltpu.VMEM((2, kr + kn, tn), BF16), pltpu.VMEM((2, ch, tn), F32),
                        pltpu.SemaphoreType.DMA((2,))],
        compiler_params=_params("arbitrary", "arbitrary"), name="up_merge",
    )(y_ret, y_nsa, gates, gates, w_ru, w_nu)


def _matmul_ktiled_kernel(a_ref, w_ref, o_ref, acc_ref):
    kk = pl.program_id(2)

    @pl.when(kk == 0)
    def _():
        acc_ref[...] = jnp.zeros_like(acc_ref)

    acc_ref[...] += _dot(a_ref[...], w_ref[...])

    @pl.when(kk == pl.num_programs(2) - 1)
    def _():
        o_ref[...] = acc_ref[...].astype(o_ref.dtype)


def _matmul_ktiled(a, w, out_dtype, tm=2048, tn=1024, tk=2048, name="matmul_k"):
    m, k = a.shape
    n = w.shape[1]
    tm, tn, tk = min(tm, m), min(tn, n), min(tk, k)
    return pl.pallas_call(
        _matmul_ktiled_kernel,
        grid=(m // tm, n // tn, k // tk),
        in_specs=[pl.BlockSpec((tm, tk), lambda i, j, kk: (i, kk)),
                  pl.BlockSpec((tk, tn), lambda i, j, kk: (kk, j))],
        out_specs=pl.BlockSpec((tm, tn), lambda i, j, kk: (i, j)),
        out_shape=jax.ShapeDtypeStruct((m, n), out_dtype),
        scratch_shapes=[pltpu.VMEM((tm, tn), F32)],
        compiler_params=_params("parallel", "parallel", "arbitrary"), name=name,
    )(a, w)


def _retention_kernel(q_ref, k_ref, v_ref, g_ref, di_ref, dq_ref, dk_ref, dc_ref, o_ref,
                      state_ref, *, n_chunks, chunk, k_scale):
    state_ref[...] = jnp.zeros_like(state_ref)
    hp = state_ref.shape[0]
    dk = state_ref.shape[1]

    def body(n, carry):
        rows = pl.ds(pl.multiple_of(n * chunk, chunk), chunk)
        for hh in range(hp):
            cols = slice(hh * dk, (hh + 1) * dk)
            q = q_ref[rows, cols]
            kf = k_ref[rows, cols].astype(F32) * k_scale
            v = v_ref[rows, cols]
            s = _dot_nt(q, kf.astype(BF16)) * di_ref[hh]
            state = state_ref[hh]
            o = _dot(s.astype(BF16), v) + _dot(q, state.astype(BF16)) * dq_ref[hh]
            state_ref[hh] = state * dc_ref[hh, 0:1, :] + _dot_tn((kf * dk_ref[hh]).astype(BF16), v)
            mu = jnp.mean(o, axis=-1, keepdims=True)
            oc = o - mu
            var = jnp.mean(oc * oc, axis=-1, keepdims=True)
            gate = g_ref[rows, cols].astype(F32)
            y = gate * jax.nn.sigmoid(gate) * (oc * lax.rsqrt(var + EPS))
            o_ref[rows, cols] = y.astype(o_ref.dtype)
        return carry

    lax.fori_loop(0, n_chunks, body, 0, unroll=4)


def _retention(proj, batch, seq, dk, hp=4):
    heads, chunk = RET_HEADS, RET_CHUNK
    assert heads % hp == 0
    dv = dk
    n_chunks = seq // chunk
    log_gamma = jnp.log1p(-jnp.exp2(-5.0 - jnp.arange(heads, dtype=F32)))
    pos = jnp.arange(chunk, dtype=F32)
    rel = pos[:, None] - pos[None, :]
    d_intra = jnp.where(rel >= 0, jnp.exp(log_gamma[:, None, None] * jnp.maximum(rel, 0.0)), 0.0)
    d_query = jnp.broadcast_to(jnp.exp(log_gamma[:, None] * (pos + 1.0))[..., None], (heads, chunk, dv))
    d_key = jnp.broadcast_to(jnp.exp(log_gamma[:, None] * (chunk - 1.0 - pos))[..., None], (heads, chunk, dk))
    d_chunk = jnp.broadcast_to(jnp.exp(log_gamma * chunk)[:, None, None], (heads, 8, dv))

    col = lambda off: pl.BlockSpec((seq, hp * dk), lambda b, h: (b, off // hp + h))
    tab = lambda r, c: pl.BlockSpec((hp, r, c), lambda b, h: (h, 0, 0))
    return pl.pallas_call(
        functools.partial(_retention_kernel, n_chunks=n_chunks, chunk=chunk, k_scale=dk ** -0.5),
        grid=(batch, heads // hp),
        in_specs=[col(0), col(heads), col(2 * heads), col(3 * heads),
                  tab(chunk, chunk), tab(chunk, dv), tab(chunk, dk), tab(8, dv)],
        out_specs=pl.BlockSpec((seq, hp * dv), lambda b, h: (b, h)),
        out_shape=jax.ShapeDtypeStruct((batch * seq, heads * dv), BF16),
        scratch_shapes=[pltpu.VMEM((hp, dk, dv), F32)],
        compiler_params=_params("parallel", "parallel"), name="retention",
    )(proj, proj, proj, proj, d_intra, d_query, d_key, d_chunk)


def _compress_kernel(x_ref, pos_ref, w1_ref, w2_ref, o_ref, x32_ref, *, n_cmp, n_chunks):
    dh = NSA_DH
    half = CMP_BLOCK // 2
    x32_ref[...] = x_ref[...].astype(F32)
    lo = jnp.zeros((n_chunks, w1_ref.shape[1]), F32)
    hi = jnp.zeros((n_chunks, w1_ref.shape[1]), F32)
    for l in range(half):
        xl = x32_ref[pl.ds(l, n_chunks, stride=CMP_STRIDE), :]
        a = (xl + pos_ref[l:l + 1, :]).astype(BF16)
        b = (xl + pos_ref[half + l:half + l + 1, :]).astype(BF16)
        lo = lo + _dot(a, w1_ref[l * dh:(l + 1) * dh, :])
        hi = hi + _dot(b, w1_ref[(half + l) * dh:(half + l + 1) * dh, :])
    pre = lo + pltpu.roll(hi, n_chunks - 1, axis=0)
    hid = pre * jax.nn.sigmoid(pre)
    out = _dot(hid.astype(BF16), w2_ref[...])
    row = lax.broadcasted_iota(jnp.int32, out.shape, 0)
    o_ref[...] = jnp.where(row < n_cmp, out, 0.0).astype(o_ref.dtype)


def _compress(proj, col_blk, pos, w1, w2, batch, seq):
    kvh, dh = NSA_KV_HEADS, NSA_DH
    n_chunks = seq // CMP_STRIDE
    n_cmp = (seq - CMP_BLOCK) // CMP_STRIDE + 1
    assert CMP_BLOCK == 2 * CMP_STRIDE and n_cmp == n_chunks - 1
    hidden = w1.shape[-1]
    return pl.pallas_call(
        functools.partial(_compress_kernel, n_cmp=n_cmp, n_chunks=n_chunks),
        grid=(2, batch, kvh),
        in_specs=[pl.BlockSpec((seq, dh), lambda c, b, g: (b, col_blk + kvh * c + g)),
                  pl.BlockSpec((None, CMP_BLOCK, dh), lambda c, b, g: (c, 0, 0)),
                  pl.BlockSpec((None, CMP_BLOCK * dh, hidden), lambda c, b, g: (c, 0, 0)),
                  pl.BlockSpec((None, hidden, dh), lambda c, b, g: (c, 0, 0))],
        out_specs=pl.BlockSpec((None, None, None, n_chunks, dh), lambda c, b, g: (c, b, g, 0, 0)),
        out_shape=jax.ShapeDtypeStruct((2, batch, kvh, n_chunks, dh), BF16),
        scratch_shapes=[pltpu.VMEM((seq, dh), F32)],
        compiler_params=_params("parallel", "parallel", "parallel"), name="nsa_compress",
    )(proj, pos, w1, w2)


def _nsa_kernel(q_ref, ks_ref, vs_ref, kw_ref, vw_ref, kc_ref, vc_ref, gl_ref, slope_ref, qx_ref,
                ovt_ref, kxs_ref, kxw_ref, band_ref, o_ref, osel_ref, gate_ref,
                *, tq, tk, tw, n_cmp, n_sel, k_top):
    dh = NSA_DH
    grp = NSA_GROUP
    t0 = pl.program_id(2) * tq
    q = q_ref[...]
    q4 = jnp.concatenate([q[:, r * dh:(r + 1) * dh] for r in range(grp)], axis=0)
    slope_row = slope_ref[0:1, :]
    tile4 = lambda a: jnp.concatenate([a] * grp, axis=1)

    kc = kc_ref[...]
    vc = vc_ref[...]
    nc = kc.shape[0]
    n_idx = lax.broadcasted_iota(jnp.int32, (nc, tq), 0)
    t_idx = t0 + lax.broadcasted_iota(jnp.int32, (nc, tq), 1)
    dist_c = t_idx - (CMP_STRIDE * n_idx + (CMP_BLOCK - 1))
    dist_c = jnp.where(n_idx < n_cmp, dist_c, -1)
    dist_c = tile4(jnp.where(dist_c >= 0, dist_c.astype(F32), MASK_DIST))
    s = _dot_nt(kc, q4) - slope_row * dist_c
    m = jnp.max(s, axis=0, keepdims=True)
    e = jnp.where(dist_c < MASK_DIST, jnp.exp2(s - m), 0.0)
    l = jnp.sum(e, axis=0, keepdims=True)
    p = e / jnp.where(l > 0.0, l, 1.0)
    o_cmp = _dot_tn(vc, p.astype(BF16))
    p_sum = p[:, 0:tq]
    for r in range(1, grp):
        p_sum = p_sum + p[:, r * tq:(r + 1) * tq]

    p_hi = p_sum.astype(BF16)
    p_lo = (p_sum - p_hi.astype(F32)).astype(BF16)
    ovt = ovt_ref[...]
    imp = _dot(ovt, p_hi) + _dot(ovt, p_lo)
    blk = lax.broadcasted_iota(jnp.int32, (n_sel, tq), 0)
    t_sel = t0 + lax.broadcasted_iota(jnp.int32, (n_sel, tq), 1)
    back = jnp.right_shift(t_sel, SEL_SHIFT) - blk
    forced = jnp.where(blk == 0, 0, jnp.where(back < 0, SEL_LOCAL, back)) < SEL_LOCAL
    imp = jnp.where(forced, FORCED_SCORE, jnp.where(blk * SEL_BLOCK > t_sel, -1.0, imp))
    rank = jnp.zeros((n_sel, tq), F32)
    for j in range(n_sel):
        cand = imp[j:j + 1, :]
        tie = jnp.where(blk > j, 1.0, 0.0)
        rank = rank + jnp.where(cand > imp, 1.0, jnp.where(cand == imp, tie, 0.0))
    unsel = jnp.where(rank < k_top, 0.0, -MASK_POW2)
    unsel = jnp.concatenate([jnp.zeros((SEL_LANE0, tq), F32), unsel,
                             jnp.zeros((LANES - SEL_LANE0 - n_sel, tq), F32)], axis=0).T
    q_aug = jnp.concatenate(
        [q4, jnp.concatenate([(unsel + qx_ref[r:r + 1, :]).astype(BF16) for r in range(grp)], axis=0)],
        axis=1)

    span = WINDOW + tw
    o_parts = []
    for u in range(tq // tw):
        tu = t0 + u * tw
        case = jnp.minimum(tu // tw, WINDOW // tw)
        w0 = pl.multiple_of(jnp.maximum(tu - WINDOW, 0), tw)
        q_u = jnp.concatenate([q_aug[r * tq + u * tw:r * tq + (u + 1) * tw, :] for r in range(grp)], axis=0)
        k_aug = jnp.concatenate([kw_ref[pl.ds(w0, span), :], kxw_ref[pl.ds(w0, span), :]], axis=1)
        s = _dot_nt(k_aug, q_u) + jnp.concatenate([band_ref[case]] * grp, axis=1)
        p = jnp.exp2(s - jnp.max(s, axis=0, keepdims=True))
        o_parts.append(_dot_tn(vw_ref[pl.ds(w0, span), :], p.astype(BF16))
                       / jnp.sum(p, axis=0, keepdims=True))
    o_win = jnp.concatenate([o_parts[u][:, r * tw:(r + 1) * tw]
                             for r in range(grp) for u in range(tq // tw)], axis=1)

    def sel_branch(nk):
        n_chunks = nk // tk

        def scores(c):
            rows = slice(c * tk, (c + 1) * tk)
            k_aug = jnp.concatenate([ks_ref[rows, :], kxs_ref[rows, :]], axis=1)
            s = _dot_nt(k_aug, q_aug)
            if c == n_chunks - 1:
                ahead = (lax.broadcasted_iota(jnp.int32, (tk, tq), 0) + (c * tk - t0)
                         > lax.broadcasted_iota(jnp.int32, (tk, tq), 1))
                s = s + tile4(jnp.where(ahead, NEG_INF, 0.0))
            return s

        m = l = acc = None
        s_next = scores(0)
        for c in range(n_chunks):
            rows = slice(c * tk, (c + 1) * tk)
            s = s_next
            if c + 1 < n_chunks:
                s_next = scores(c + 1)
            m_c = jnp.max(s, axis=0, keepdims=True)
            m_new = m_c if c == 0 else jnp.maximum(m, m_c)
            p = jnp.exp2(s - m_new)
            l_c = jnp.sum(p, axis=0, keepdims=True)
            pv = _dot_tn(vs_ref[rows, :], p.astype(BF16))
            if c == 0:
                l, acc = l_c, pv
            else:
                alpha = jnp.exp2(m - m_new)
                l, acc = alpha * l + l_c, alpha * acc + pv
            m = m_new
        return acc / l

    n_kt = (t0 + tq + tk - 1) // tk
    for n in range(1, ks_ref.shape[0] // tk + 1):
        @pl.when(n_kt == n)
        def _():
            osel_ref[...] = sel_branch(n * tk)

    o_sel = osel_ref[...]

    gate_ref[...] = jax.nn.sigmoid(gl_ref[...].T)
    row0 = 3 * grp * pl.program_id(1)
    for r in range(grp):
        cols = slice(r * tq, (r + 1) * tq)
        out = (gate_ref[pl.ds(row0 + 3 * r, 1), :] * o_cmp[:, cols]
               + gate_ref[pl.ds(row0 + 3 * r + 1, 1), :] * o_sel[:, cols]
               + gate_ref[pl.ds(row0 + 3 * r + 2, 1), :] * o_win[:, cols])
        o_ref[:, r * dh:(r + 1) * dh] = out.T.astype(o_ref.dtype)


def _nsa(proj, q_blk, kv_blk, cmp_kv, gate_logits, batch, seq, tq=512, tk=512, tw=256):
    dh, kvh, grp = NSA_DH, NSA_KV_HEADS, NSA_GROUP
    n_cmp = (seq - CMP_BLOCK) // CMP_STRIDE + 1
    nc = cmp_kv.shape[3]
    n_sel = seq // SEL_BLOCK
    k_top = min(SEL_TOPK, n_sel)
    tw = min(tw, tq)
    span = WINDOW + tw
    n_case = WINDOW // tw + 1
    assert seq % tk == 0 and seq >= span and WINDOW % tw == 0 and tq % tw == 0
    assert tk % tq == 0 and tk % SEL_BLOCK == 0
    assert SEL_LANE0 + n_sel <= LANES and seq <= 256 * 256 and dh == LANES

    slopes = jnp.exp2(-8.0 * (jnp.arange(NSA_HEADS, dtype=F32) + 1.0) / NSA_HEADS).reshape(kvh, grp)
    slope_tab = jnp.broadcast_to(jnp.repeat(slopes * LOG2E, tq, axis=1)[:, None, :], (kvh, 8, grp * tq))
    pieces, rest = [], slopes * LOG2E
    for _ in range(SLOPE_PIECES):
        piece = rest.astype(BF16).astype(F32)
        pieces += [piece, piece]
        rest = rest - piece
    qx = jnp.zeros((kvh, 8, LANES), F32).at[:, :grp, :2 * SLOPE_PIECES].set(jnp.stack(pieces, axis=-1))
    key = jnp.arange(seq)
    key_cols = jnp.stack([(key >> 8) * 256, key & 255] * SLOPE_PIECES, axis=1).astype(F32)
    kxw = jnp.zeros((seq, LANES), F32).at[:, :2 * SLOPE_PIECES].set(key_cols)
    kxs = kxw.at[key, SEL_LANE0 + key // SEL_BLOCK].set(1.0)
    kxs, kxw = kxs.astype(BF16), kxw.astype(BF16)
    cmp_start = CMP_STRIDE * jnp.arange(n_cmp)
    sel_start = SEL_BLOCK * jnp.arange(n_sel)
    overlap = jnp.clip(jnp.minimum(cmp_start[:, None] + CMP_BLOCK, sel_start[None, :] + SEL_BLOCK)
                       - jnp.maximum(cmp_start[:, None], sel_start[None, :]), 0).astype(F32) / CMP_BLOCK
    ovt = jnp.zeros((n_sel, nc), F32).at[:, :n_cmp].set(overlap.T).astype(BF16)
    dist_w = (jnp.arange(n_case)[:, None, None] * tw + jnp.arange(tw)[None, None, :]
              - jnp.arange(span)[None, :, None])
    band = jnp.where((dist_w >= 0) & (dist_w < WINDOW), 0.0, NEG_INF).astype(F32)

    nq = seq // tq
    kv_spec = lambda c: pl.BlockSpec((seq, dh), lambda b, g, i: (b, kv_blk + kvh * c + g))
    cmp_spec = lambda c: pl.BlockSpec((None, None, None, nc, dh), lambda b, g, i: (c, b, g, 0, 0))
    whole = lambda a: pl.BlockSpec(a.shape, lambda b, g, i: (0,) * a.ndim)
    return pl.pallas_call(
        functools.partial(_nsa_kernel, tq=tq, tk=tk, tw=tw, n_cmp=n_cmp, n_sel=n_sel, k_top=k_top),
        grid=(batch, kvh, nq),
        in_specs=[pl.BlockSpec((tq, grp * dh), lambda b, g, i: (b * nq + i, q_blk + g)),
                  kv_spec(0), kv_spec(1), kv_spec(2), kv_spec(3), cmp_spec(0), cmp_spec(1),
                  pl.BlockSpec((tq, LANES), lambda b, g, i: (b * nq + i, 0)),
                  pl.BlockSpec((None, 8, grp * tq), lambda b, g, i: (g, 0, 0)),
                  pl.BlockSpec((None, 8, LANES), lambda b, g, i: (g, 0, 0)),
                  whole(ovt), whole(kxs), whole(kxw), whole(band)],
        out_specs=pl.BlockSpec((tq, grp * dh), lambda b, g, i: (b * nq + i, g)),
        out_shape=jax.ShapeDtypeStruct((batch * seq, kvh * grp * dh), BF16),
        scratch_shapes=[pltpu.VMEM((dh, grp * tq), F32), pltpu.VMEM((LANES, tq), F32)],
        compiler_params=_params("parallel", "parallel", "arbitrary"), name="nsa_attention",
    )(proj, proj, proj, proj, proj, cmp_kv, cmp_kv, gate_logits, slope_tab, qx, ovt, kxs, kxw, band)


def kernel(x, mix_norm_pre, w_in, cmp_pos_k, cmp_w1_k, cmp_w2_k, cmp_pos_v, cmp_w1_v, cmp_w2_v,
           w_ret_up, w_nsa_up, w_out, mix_norm_post, mlp_norm_pre, w_mlp_in, w_mlp_out,
           mlp_norm_post):
    batch, seq, d = x.shape
    depth = w_in.shape[0]
    ret_w = w_ret_up.shape[1]
    nsa_w = w_nsa_up.shape[1]
    kvh, dh = NSA_KV_HEADS, NSA_DH
    kv_w = kvh * dh
    n_gate = 3 * NSA_HEADS
    c_qn = 4 * ret_w
    c_kv = c_qn + nsa_w
    c_gate = c_kv + 6 * kv_w
    c_gr = c_gate + n_gate

    w_in_t = jnp.swapaxes(w_in, 1, 2)
    assert n_gate <= LANES
    w_gate_t = jnp.pad(w_in_t[:, c_gate:c_gr, :], ((0, 0), (0, LANES - n_gate), (0, 0))).astype(BF16)
    cmp_pos = jnp.stack([cmp_pos_k, cmp_pos_v], axis=1)
    cmp_w1 = jnp.stack([cmp_w1_k, cmp_w1_v], axis=1).astype(BF16)
    cmp_w2 = jnp.stack([cmp_w2_k, cmp_w2_v], axis=1).astype(BF16)

    xs = x.reshape(batch * seq, d)
    h, gate_logits = _rmsnorm(xs, mix_norm_pre[0], w_gate_t[0])
    for l in range(depth):
        tn_in = math.gcd(c_gate, 1024)
        assert c_qn % tn_in == 0 and c_kv % tn_in == 0
        proj = _matmul_ws(h, w_in_t, l, 0, c_gate + 2 * d, BF16, tn=tn_in, transposed=True,
                          split=c_gate // tn_in, gap=n_gate,
                          col_scale=(c_qn // tn_in, c_kv // tn_in, NSA_DH ** -0.5 * LOG2E), name="proj_in")
        y_ret = _retention(proj, batch, seq, ret_w // RET_HEADS)
        cmp_kv = _compress(proj, c_kv // dh, cmp_pos[l], cmp_w1[l], cmp_w2[l], batch, seq)
        y_nsa = _nsa(proj, c_qn // (NSA_GROUP * dh), (c_kv + 2 * kv_w) // dh, cmp_kv, gate_logits,
                     batch, seq)
        merged = _up_merge(y_ret, y_nsa, proj, c_gate, w_ret_up, w_nsa_up, l)
        mix = _matmul_ws(merged, w_out, l, 0, d, BF16, name="proj_out")
        xs, h = _post(mix, xs, mix_norm_post[l], mlp_norm_pre[l])[:2]
        u, w_mo = _matmul_ws(h, w_mlp_in, l, 0, w_mlp_in.shape[2], BF16, relu2=True, side=w_mlp_out,
                             name="mlp_in")
        y = _matmul_ktiled(u, w_mo, BF16, name="mlp_out")
        if l + 1 < depth:
            xs, h, gate_logits = _post(y, xs, mlp_norm_post[l], mix_norm_pre[l + 1], w_gate_t[l + 1])
        else:
            xs, _ = _post(y, xs, mlp_norm_post[l], None)
    return xs.reshape(batch, seq, d)
```

```python
import functools
import math

import jax
import jax.numpy as jnp
from jax import lax
from jax.experimental import pallas as pl
from jax.experimental.pallas import tpu as pltpu

RET_HEADS = 8
RET_CHUNK = 128
NSA_HEADS = 16
NSA_KV_HEADS = 4
NSA_DH = 128
CMP_BLOCK = 32
CMP_STRIDE = 16
SEL_BLOCK = 64
SEL_TOPK = 16
SEL_LOCAL = 2
WINDOW = 512
EPS = 1e-6
NEG_INF = -1e30
FORCED_SCORE = 1e4

NSA_GROUP = NSA_HEADS // NSA_KV_HEADS
SEL_SHIFT = SEL_BLOCK.bit_length() - 1
LANES = 128
ROW_ALIGN = 16
VMEM_LIMIT = 56 * 1024 * 1024
MAX_STAGE_CHUNKS = 8
LOG2E = math.log2(math.e)
MASK_DIST = 1e32
MASK_POW2 = 2.0 ** 100
SLOPE_PIECES = 3
SEL_LANE0 = 32

F32 = jnp.float32
BF16 = jnp.bfloat16


def _params(*semantics):
    return pltpu.CompilerParams(dimension_semantics=semantics, vmem_limit_bytes=VMEM_LIMIT)


def _dot(a, b):
    return jnp.dot(a, b, preferred_element_type=F32)


def _dot_nt(a, b):
    return lax.dot_general(a, b, (((1,), (1,)), ((), ())), preferred_element_type=F32)


def _dot_tn(a, b):
    return lax.dot_general(a, b, (((0,), (0,)), ((), ())), preferred_element_type=F32)


def _rmsnorm_kernel(x_ref, g_ref, wg_ref, o_ref, gl_ref):
    x = x_ref[...]
    ms = jnp.mean(x * x, axis=-1, keepdims=True)
    h = (x * lax.rsqrt(ms + EPS) * g_ref[...]).astype(o_ref.dtype)
    o_ref[...] = h
    gl_ref[...] = _dot_nt(h, wg_ref[...])


def _rmsnorm(x, gain, w_side_t, tm=256):
    m, d = x.shape
    ns = w_side_t.shape[0]
    return pl.pallas_call(
        _rmsnorm_kernel,
        grid=(m // tm,),
        in_specs=[pl.BlockSpec((tm, d), lambda i: (i, 0)),
                  pl.BlockSpec((1, d), lambda i: (0, 0)),
                  pl.BlockSpec((ns, d), lambda i: (0, 0))],
        out_specs=[pl.BlockSpec((tm, d), lambda i: (i, 0)), pl.BlockSpec((tm, ns), lambda i: (i, 0))],
        out_shape=[jax.ShapeDtypeStruct((m, d), BF16), jax.ShapeDtypeStruct((m, ns), F32)],
        compiler_params=_params("parallel"),
        name="rmsnorm",
    )(x, gain.reshape(1, d), w_side_t)


def _post_kernel(*refs, side):
    if side:
        y_ref, x_ref, gp_ref, gn_ref, wg_ref, xo_ref, ho_ref, gl_ref = refs
    else:
        y_ref, x_ref, gp_ref, gn_ref, xo_ref, ho_ref = refs
    y = y_ref[...].astype(F32)
    yn = y * lax.rsqrt(jnp.mean(y * y, axis=-1, keepdims=True) + EPS) * gp_ref[...]
    xn = x_ref[...] + yn
    xo_ref[...] = xn
    h = (xn * lax.rsqrt(jnp.mean(xn * xn, axis=-1, keepdims=True) + EPS) * gn_ref[...]).astype(ho_ref.dtype)
    ho_ref[...] = h
    if side:
        gl_ref[...] = _dot_nt(h, wg_ref[...])


def _post_last_kernel(y_ref, x_ref, gp_ref, xo_ref):
    y = y_ref[...].astype(F32)
    yn = y * lax.rsqrt(jnp.mean(y * y, axis=-1, keepdims=True) + EPS) * gp_ref[...]
    xo_ref[...] = x_ref[...] + yn


def _post(y, x, g_post, g_next, w_side_t=None, tm=256):
    m, d = x.shape
    row = pl.BlockSpec((tm, d), lambda i: (i, 0))
    vec = pl.BlockSpec((1, d), lambda i: (0, 0))
    if g_next is None:
        return pl.pallas_call(
            _post_last_kernel, grid=(m // tm,),
            in_specs=[row, row, vec], out_specs=row,
            out_shape=jax.ShapeDtypeStruct((m, d), F32),
            compiler_params=_params("parallel"), name="post_last",
        )(y, x, g_post.reshape(1, d)), None
    in_specs, operands = [row, row, vec, vec], [y, x, g_post.reshape(1, d), g_next.reshape(1, d)]
    out_specs = [row, row]
    out_shape = [jax.ShapeDtypeStruct((m, d), F32), jax.ShapeDtypeStruct((m, d), BF16)]
    if w_side_t is not None:
        ns = w_side_t.shape[0]
        in_specs.append(pl.BlockSpec((ns, d), lambda i: (0, 0)))
        operands.append(w_side_t)
        out_specs.append(pl.BlockSpec((tm, ns), lambda i: (i, 0)))
        out_shape.append(jax.ShapeDtypeStruct((m, ns), F32))
    return pl.pallas_call(
        functools.partial(_post_kernel, side=w_side_t is not None), grid=(m // tm,),
        in_specs=in_specs, out_specs=out_specs, out_shape=out_shape,
        compiler_params=_params("parallel"), name="post",
    )(*operands)


def _stage_weights(j, i, n_ch, wq_ref, stage_ref, start, wait):
    ch = stage_ref.shape[1]

    def convert(buf, c, slot):
        wq_ref[buf, pl.ds(pl.multiple_of(c * ch, ch), ch), :] = stage_ref[slot].astype(BF16)

    @pl.when((j == 0) & (i == 0))
    def _():
        start(0, 0, 0)
        if n_ch > 1:
            start(0, 1, 1)
        for c in range(n_ch):
            wait(0, c, c % 2)
            convert(0, c, c % 2)
            if c + 2 < n_ch:
                start(0, c + 2, c % 2)

    has_next = j + 1 < pl.num_programs(0)

    @pl.when(has_next & (i >= 1) & (i <= n_ch))
    def _():
        slot = lax.rem(i - 1, 2)
        wait(j + 1, i - 1, slot)
        convert(lax.rem(j + 1, 2), i - 1, slot)

    @pl.when(has_next & (i < n_ch))
    def _():
        start(j + 1, i, lax.rem(i, 2))


def _stage_chunks(rows, n_i):
    n_ch = min(MAX_STAGE_CHUNKS, n_i - 1)
    while rows % n_ch or (rows // n_ch) % ROW_ALIGN:
        n_ch -= 1
    return n_ch


def _matmul_ws_kernel(*refs, relu2, transposed, layer, row0, n_ch, split, gap, side, col_scale):
    if side:
        a_ref, w_hbm, side_ref, o_ref, side_o_ref, wq_ref, stage_ref, sem = refs
        side_o_ref[...] = side_ref[...].astype(side_o_ref.dtype)
    else:
        a_ref, w_hbm, o_ref, wq_ref, stage_ref, sem = refs
    j, i = pl.program_id(0), pl.program_id(1)
    ch = stage_ref.shape[1]
    tile = wq_ref.shape[1] if transposed else wq_ref.shape[2]

    def chunk_copy(col, c, slot):
        first = row0 + col * tile
        if split is not None:
            first = first + jnp.where(col >= split, gap, 0)
        if transposed:
            src = w_hbm.at[layer, pl.ds(pl.multiple_of(first + c * ch, ROW_ALIGN), ch), :]
        else:
            src = w_hbm.at[layer, pl.ds(c * ch, ch), pl.ds(first, tile)]
        return pltpu.make_async_copy(src, stage_ref.at[slot], sem.at[slot])

    _stage_weights(j, i, n_ch, wq_ref, stage_ref,
                   lambda col, c, slot: chunk_copy(col, c, slot).start(),
                   lambda col, c, slot: chunk_copy(col, c, slot).wait())

    def compute(buf, gate):
        wq = wq_ref[buf]
        acc = _dot_nt(a_ref[...], wq) if transposed else _dot(a_ref[...], wq)
        if relu2:
            acc = jnp.square(jnp.maximum(acc, 0.0))
        if gate:
            acc = jax.nn.sigmoid(acc)
        elif col_scale is not None:
            lo, hi, factor = col_scale
            acc = acc * jnp.where((j >= lo) & (j < hi), factor, 1.0).astype(F32)
        o_ref[...] = acc.astype(o_ref.dtype)

    for buf in range(2):
        if split is None:
            pl.when(lax.rem(j, 2) == buf)(functools.partial(compute, buf, False))
        else:
            pl.when((lax.rem(j, 2) == buf) & (j < split))(functools.partial(compute, buf, False))
            pl.when((lax.rem(j, 2) == buf) & (j >= split))(functools.partial(compute, buf, True))


def _matmul_ws(a, w, layer, col0, n, out_dtype, tm=1024, tn=1024, relu2=False, transposed=False,
               split=None, gap=0, side=None, col_scale=None, name="matmul_ws"):
    m, k = a.shape
    tm, tn = min(tm, m), min(tn, n)
    assert n % tn == 0 and m % tm == 0 and col0 % ROW_ALIGN == 0 and gap % ROW_ALIGN == 0
    n_i = m // tm
    assert n_i >= 2, "the next tile's weights are staged across the row tiles of the current one"
    n_ch = _stage_chunks(tn if transposed else k, n_i)
    wq_shape = (2, tn, k) if transposed else (2, k, tn)
    in_specs = [pl.BlockSpec((tm, k), lambda j, i: (i, 0)), pl.BlockSpec(memory_space=pl.ANY)]
    out_specs = [pl.BlockSpec((tm, tn), lambda j, i: (i, j))]
    out_shape = [jax.ShapeDtypeStruct((m, n), out_dtype)]
    operands = [a, w]
    if side is not None:
        _, rows, cols = side.shape
        steps = (n // tn) * n_i
        sr = rows // steps
        assert rows % steps == 0 and sr % ROW_ALIGN == 0
        in_specs.append(pl.BlockSpec((None, sr, cols), lambda j, i: (layer, j * n_i + i, 0)))
        out_specs.append(pl.BlockSpec((sr, cols), lambda j, i: (j * n_i + i, 0)))
        out_shape.append(jax.ShapeDtypeStruct((rows, cols), BF16))
        operands.append(side)
    out = pl.pallas_call(
        functools.partial(_matmul_ws_kernel, relu2=relu2, transposed=transposed, layer=layer,
                          row0=col0, n_ch=n_ch, split=split, gap=gap, side=side is not None,
                          col_scale=col_scale),
        grid=(n // tn, n_i),
        in_specs=in_specs, out_specs=out_specs, out_shape=out_shape,
        scratch_shapes=[pltpu.VMEM(wq_shape, BF16),
                        pltpu.VMEM((2, wq_shape[1] // n_ch, wq_shape[2]), F32),
                        pltpu.SemaphoreType.DMA((2,))],
        compiler_params=_params("arbitrary", "arbitrary"), name=name,
    )(*operands)
    return out if side is not None else out[0]


def _up_merge_kernel(yr_ref, yn_ref, gr_ref, gn_ref, wr_hbm, wn_hbm, o_ref, wq_ref, stage_ref, sem,
                     *, layer, n_ch):
    j, i = pl.program_id(0), pl.program_id(1)
    ch = stage_ref.shape[1]
    tile = wq_ref.shape[2]
    kr = yr_ref.shape[1]
    half = kr // ch

    def chunk_copy(w_hbm, col, c, slot):
        src = w_hbm.at[layer, pl.ds(pl.multiple_of(c * ch, ROW_ALIGN), ch), pl.ds(col * tile, tile)]
        return pltpu.make_async_copy(src, stage_ref.at[slot], sem.at[slot])

    def both(method):
        def go(col, c, slot):
            if isinstance(c, int):
                w_hbm, cc = (wr_hbm, c) if c < half else (wn_hbm, c - half)
                getattr(chunk_copy(w_hbm, col, cc, slot), method)()
            else:
                pl.when(c < half)(lambda: getattr(chunk_copy(wr_hbm, col, c, slot), method)())
                pl.when(c >= half)(lambda: getattr(chunk_copy(wn_hbm, col, c - half, slot), method)())
        return go

    _stage_weights(j, i, n_ch, wq_ref, stage_ref, both("start"), both("wait"))

    for buf in range(2):
        @pl.when(lax.rem(j, 2) == buf)
        def _():
            a = _dot(yr_ref[...], wq_ref[buf, 0:kr, :])
            b = _dot(yn_ref[...], wq_ref[buf, kr:, :])
            o_ref[...] = (gr_ref[...].astype(F32) * a + gn_ref[...].astype(F32) * b).astype(o_ref.dtype)


def _up_merge(y_ret, y_nsa, gates, g_col, w_ru, w_nu, layer, tm=1024, tn=1024):
    m, kr = y_ret.shape
    kn = y_nsa.shape[1]
    n = w_ru.shape[2]
    tm, tn = min(tm, m), min(tn, n)
    while m // tm < 3 and tm % (2 * ROW_ALIGN) == 0:
        tm //= 2
    n_i, n_j = m // tm, n // tn
    assert m % tm == 0 and n % tn == 0 and n_i >= 3 and kr == kn and g_col % tn == 0
    g_blk = g_col // tn
    n_ch = 2 * _stage_chunks(kr, (n_i + 1) // 2)
    ch = (kr + kn) // n_ch
    return pl.pallas_call(
        functools.partial(_up_merge_kernel, layer=layer, n_ch=n_ch),
        grid=(n_j, n_i),
        in_specs=[pl.BlockSpec((tm, kr), lambda j, i: (i, 0)),
                  pl.BlockSpec((tm, kn), lambda j, i: (i, 0)),
                  pl.BlockSpec((tm, tn), lambda j, i: (i, g_blk + j)),
                  pl.BlockSpec((tm, tn), lambda j, i: (i, g_blk + n_j + j)),
                  pl.BlockSpec(memory_space=pl.ANY), pl.BlockSpec(memory_space=pl.ANY)],
        out_specs=pl.BlockSpec((tm, tn), lambda j, i: (i, j)),
        out_shape=jax.ShapeDtypeStruct((m, n), BF16),
        scratch_shapes=[pltpu.VMEM((2, kr + kn, tn), BF16), pltpu.VMEM((2, ch, tn), F32),
                        pltpu.SemaphoreType.DMA((2,))],
        compiler_params=_params("arbitrary", "arbitrary"), name="up_merge",
    )(y_ret, y_nsa, gates, gates, w_ru, w_nu)


def _matmul_ktiled_kernel(a_ref, w_ref, o_ref, acc_ref):
    kk = pl.program_id(2)

    @pl.when(kk == 0)
    def _():
        acc_ref[...] = jnp.zeros_like(acc_ref)

    acc_ref[...] += _dot(a_ref[...], w_ref[...])

    @pl.when(kk == pl.num_programs(2) - 1)
    def _():
        o_ref[...] = acc_ref[...].astype(o_ref.dtype)


def _matmul_ktiled(a, w, out_dtype, tm=2048, tn=1024, tk=2048, name="matmul_k"):
    m, k = a.shape
    n = w.shape[1]
    tm, tn, tk = min(tm, m), min(tn, n), min(tk, k)
    return pl.pallas_call(
        _matmul_ktiled_kernel,
        grid=(m // tm, n // tn, k // tk),
        in_specs=[pl.BlockSpec((tm, tk), lambda i, j, kk: (i, kk)),
                  pl.BlockSpec((tk, tn), lambda i, j, kk: (kk, j))],
        out_specs=pl.BlockSpec((tm, tn), lambda i, j, kk: (i, j)),
        out_shape=jax.ShapeDtypeStruct((m, n), out_dtype),
        scratch_shapes=[pltpu.VMEM((tm, tn), F32)],
        compiler_params=_params("parallel", "parallel", "arbitrary"), name=name,
    )(a, w)


def _retention_kernel(q_ref, k_ref, v_ref, g_ref, di_ref, dq_ref, dk_ref, dc_ref, o_ref,
                      state_ref, *, n_chunks, chunk, k_scale):
    state_ref[...] = jnp.zeros_like(state_ref)
    hp = state_ref.shape[0]
    dk = state_ref.shape[1]

    def body(n, carry):
        rows = pl.ds(pl.multiple_of(n * chunk, chunk), chunk)
        for hh in range(hp):
            cols = slice(hh * dk, (hh + 1) * dk)
            q = q_ref[rows, cols]
            kf = k_ref[rows, cols].astype(F32) * k_scale
            v = v_ref[rows, cols]
            s = _dot_nt(q, kf.astype(BF16)) * di_ref[hh]
            state = state_ref[hh]
            o = _dot(s.astype(BF16), v) + _dot(q, state.astype(BF16)) * dq_ref[hh]
            state_ref[hh] = state * dc_ref[hh, 0:1, :] + _dot_tn((kf * dk_ref[hh]).astype(BF16), v)
            mu = jnp.mean(o, axis=-1, keepdims=True)
            oc = o - mu
            var = jnp.mean(oc * oc, axis=-1, keepdims=True)
            gate = g_ref[rows, cols].astype(F32)
            y = gate * jax.nn.sigmoid(gate) * (oc * lax.rsqrt(var + EPS))
            o_ref[rows, cols] = y.astype(o_ref.dtype)
        return carry

    lax.fori_loop(0, n_chunks, body, 0, unroll=4)


def _retention(proj, batch, seq, dk, hp=4):
    heads, chunk = RET_HEADS, RET_CHUNK
    assert heads % hp == 0
    dv = dk
    n_chunks = seq // chunk
    log_gamma = jnp.log1p(-jnp.exp2(-5.0 - jnp.arange(heads, dtype=F32)))
    pos = jnp.arange(chunk, dtype=F32)
    rel = pos[:, None] - pos[None, :]
    d_intra = jnp.where(rel >= 0, jnp.exp(log_gamma[:, None, None] * jnp.maximum(rel, 0.0)), 0.0)
    d_query = jnp.broadcast_to(jnp.exp(log_gamma[:, None] * (pos + 1.0))[..., None], (heads, chunk, dv))
    d_key = jnp.broadcast_to(jnp.exp(log_gamma[:, None] * (chunk - 1.0 - pos))[..., None], (heads, chunk, dk))
    d_chunk = jnp.broadcast_to(jnp.exp(log_gamma * chunk)[:, None, None], (heads, 8, dv))

    col = lambda off: pl.BlockSpec((seq, hp * dk), lambda b, h: (b, off // hp + h))
    tab = lambda r, c: pl.BlockSpec((hp, r, c), lambda b, h: (h, 0, 0))
    return pl.pallas_call(
        functools.partial(_retention_kernel, n_chunks=n_chunks, chunk=chunk, k_scale=dk ** -0.5),
        grid=(batch, heads // hp),
        in_specs=[col(0), col(heads), col(2 * heads), col(3 * heads),
                  tab(chunk, chunk), tab(chunk, dv), tab(chunk, dk), tab(8, dv)],
        out_specs=pl.BlockSpec((seq, hp * dv), lambda b, h: (b, h)),
        out_shape=jax.ShapeDtypeStruct((batch * seq, heads * dv), BF16),
        scratch_shapes=[pltpu.VMEM((hp, dk, dv), F32)],
        compiler_params=_params("parallel", "parallel"), name="retention",
    )(proj, proj, proj, proj, d_intra, d_query, d_key, d_chunk)


def _compress_kernel(x_ref, pos_ref, w1_ref, w2_ref, o_ref, x32_ref, *, n_cmp, n_chunks):
    dh = NSA_DH
    half = CMP_BLOCK // 2
    x32_ref[...] = x_ref[...].astype(F32)
    lo = jnp.zeros((n_chunks, w1_ref.shape[1]), F32)
    hi = jnp.zeros((n_chunks, w1_ref.shape[1]), F32)
    for l in range(half):
        xl = x32_ref[pl.ds(l, n_chunks, stride=CMP_STRIDE), :]
        a = (xl + pos_ref[l:l + 1, :]).astype(BF16)
        b = (xl + pos_ref[half + l:half + l + 1, :]).astype(BF16)
        lo = lo + _dot(a, w1_ref[l * dh:(l + 1) * dh, :])
        hi = hi + _dot(b, w1_ref[(half + l) * dh:(half + l + 1) * dh, :])
    pre = lo + pltpu.roll(hi, n_chunks - 1, axis=0)
    hid = pre * jax.nn.sigmoid(pre)
    out = _dot(hid.astype(BF16), w2_ref[...])
    row = lax.broadcasted_iota(jnp.int32, out.shape, 0)
    o_ref[...] = jnp.where(row < n_cmp, out, 0.0).astype(o_ref.dtype)


def _compress(proj, col_blk, pos, w1, w2, batch, seq):
    kvh, dh = NSA_KV_HEADS, NSA_DH
    n_chunks = seq // CMP_STRIDE
    n_cmp = (seq - CMP_BLOCK) // CMP_STRIDE + 1
    assert CMP_BLOCK == 2 * CMP_STRIDE and n_cmp == n_chunks - 1
    hidden = w1.shape[-1]
    return pl.pallas_call(
        functools.partial(_compress_kernel, n_cmp=n_cmp, n_chunks=n_chunks),
        grid=(2, batch, kvh),
        in_specs=[pl.BlockSpec((seq, dh), lambda c, b, g: (b, col_blk + kvh * c + g)),
                  pl.BlockSpec((None, CMP_BLOCK, dh), lambda c, b, g: (c, 0, 0)),
                  pl.BlockSpec((None, CMP_BLOCK * dh, hidden), lambda c, b, g: (c, 0, 0)),
                  pl.BlockSpec((None, hidden, dh), lambda c, b, g: (c, 0, 0))],
        out_specs=pl.BlockSpec((None, None, None, n_chunks, dh), lambda c, b, g: (c, b, g, 0, 0)),
        out_shape=jax.ShapeDtypeStruct((2, batch, kvh, n_chunks, dh), BF16),
        scratch_shapes=[pltpu.VMEM((seq, dh), F32)],
        compiler_params=_params("parallel", "parallel", "parallel"), name="nsa_compress",
    )(proj, pos, w1, w2)


def _nsa_kernel(q_ref, ks_ref, vs_ref, kw_ref, vw_ref, kc_ref, vc_ref, gl_ref, slope_ref, qx_ref,
                ovt_ref, kxs_ref, kxw_ref, band_ref, o_ref, osel_ref, owin_ref, gate_ref,
                *, tq, tk, tw, n_cmp, n_sel, k_top):
    dh = NSA_DH
    grp = NSA_GROUP
    t0 = pl.program_id(2) * tq
    q = q_ref[...]
    q4 = jnp.concatenate([q[:, r * dh:(r + 1) * dh] for r in range(grp)], axis=0)
    slope_row = slope_ref[0:1, :]
    tile4 = lambda a: jnp.concatenate([a] * grp, axis=1)

    kc = kc_ref[...]
    vc = vc_ref[...]
    nc = kc.shape[0]
    n_idx = lax.broadcasted_iota(jnp.int32, (nc, tq), 0)
    t_idx = t0 + lax.broadcasted_iota(jnp.int32, (nc, tq), 1)
    dist_c = t_idx - (CMP_STRIDE * n_idx + (CMP_BLOCK - 1))
    dist_c = jnp.where(n_idx < n_cmp, dist_c, -1)
    dist_c = tile4(jnp.where(dist_c >= 0, dist_c.astype(F32), MASK_DIST))
    s = _dot_nt(kc, q4) - slope_row * dist_c
    m = jnp.max(s, axis=0, keepdims=True)
    e = jnp.where(dist_c < MASK_DIST, jnp.exp2(s - m), 0.0)
    l = jnp.sum(e, axis=0, keepdims=True)
    p = e / jnp.where(l > 0.0, l, 1.0)
    o_cmp = _dot_tn(vc, p.astype(BF16))
    p_sum = p[:, 0:tq]
    for r in range(1, grp):
        p_sum = p_sum + p[:, r * tq:(r + 1) * tq]

    p_hi = p_sum.astype(BF16)
    p_lo = (p_sum - p_hi.astype(F32)).astype(BF16)
    ovt = ovt_ref[...]
    imp = _dot(ovt, p_hi) + _dot(ovt, p_lo)
    blk = lax.broadcasted_iota(jnp.int32, (n_sel, tq), 0)
    t_sel = t0 + lax.broadcasted_iota(jnp.int32, (n_sel, tq), 1)
    back = jnp.right_shift(t_sel, SEL_SHIFT) - blk
    forced = jnp.where(blk == 0, 0, jnp.where(back < 0, SEL_LOCAL, back)) < SEL_LOCAL
    imp = jnp.where(forced, FORCED_SCORE, jnp.where(blk * SEL_BLOCK > t_sel, -1.0, imp))
    rank = jnp.zeros((n_sel, tq), F32)
    for j in range(n_sel):
        cand = imp[j:j + 1, :]
        tie = jnp.where(blk > j, 1.0, 0.0)
        rank = rank + jnp.where(cand > imp, 1.0, jnp.where(cand == imp, tie, 0.0))
    unsel = jnp.where(rank < k_top, 0.0, -MASK_POW2)
    unsel = jnp.concatenate([jnp.zeros((SEL_LANE0, tq), F32), unsel,
                             jnp.zeros((LANES - SEL_LANE0 - n_sel, tq), F32)], axis=0).T
    q_aug = jnp.concatenate(
        [q4, jnp.concatenate([(unsel + qx_ref[r:r + 1, :]).astype(BF16) for r in range(grp)], axis=0)],
        axis=1)

    span = WINDOW + tw

    def window_branch():
        o_parts = []
        for u in range(tq // tw):
            tu = t0 + u * tw
            case = jnp.minimum(tu // tw, WINDOW // tw)
            w0 = pl.multiple_of(jnp.maximum(tu - WINDOW, 0), tw)
            q_u = jnp.concatenate([q_aug[r * tq + u * tw:r * tq + (u + 1) * tw, :] for r in range(grp)], axis=0)
            k_aug = jnp.concatenate([kw_ref[pl.ds(w0, span), :], kxw_ref[pl.ds(w0, span), :]], axis=1)
            s = _dot_nt(k_aug, q_u) + jnp.concatenate([band_ref[case]] * grp, axis=1)
            p = jnp.exp2(s - jnp.max(s, axis=0, keepdims=True))
            o_parts.append(_dot_tn(vw_ref[pl.ds(w0, span), :], p.astype(BF16))
                           / jnp.sum(p, axis=0, keepdims=True))
        return jnp.concatenate([o_parts[u][:, r * tw:(r + 1) * tw]
                                for r in range(grp) for u in range(tq // tw)], axis=1)

    def sel_branch(nk):
        n_chunks = nk // tk

        def scores(c):
            rows = slice(c * tk, (c + 1) * tk)
            k_aug = jnp.concatenate([ks_ref[rows, :], kxs_ref[rows, :]], axis=1)
            s = _dot_nt(k_aug, q_aug)
            if c == n_chunks - 1:
                ahead = (lax.broadcasted_iota(jnp.int32, (tk, tq), 0) + (c * tk - t0)
                         > lax.broadcasted_iota(jnp.int32, (tk, tq), 1))
                s = s + tile4(jnp.where(ahead, NEG_INF, 0.0))
            return s

        m = l = acc = None
        s_next = scores(0)
        for c in range(n_chunks):
            rows = slice(c * tk, (c + 1) * tk)
            s = s_next
            if c + 1 < n_chunks:
                s_next = scores(c + 1)
            m_c = jnp.max(s, axis=0, keepdims=True)
            m_new = m_c if c == 0 else jnp.maximum(m, m_c)
            p = jnp.exp2(s - m_new)
            l_c = jnp.sum(p, axis=0, keepdims=True)
            pv = _dot_tn(vs_ref[rows, :], p.astype(BF16))
            if c == 0:
                l, acc = l_c, pv
            else:
                alpha = jnp.exp2(m - m_new)
                l, acc = alpha * l + l_c, alpha * acc + pv
            m = m_new
        return acc / l

    n_kt = (t0 + tq + tk - 1) // tk
    for n in range(1, ks_ref.shape[0] // tk + 1):
        @pl.when(n_kt == n)
        def _():
            owin_ref[...] = window_branch()
            osel_ref[...] = sel_branch(n * tk)

    o_sel = osel_ref[...]
    o_win = owin_ref[...]

    gate_ref[...] = jax.nn.sigmoid(gl_ref[...].T)
    row0 = 3 * grp * pl.program_id(1)
    for r in range(grp):
        cols = slice(r * tq, (r + 1) * tq)
        out = (gate_ref[pl.ds(row0 + 3 * r, 1), :] * o_cmp[:, cols]
               + gate_ref[pl.ds(row0 + 3 * r + 1, 1), :] * o_sel[:, cols]
               + gate_ref[pl.ds(row0 + 3 * r + 2, 1), :] * o_win[:, cols])
        o_ref[:, r * dh:(r + 1) * dh] = out.T.astype(o_ref.dtype)


def _nsa(proj, q_blk, kv_blk, cmp_kv, gate_logits, batch, seq, tq=512, tk=512, tw=256):
    dh, kvh, grp = NSA_DH, NSA_KV_HEADS, NSA_GROUP
    n_cmp = (seq - CMP_BLOCK) // CMP_STRIDE + 1
    nc = cmp_kv.shape[3]
    n_sel = seq // SEL_BLOCK
    k_top = min(SEL_TOPK, n_sel)
    tw = min(tw, tq)
    span = WINDOW + tw
    n_case = WINDOW // tw + 1
    assert seq % tk == 0 and seq >= span and WINDOW % tw == 0 and tq % tw == 0
    assert tk % tq == 0 and tk % SEL_BLOCK == 0
    assert SEL_LANE0 + n_sel <= LANES and seq <= 256 * 256 and dh == LANES

    slopes = jnp.exp2(-8.0 * (jnp.arange(NSA_HEADS, dtype=F32) + 1.0) / NSA_HEADS).reshape(kvh, grp)
    slope_tab = jnp.broadcast_to(jnp.repeat(slopes * LOG2E, tq, axis=1)[:, None, :], (kvh, 8, grp * tq))
    pieces, rest = [], slopes * LOG2E
    for _ in range(SLOPE_PIECES):
        piece = rest.astype(BF16).astype(F32)
        pieces += [piece, piece]
        rest = rest - piece
    qx = jnp.zeros((kvh, 8, LANES), F32).at[:, :grp, :2 * SLOPE_PIECES].set(jnp.stack(pieces, axis=-1))
    key = jnp.arange(seq)
    key_cols = jnp.stack([(key >> 8) * 256, key & 255] * SLOPE_PIECES, axis=1).astype(F32)
    kxw = jnp.zeros((seq, LANES), F32).at[:, :2 * SLOPE_PIECES].set(key_cols)
    kxs = kxw.at[key, SEL_LANE0 + key // SEL_BLOCK].set(1.0)
    kxs, kxw = kxs.astype(BF16), kxw.astype(BF16)
    cmp_start = CMP_STRIDE * jnp.arange(n_cmp)
    sel_start = SEL_BLOCK * jnp.arange(n_sel)
    overlap = jnp.clip(jnp.minimum(cmp_start[:, None] + CMP_BLOCK, sel_start[None, :] + SEL_BLOCK)
                       - jnp.maximum(cmp_start[:, None], sel_start[None, :]), 0).astype(F32) / CMP_BLOCK
    ovt = jnp.zeros((n_sel, nc), F32).at[:, :n_cmp].set(overlap.T).astype(BF16)
    dist_w = (jnp.arange(n_case)[:, None, None] * tw + jnp.arange(tw)[None, None, :]
              - jnp.arange(span)[None, :, None])
    band = jnp.where((dist_w >= 0) & (dist_w < WINDOW), 0.0, NEG_INF).astype(F32)

    nq = seq // tq
    kv_spec = lambda c: pl.BlockSpec((seq, dh), lambda b, g, i: (b, kv_blk + kvh * c + g))
    cmp_spec = lambda c: pl.BlockSpec((None, None, None, nc, dh), lambda b, g, i: (c, b, g, 0, 0))
    whole = lambda a: pl.BlockSpec(a.shape, lambda b, g, i: (0,) * a.ndim)
    return pl.pallas_call(
        functools.partial(_nsa_kernel, tq=tq, tk=tk, tw=tw, n_cmp=n_cmp, n_sel=n_sel, k_top=k_top),
        grid=(batch, kvh, nq),
        in_specs=[pl.BlockSpec((tq, grp * dh), lambda b, g, i: (b * nq + i, q_blk + g)),
                  kv_spec(0), kv_spec(1), kv_spec(2), kv_spec(3), cmp_spec(0), cmp_spec(1),
                  pl.BlockSpec((tq, LANES), lambda b, g, i: (b * nq + i, 0)),
                  pl.BlockSpec((None, 8, grp * tq), lambda b, g, i: (g, 0, 0)),
                  pl.BlockSpec((None, 8, LANES), lambda b, g, i: (g, 0, 0)),
                  whole(ovt), whole(kxs), whole(kxw), whole(band)],
        out_specs=pl.BlockSpec((tq, grp * dh), lambda b, g, i: (b * nq + i, g)),
        out_shape=jax.ShapeDtypeStruct((batch * seq, kvh * grp * dh), BF16),
        scratch_shapes=[pltpu.VMEM((dh, grp * tq), F32), pltpu.VMEM((dh, grp * tq), F32),
                        pltpu.VMEM((LANES, tq), F32)],
        compiler_params=_params("parallel", "parallel", "arbitrary"), name="nsa_attention",
    )(proj, proj, proj, proj, proj, cmp_kv, cmp_kv, gate_logits, slope_tab, qx, ovt, kxs, kxw, band)


def kernel(x, mix_norm_pre, w_in, cmp_pos_k, cmp_w1_k, cmp_w2_k, cmp_pos_v, cmp_w1_v, cmp_w2_v,
           w_ret_up, w_nsa_up, w_out, mix_norm_post, mlp_norm_pre, w_mlp_in, w_mlp_out,
           mlp_norm_post):
    batch, seq, d = x.shape
    depth = w_in.shape[0]
    ret_w = w_ret_up.shape[1]
    nsa_w = w_nsa_up.shape[1]
    kvh, dh = NSA_KV_HEADS, NSA_DH
    kv_w = kvh * dh
    n_gate = 3 * NSA_HEADS
    c_qn = 4 * ret_w
    c_kv = c_qn + nsa_w
    c_gate = c_kv + 6 * kv_w
    c_gr = c_gate + n_gate

    w_in_t = jnp.swapaxes(w_in, 1, 2)
    assert n_gate <= LANES
    w_gate_t = jnp.pad(w_in_t[:, c_gate:c_gr, :], ((0, 0), (0, LANES - n_gate), (0, 0))).astype(BF16)
    cmp_pos = jnp.stack([cmp_pos_k, cmp_pos_v], axis=1)
    cmp_w1 = jnp.stack([cmp_w1_k, cmp_w1_v], axis=1).astype(BF16)
    cmp_w2 = jnp.stack([cmp_w2_k, cmp_w2_v], axis=1).astype(BF16)

    xs = x.reshape(batch * seq, d)
    h, gate_logits = _rmsnorm(xs, mix_norm_pre[0], w_gate_t[0])
    for l in range(depth):
        tn_in = math.gcd(c_gate, 1024)
        assert c_qn % tn_in == 0 and c_kv % tn_in == 0
        proj = _matmul_ws(h, w_in_t, l, 0, c_gate + 2 * d, BF16, tn=tn_in, transposed=True,
                          split=c_gate // tn_in, gap=n_gate,
                          col_scale=(c_qn // tn_in, c_kv // tn_in, NSA_DH ** -0.5 * LOG2E), name="proj_in")
        y_ret = _retention(proj, batch, seq, ret_w // RET_HEADS)
        cmp_kv = _compress(proj, c_kv // dh, cmp_pos[l], cmp_w1[l], cmp_w2[l], batch, seq)
        y_nsa = _nsa(proj, c_qn // (NSA_GROUP * dh), (c_kv + 2 * kv_w) // dh, cmp_kv, gate_logits,
                     batch, seq)
        merged = _up_merge(y_ret, y_nsa, proj, c_gate, w_ret_up, w_nsa_up, l)
        mix = _matmul_ws(merged, w_out, l, 0, d, BF16, name="proj_out")
        xs, h = _post(mix, xs, mix_norm_post[l], mlp_norm_pre[l])[:2]
        u, w_mo = _matmul_ws(h, w_mlp_in, l, 0, w_mlp_in.shape[2], BF16, relu2=True, side=w_mlp_out,
                             name="mlp_in")
        y = _matmul_ktiled(u, w_mo, BF16, name="mlp_out")
        if l + 1 < depth:
            xs, h, gate_logits = _post(y, xs, mlp_norm_post[l], mix_norm_pre[l + 1], w_gate_t[l + 1])
        else:
            xs, _ = _post(y, xs, mlp_norm_post[l], None)
    return xs.reshape(batch, seq, d)
```
